```python
import jax, jax.numpy as jnp
from jax import lax
import numpy as np

D_MODEL = 1024
BATCH = 8
SEQ = 4096
DEPTH = 1

CHUNK = 64
PLE_DIM = 256
GLA_HEADS = 4
GLA_DK = 128
GLA_DV = 256
GLA_GATE_RANK = 16
GLA_GATE_TAU = 16.0
GLA_QK = GLA_HEADS * GLA_DK
GLA_V = GLA_HEADS * GLA_DV
ATT_HEADS = 16
ATT_DH = 64
ATT_W = ATT_HEADS * ATT_DH
ATT_PAST_CHUNKS = 8
REL_CLIP = 256
N_EXPERTS = 32
TOP_K = 4
D_EXPERT = D_MODEL
SWIGLU_ALPHA = 1.702
SWIGLU_LIMIT = 7.0
EPS = 1e-6
IN_SPLITS = (GLA_QK, GLA_QK, GLA_V, GLA_GATE_RANK, GLA_V, ATT_W, ATT_W, ATT_W, D_MODEL, D_MODEL)
D_IN = 2 * GLA_QK + 2 * GLA_V + GLA_GATE_RANK + 3 * ATT_W + 2 * D_MODEL

kernel_name = 'hybrid_gla_bandattn_moe_ple_block'


def rmsnorm(x, g):
    xf = x.astype(jnp.float32)
    y = xf * lax.rsqrt(jnp.mean(xf * xf, axis=-1, keepdims=True) + EPS)
    return (y * g.astype(jnp.float32)).astype(x.dtype)


def gla_chunked(q, k, v, g_log):
    B, T, H, dk = q.shape
    dv = v.shape[-1]
    n = T // CHUNK

    def chunks(a):
        return a.astype(jnp.float32).reshape(B, n, CHUNK, H, a.shape[-1]).transpose(1, 0, 3, 2, 4)

    qc = chunks(q) * (dk ** -0.5)
    kc, vc, gc = chunks(k), chunks(v), chunks(g_log)
    causal = jnp.tril(jnp.ones((CHUNK, CHUNK), dtype=bool))[:, :, None]

    def step(S, inp):
        qi, ki, vi, gi = inp
        b = jnp.cumsum(gi, axis=2)
        diff = b[:, :, :, None, :] - b[:, :, None, :, :]
        decay = jnp.exp(jnp.where(causal, diff, -jnp.inf))
        scores = jnp.einsum('bhtsd,bhsd->bhts', qi[:, :, :, None, :] * decay, ki)
        o = (jnp.einsum('bhts,bhsv->bhtv', scores, vi)
             + jnp.einsum('bhtd,bhdv->bhtv', qi * jnp.exp(b), S))
        b_last = b[:, :, -1:, :]
        S = (jnp.exp(b_last[:, :, 0, :, None]) * S
             + jnp.einsum('bhsd,bhsv->bhdv', ki * jnp.exp(b_last - b), vi))
        return S, o

    S0 = jnp.zeros((B, H, dk, dv), jnp.float32)
    _, o = lax.scan(step, S0, (qc, kc, vc, gc))
    return o.transpose(1, 0, 3, 2, 4).reshape(B, T, H, dv)


def chunk_band_attention(q, k, v, rel_bias):
    B, T, H, dh = q.shape
    n = T // CHUNK
    pad = ATT_PAST_CHUNKS * CHUNK
    band = pad + CHUNK
    kp = jnp.pad(k, ((0, 0), (pad, 0), (0, 0), (0, 0)))
    vp = jnp.pad(v, ((0, 0), (pad, 0), (0, 0), (0, 0)))
    qc = q.reshape(B, n, CHUNK, H, dh).transpose(1, 0, 2, 3, 4)
    q_pos = jnp.arange(CHUNK) + pad
    k_pos = jnp.arange(band)
    rel = jnp.clip(q_pos[:, None] - k_pos[None, :], -REL_CLIP, REL_CLIP) + REL_CLIP
    bias = rel_bias.astype(jnp.float32)[:, rel]
    scale = dh ** -0.5

    def one_chunk(args):
        c, qi = args
        start = c * CHUNK
        ki = lax.dynamic_slice_in_dim(kp, start, band, axis=1)
        vi = lax.dynamic_slice_in_dim(vp, start, band, axis=1)
        s = jnp.einsum('bqhd,bkhd->bhqk', qi, ki).astype(jnp.float32) * scale + bias
        valid = (start + k_pos) >= pad
        s = jnp.where(valid[None, None, None, :], s, -jnp.inf)
        w = jax.nn.softmax(s, axis=-1)
        return jnp.einsum('bhqk,bkhd->bqhd', w.astype(vi.dtype), vi)

    o = lax.map(one_chunk, (jnp.arange(n), qc))
    return o.transpose(1, 0, 2, 3, 4).reshape(B, T, H * dh)


def clamped_swiglu(h):
    x_glu = jnp.minimum(h[..., ::2], SWIGLU_LIMIT)
    x_lin = jnp.clip(h[..., 1::2], -SWIGLU_LIMIT, SWIGLU_LIMIT)
    return x_glu * jax.nn.sigmoid(SWIGLU_ALPHA * x_glu) * (x_lin + 1.0)


def moe_ffn(x, w_router, b_router, w1, b1, w2, b2):
    logits = (x @ w_router + b_router).astype(jnp.float32)
    top_vals, top_idx = lax.top_k(logits, TOP_K)
    top_w = jax.nn.softmax(top_vals, axis=-1)
    gates = jnp.einsum('nk,nke->ne', top_w, jax.nn.one_hot(top_idx, N_EXPERTS, dtype=jnp.float32))
    out = jnp.zeros(x.shape, jnp.float32)
    for e in range(N_EXPERTS):
        h = x @ w1[e] + b1[e]
        y = clamped_swiglu(h) @ w2[e] + b2[e]
        out = out + gates[:, e:e + 1] * y.astype(jnp.float32)
    return out.astype(x.dtype)


def setup_inputs(seed: int = 0) -> dict:
    key = jax.random.key(seed)
    ks = jax.random.split(key, 20)

    def nrm(k, shape, scale):
        return jax.random.normal(k, shape, jnp.float32) * scale

    def gain(k, shape):
        return 1.0 + nrm(k, shape, 0.05)

    D, E, F = D_MODEL, N_EXPERTS, D_EXPERT
    return {
        'x': nrm(ks[0], (BATCH, SEQ, D), 1.0),
        'p': nrm(ks[1], (DEPTH, BATCH, SEQ, PLE_DIM), 1.0),
        'ln_mix': gain(ks[2], (DEPTH, D)),
        'w_in': nrm(ks[3], (DEPTH, D, D_IN), D ** -0.5),
        'w_gk': nrm(ks[4], (DEPTH, GLA_GATE_RANK, GLA_QK), GLA_GATE_RANK ** -0.5),
        'b_gk': nrm(ks[5], (DEPTH, GLA_QK), 0.1),
        'gla_norm': gain(ks[6], (DEPTH, GLA_DV)),
        'rel_bias': nrm(ks[7], (DEPTH, ATT_HEADS, 2 * REL_CLIP + 1), 0.1),
        'w_out': nrm(ks[8], (DEPTH, D, D), D ** -0.5),
        'ln_moe': gain(ks[9], (DEPTH, D)),
        'w_router': nrm(ks[10], (DEPTH, D, E), D ** -0.5),
        'b_router': nrm(ks[11], (DEPTH, E), 0.01),
        'w1': nrm(ks[12], (DEPTH, E, D, 2 * F), D ** -0.5),
        'b1': nrm(ks[13], (DEPTH, E, 2 * F), 0.01),
        'w2': nrm(ks[14], (DEPTH, E, F, D), F ** -0.5),
        'b2': nrm(ks[15], (DEPTH, E, D), 0.01),
        'ln_ple': gain(ks[16], (DEPTH, D)),
        'w_ple_gate': nrm(ks[17], (DEPTH, D, D), D ** -0.5),
        'w_ple_proj': nrm(ks[18], (DEPTH, PLE_DIM, D), PLE_DIM ** -0.5),
        'ln_final': gain(ks[19], (D,)),
    }


def reference(x, p, ln_mix, w_in, w_gk, b_gk, gla_norm, rel_bias, w_out, ln_moe, w_router,
              b_router, w1, b1, w2, b2, ln_ple, w_ple_gate, w_ple_proj, ln_final):
    B, T, D = x.shape
    splits = np.cumsum(IN_SPLITS)[:-1].tolist()
    for i in range(DEPTH):
        xn = rmsnorm(x, ln_mix[i])
        z = xn @ w_in[i]
        q_g, k_g, v_g, gk_low, r_g, q_a, k_a, v_a, gt_a, gt_b = jnp.split(z, splits, axis=-1)

        g_log = jax.nn.log_sigmoid((gk_low @ w_gk[i] + b_gk[i]).astype(jnp.float32)) / GLA_GATE_TAU
        o_g = gla_chunked(q_g.reshape(B, T, GLA_HEADS, GLA_DK),
                          k_g.reshape(B, T, GLA_HEADS, GLA_DK),
                          v_g.reshape(B, T, GLA_HEADS, GLA_DV),
                          g_log.reshape(B, T, GLA_HEADS, GLA_DK))
        o_g = rmsnorm(o_g, gla_norm[i]) * jax.nn.silu(r_g.reshape(B, T, GLA_HEADS, GLA_DV).astype(jnp.float32))
        y_a = o_g.reshape(B, T, GLA_V).astype(x.dtype)

        y_b = chunk_band_attention(q_a.reshape(B, T, ATT_HEADS, ATT_DH),
                                   k_a.reshape(B, T, ATT_HEADS, ATT_DH),
                                   v_a.reshape(B, T, ATT_HEADS, ATT_DH),
                                   rel_bias[i])

        h = jax.nn.sigmoid(gt_a) * y_a + jax.nn.sigmoid(gt_b) * y_b
        x = x + h @ w_out[i]

        xm = rmsnorm(x, ln_moe[i]).reshape(B * T, D)
        x = x + moe_ffn(xm, w_router[i], b_router[i], w1[i], b1[i], w2[i], b2[i]).reshape(B, T, D)

        ple_gate = jax.nn.sigmoid(rmsnorm(x, ln_ple[i]) @ w_ple_gate[i])
        x = x + ple_gate * (p[i] @ w_ple_proj[i])
    return rmsnorm(x, ln_final)
```

```python
import functools

import numpy as np
import jax
import jax.numpy as jnp
from jax import lax
from jax.experimental import pallas as pl
from jax.experimental.pallas import tpu as pltpu

F32 = jnp.float32
BF16 = jnp.bfloat16
I32 = jnp.int32

LANES = 128
CHUNK = 64
GLA_HEADS = 4
GLA_DK = 128
GLA_DV = 256
GLA_RANK = 16
GLA_TAU = 16.0
ATT_HEADS = 16
ATT_DH = 64
ATT_PAST = 8
REL_CLIP = 256
N_EXPERTS = 32
TOP_K = 4
SWIGLU_ALPHA = 1.702
SWIGLU_LIMIT = 7.0
EPS = 1e-6

VMEM_LIMIT = 48 * 1024 * 1024

IN_TM, IN_TN = 1024, 1024
GLA_TT = 128
ATT_TQ = 256
OUT_TM = 512
RANK_TB = 512
EXP_TM = 256


def _cparams(sem):
    return pltpu.CompilerParams(dimension_semantics=sem, vmem_limit_bytes=VMEM_LIMIT)


def _split_bf16(a):
    hi = a.astype(BF16)
    lo = (a - hi.astype(F32)).astype(BF16)
    return hi, lo


def _dot_split(a, b):
    a_hi, a_lo = _split_bf16(a)
    b_hi, b_lo = _split_bf16(b)
    d = functools.partial(jnp.dot, preferred_element_type=F32)
    return d(a_hi, b_hi) + (d(a_hi, b_lo) + d(a_lo, b_hi))


def _rms(x, gain):
    ms = jnp.mean(x * x, axis=-1, keepdims=True)
    return x * lax.rsqrt(ms + EPS) * gain


_NT = (((1,), (1,)), ((), ()))
_TN = (((0,), (0,)), ((), ()))


def _in_proj_kernel(x_ref, g_ref, w_ref, wlow_ref, z_ref, gk_ref, xn_ref):
    @pl.when(pl.program_id(1) == 0)
    def _():
        xn = _rms(x_ref[...], g_ref[...]).astype(BF16)
        xn_ref[...] = xn
        gk_ref[...] = jnp.dot(xn, wlow_ref[...], preferred_element_type=F32)

    z_ref[...] = jnp.dot(xn_ref[...], w_ref[...], preferred_element_type=F32).astype(BF16)


def _in_proj(x2, ln, w_main, w_low):
    n, d = x2.shape
    ncol = w_main.shape[1]
    return pl.pallas_call(
        _in_proj_kernel,
        grid=(n // IN_TM, ncol // IN_TN),
        in_specs=[
            pl.BlockSpec((IN_TM, d), lambda i, j: (i, 0)),
            pl.BlockSpec((1, d), lambda i, j: (0, 0)),
            pl.BlockSpec((d, IN_TN), lambda i, j: (0, j)),
            pl.BlockSpec((d, LANES), lambda i, j: (0, 0)),
        ],
        out_specs=[
            pl.BlockSpec((IN_TM, IN_TN), lambda i, j: (i, j)),
            pl.BlockSpec((IN_TM, LANES), lambda i, j: (i, 0)),
        ],
        out_shape=[
            jax.ShapeDtypeStruct((n, ncol), BF16),
            jax.ShapeDtypeStruct((n, LANES), F32),
        ],
        scratch_shapes=[pltpu.VMEM((IN_TM, d), BF16)],
        compiler_params=_cparams(("parallel", "arbitrary")),
        name="in_proj",
    )(x2, ln, w_main, w_low)


_GLA_LEVELS = (8, 16, 32)
_GLA_BOT = 8


def _gla_consts(tt):
    t = np.arange(tt)
    same_chunk = (t[:, None] // CHUNK) == (t[None, :] // CHUNK)
    tri = (same_chunk & (t[None, :] <= t[:, None])).astype(np.float32)
    mlev = []
    for h in _GLA_LEVELS:
        blk = (t[:, None] // (2 * h)) == (t[None, :] // (2 * h))
        m = blk & ((t[:, None] % (2 * h)) >= h) & ((t[None, :] % (2 * h)) < h)
        mlev.append(m.astype(np.float32))
    mbot = []
    for s in range(_GLA_BOT):
        m = (t[None, :] == (t[:, None] // _GLA_BOT) * _GLA_BOT + s) & ((t[:, None] % _GLA_BOT) >= s)
        mbot.append(m.astype(np.float32))
    return (jnp.asarray(tri, BF16), jnp.asarray(np.stack(mlev), F32),
            jnp.asarray(np.stack(mbot), F32), jnp.ones((GLA_DK, tt), BF16))


def _gla_kernel(q_ref, k_ref, v_ref, r_ref, ga_ref, gk_ref, wgk_ref, bgk_ref, gn_ref,
                tri_ref, mlev_ref, mbot_ref, ones_ref, o_ref, st_ref):
    tt = q_ref.shape[0]

    @pl.when(pl.program_id(2) == 0)
    def _():
        st_ref[...] = jnp.zeros_like(st_ref)

    q = q_ref[...].astype(F32) * (GLA_DK ** -0.5)
    k = k_ref[...].astype(F32)
    v = v_ref[...]

    xg = _dot_split(gk_ref[...], wgk_ref[...]) + bgk_ref[...]
    g = -(jnp.maximum(-xg, 0.0) + jnp.log(1.0 + jnp.exp(-jnp.abs(xg)))) * (1.0 / GLA_TAU)

    g_hi, g_lo = _split_bf16(g)
    tri = tri_ref[...]
    b = (jnp.dot(tri, g_hi, preferred_element_type=F32)
         + jnp.dot(tri, g_lo, preferred_element_type=F32))

    s_intra = jnp.zeros((tt, tt), F32)
    for li, h in enumerate(_GLA_LEVELS):
        b3 = b.reshape(tt // (2 * h), 2 * h, GLA_DK)
        bm = b3[:, h - 1:h, :]
        eq = jnp.exp(jnp.minimum(b3 - bm, 0.0)).reshape(tt, GLA_DK)
        ek = jnp.exp(jnp.minimum(bm - b3, 0.0)).reshape(tt, GLA_DK)
        sc = lax.dot_general((q * eq).astype(BF16), (k * ek).astype(BF16), _NT,
                             preferred_element_type=F32)
        s_intra = s_intra + sc * mlev_ref[li]
    nb = tt // _GLA_BOT
    b3 = b.reshape(nb, _GLA_BOT, GLA_DK)
    q3 = q.reshape(nb, _GLA_BOT, GLA_DK)
    k3 = k.reshape(nb, _GLA_BOT, GLA_DK)
    ones = ones_ref[...]
    for s in range(_GLA_BOT):
        e = jnp.exp(jnp.minimum(b3 - b3[:, s:s + 1, :], 0.0))
        a = (q3 * e * k3[:, s:s + 1, :]).reshape(tt, GLA_DK).astype(BF16)
        s_intra = s_intra + jnp.dot(a, ones, preferred_element_type=F32) * mbot_ref[s]
    o_intra = jnp.dot(s_intra.astype(BF16), v, preferred_element_type=F32)

    st = st_ref[...]
    outs = []
    for c in range(tt // CHUNK):
        lo = c * CHUNK
        bc = b[lo:lo + CHUNK]
        bl = b[lo + CHUNK - 1:lo + CHUNK]
        qe = (q[lo:lo + CHUNK] * jnp.exp(bc)).astype(BF16)
        outs.append(lax.dot_general(qe, st.astype(BF16), _NT, preferred_element_type=F32))
        kd = (k[lo:lo + CHUNK] * jnp.exp(bl - bc)).astype(BF16)
        upd = lax.dot_general(v[lo:lo + CHUNK], kd, _TN, preferred_element_type=F32)
        st = st * jnp.exp(bl) + upd
    st_ref[...] = st
    o = o_intra + jnp.concatenate(outs, axis=0)

    r = r_ref[...].astype(F32)
    ya = _rms(o, gn_ref[...]) * (r * jax.nn.sigmoid(r))
    o_ref[...] = (jax.nn.sigmoid(ga_ref[...].astype(F32)) * ya).astype(BF16)


def _gla(z3, gk3, wgk_pad, bgk, gnorm, col):
    b, t, _ = z3.shape
    tt = GLA_TT
    tri, mlev, mbot, ones = _gla_consts(tt)

    def zspec(width, off):
        return pl.BlockSpec((None, tt, width), lambda bi, h, ti, o=off // width: (bi, ti, o + h))

    const2 = lambda bi, h, ti: (0, 0)
    const3 = lambda bi, h, ti: (0, 0, 0)
    return pl.pallas_call(
        _gla_kernel,
        grid=(b, GLA_HEADS, t // tt),
        in_specs=[
            zspec(GLA_DK, col["q_g"]), zspec(GLA_DK, col["k_g"]), zspec(GLA_DV, col["v_g"]),
            zspec(GLA_DV, col["r_g"]), zspec(GLA_DV, col["gt_a"]),
            pl.BlockSpec((None, tt, LANES), lambda bi, h, ti: (bi, ti, 0)),
            pl.BlockSpec((LANES, GLA_DK), lambda bi, h, ti: (0, h)),
            pl.BlockSpec((1, GLA_DK), lambda bi, h, ti: (0, h)),
            pl.BlockSpec((1, GLA_DV), const2),
            pl.BlockSpec((tt, tt), const2),
            pl.BlockSpec((len(_GLA_LEVELS), tt, tt), const3),
            pl.BlockSpec((_GLA_BOT, tt, tt), const3),
            pl.BlockSpec((GLA_DK, tt), const2),
        ],
        out_specs=pl.BlockSpec((None, tt, GLA_DV), lambda bi, h, ti: (bi, ti, h)),
        out_shape=jax.ShapeDtypeStruct((b, t, GLA_HEADS * GLA_DV), BF16),
        scratch_shapes=[pltpu.VMEM((GLA_DV, GLA_DK), F32)],
        compiler_params=_cparams(("parallel", "parallel", "arbitrary")),
        name="gla",
    )(z3, z3, z3, z3, z3, gk3, wgk_pad, bgk, gnorm, tri, mlev, mbot, ones)


_ATT_NKB = 3


def _attn_bias(rel_bias):
    tq = ATT_TQ
    t = np.arange(tq)[:, None]
    w = np.arange(_ATT_NKB * tq)[None, :]
    s = w - (_ATT_NKB - 1) * tq
    rel = np.clip(t - s, -REL_CLIP, REL_CLIP) + REL_CLIP
    dc = t // CHUNK - np.floor_divide(s, CHUNK)
    valid = (dc >= 0) & (dc <= ATT_PAST)
    tab = rel_bias.astype(F32)[:, rel]
    return jnp.where(jnp.asarray(valid)[None], tab, -jnp.inf)


def _attn_kernel(q_ref, k0_ref, k1_ref, k2_ref, v0_ref, v1_ref, v2_ref, gb_ref, bias_ref, o_ref):
    j = pl.program_id(2)
    tq = q_ref.shape[0]
    q = q_ref[...]
    ks = (k0_ref[...], k1_ref[...], k2_ref[...])
    vs = (v0_ref[...], v1_ref[...], v2_ref[...])
    lane = lax.broadcasted_iota(I32, (1, LANES), 1)
    scale = ATT_DH ** -0.5
    heads = []
    for hh in range(LANES // ATT_DH):
        in_head = (lane // ATT_DH) == hh
        qh = jnp.where(in_head, q, jnp.zeros_like(q))
        parts = []
        for i in range(_ATT_NKB):
            s = lax.dot_general(qh, ks[i], _NT, preferred_element_type=F32)
            s = s * scale + bias_ref[hh, :, i * tq:(i + 1) * tq]
            if i < _ATT_NKB - 1:
                s = s + jnp.where(j - (_ATT_NKB - 1) + i >= 0, 0.0, -jnp.inf)
            parts.append(s)
        m = functools.reduce(jnp.maximum, [jnp.max(p, axis=-1, keepdims=True) for p in parts])
        ps = [jnp.exp(p - m) for p in parts]
        l = functools.reduce(jnp.add, [jnp.sum(p, axis=-1, keepdims=True) for p in ps])
        pv = functools.reduce(
            jnp.add, [jnp.dot(p.astype(BF16), vv, preferred_element_type=F32) for p, vv in zip(ps, vs)])
        heads.append(pv / l)
    o = jnp.where((lane // ATT_DH) == 0, heads[0], heads[1])
    o_ref[...] = (jax.nn.sigmoid(gb_ref[...].astype(F32)) * o).astype(BF16)


def _attn(z3, bias, col):
    b, t, _ = z3.shape
    tq = ATT_TQ
    hp = ATT_HEADS * ATT_DH // LANES

    def cur(off):
        return pl.BlockSpec((None, tq, LANES), lambda h, bi, j, o=off // LANES: (bi, j, o + h))

    def past(off, back):
        return pl.BlockSpec((None, tq, LANES),
                            lambda h, bi, j, o=off // LANES: (bi, jnp.maximum(j - back, 0), o + h))

    return pl.pallas_call(
        _attn_kernel,
        grid=(hp, b, t // tq),
        in_specs=[
            cur(col["q_a"]),
            past(col["k_a"], 2), past(col["k_a"], 1), cur(col["k_a"]),
            past(col["v_a"], 2), past(col["v_a"], 1), cur(col["v_a"]),
            cur(col["gt_b"]),
            pl.BlockSpec((LANES // ATT_DH, tq, _ATT_NKB * tq), lambda h, bi, j: (h, 0, 0)),
        ],
        out_specs=pl.BlockSpec((None, tq, LANES), lambda h, bi, j: (bi, j, h)),
        out_shape=jax.ShapeDtypeStruct((b, t, ATT_HEADS * ATT_DH), BF16),
        compiler_params=_cparams(("parallel", "parallel", "parallel")),
        name="band_attn",
    )(z3, z3, z3, z3, z3, z3, z3, z3, bias)


def _outproj_kernel(x_ref, ya_ref, yb_ref, wo_ref, lnm_ref, wr_ref, br_ref,
                    x1_ref, xm_ref, topi_ref, topw_ref):
    h = (ya_ref[...].astype(F32) + yb_ref[...].astype(F32)).astype(BF16)
    x1 = x_ref[...] + jnp.dot(h, wo_ref[...], preferred_element_type=F32)
    x1_ref[...] = x1
    xm = _rms(x1, lnm_ref[...])
    xm_ref[...] = xm

    tm = xm.shape[0]
    lane = lax.broadcasted_iota(I32, (tm, LANES), 1)
    lanef = lane.astype(F32)
    logits = _dot_split(xm, wr_ref[...]) + br_ref[...]
    l = jnp.where(lane < N_EXPERTS, logits, -jnp.inf)
    vals, idxs = [], []
    for _ in range(TOP_K):
        m = jnp.max(l, axis=-1, keepdims=True)
        idx = jnp.min(jnp.where(l == m, lanef, float(LANES)), axis=-1, keepdims=True)
        vals.append(m)
        idxs.append(idx)
        l = jnp.where(lanef == idx, -jnp.inf, l)
    es = [jnp.exp(vv - vals[0]) for vv in vals]
    tot = functools.reduce(jnp.add, es)
    topw = jnp.zeros((tm, LANES), F32)
    topi = jnp.zeros((tm, LANES), F32)
    for kk in range(TOP_K):
        topw = jnp.where(lane == kk, es[kk] / tot, topw)
        topi = jnp.where(lane == kk, idxs[kk], topi)
    topw_ref[...] = topw
    topi_ref[...] = topi.astype(I32)


def _outproj(x2, ya, yb, w_out, ln_moe, wr_pad, br_pad):
    n, d = x2.shape
    tm = OUT_TM
    row = lambda i: (i, 0)
    const = lambda i: (0, 0)
    return pl.pallas_call(
        _outproj_kernel,
        grid=(n // tm,),
        in_specs=[
            pl.BlockSpec((tm, d), row), pl.BlockSpec((tm, d), row), pl.BlockSpec((tm, d), row),
            pl.BlockSpec((d, d), const), pl.BlockSpec((1, d), const),
            pl.BlockSpec((d, LANES), const), pl.BlockSpec((1, LANES), const),
        ],
        out_specs=[
            pl.BlockSpec((tm, d), row), pl.BlockSpec((tm, d), row),
            pl.BlockSpec((tm, LANES), row), pl.BlockSpec((tm, LANES), row),
        ],
        out_shape=[
            jax.ShapeDtypeStruct((n, d), F32), jax.ShapeDtypeStruct((n, d), F32),
            jax.ShapeDtypeStruct((n, LANES), I32), jax.ShapeDtypeStruct((n, LANES), F32),
        ],
        compiler_params=_cparams(("parallel",)),
        name="out_proj_router",
    )(x2, ya, yb, w_out, ln_moe, wr_pad, br_pad)


def _rank_kernel(topi_ref, tri_ref, rank_ref, cnt_ref, carry_ref):
    @pl.when(pl.program_id(0) == 0)
    def _():
        carry_ref[...] = jnp.zeros_like(carry_ref)

    ti = topi_ref[...]
    tb = ti.shape[0]
    lane = lax.broadcasted_iota(I32, (tb, LANES), 1)
    sel = [lane == ti[:, kk:kk + 1] for kk in range(TOP_K)]
    oh = functools.reduce(jnp.add, [jnp.where(s, 1.0, 0.0) for s in sel])
    rank = carry_ref[...] + jnp.dot(tri_ref[...], oh.astype(BF16), preferred_element_type=F32)
    out = jnp.zeros((tb, LANES), F32)
    for kk in range(TOP_K):
        rk = jnp.sum(jnp.where(sel[kk], rank, 0.0), axis=-1, keepdims=True)
        out = jnp.where(lane == kk, rk, out)
    rank_ref[...] = out.astype(I32)
    carry_ref[...] = carry_ref[...] + jnp.sum(oh, axis=0, keepdims=True)
    cnt_ref[...] = carry_ref[...].astype(I32)


def _ranks(topi):
    n = topi.shape[0]
    tb = RANK_TB
    t = np.arange(tb)
    tri = jnp.asarray((t[None, :] < t[:, None]).astype(np.float32), BF16)
    return pl.pallas_call(
        _rank_kernel,
        grid=(n // tb,),
        in_specs=[pl.BlockSpec((tb, LANES), lambda i: (i, 0)),
                  pl.BlockSpec((tb, tb), lambda i: (0, 0))],
        out_specs=[pl.BlockSpec((tb, LANES), lambda i: (i, 0)),
                   pl.BlockSpec((1, LANES), lambda i: (0, 0))],
        out_shape=[jax.ShapeDtypeStruct((n, LANES), I32), jax.ShapeDtypeStruct((1, LANES), I32)],
        scratch_shapes=[pltpu.VMEM((1, LANES), F32)],
        compiler_params=_cparams(("arbitrary",)),
        name="route_rank",
    )(topi, tri)


def _expert_kernel(te_ref, tv_ref, xs_ref, w1g_ref, w1l_ref, b1g_ref, b1l_ref, w2_ref, b2_ref, ys_ref):
    t = pl.program_id(0)
    nvalid = tv_ref[t]

    @pl.when(nvalid > 0)
    def _():
        tm = xs_ref.shape[0]
        rows = lax.broadcasted_iota(I32, (tm, 1), 0)
        x = jnp.where(rows < nvalid, xs_ref[...], 0.0).astype(BF16)
        hg = jnp.dot(x, w1g_ref[...], preferred_element_type=F32) + b1g_ref[...]
        hl = jnp.dot(x, w1l_ref[...], preferred_element_type=F32) + b1l_ref[...]
        glu = jnp.minimum(hg, SWIGLU_LIMIT)
        lin = jnp.clip(hl, -SWIGLU_LIMIT, SWIGLU_LIMIT)
        act = glu * jax.nn.sigmoid(SWIGLU_ALPHA * glu) * (lin + 1.0)
        ys_ref[...] = jnp.dot(act.astype(BF16), w2_ref[...], preferred_element_type=F32) + b2_ref[...]

    @pl.when(nvalid <= 0)
    def _():
        ys_ref[...] = jnp.zeros_like(ys_ref)


def _experts(tile_expert, tile_valid, xs, w1g, w1l, b1g, b1l, w2, b2):
    p, d = xs.shape
    f = w1g.shape[2]
    tm = EXP_TM
    wmap = lambda t, te, tv: (te[t], 0, 0)
    return pl.pallas_call(
        _expert_kernel,
        grid_spec=pltpu.PrefetchScalarGridSpec(
            num_scalar_prefetch=2,
            grid=(p // tm,),
            in_specs=[
                pl.BlockSpec((tm, d), lambda t, te, tv: (t, 0)),
                pl.BlockSpec((None, d, f), wmap), pl.BlockSpec((None, d, f), wmap),
                pl.BlockSpec((None, 1, f), wmap), pl.BlockSpec((None, 1, f), wmap),
                pl.BlockSpec((None, f, d), wmap), pl.BlockSpec((None, 1, d), wmap),
            ],
            out_specs=pl.BlockSpec((tm, d), lambda t, te, tv: (t, 0)),
        ),
        out_shape=jax.ShapeDtypeStruct((p, d), F32),
        compiler_params=_cparams(("arbitrary",)),
        name="experts",
    )(tile_expert, tile_valid, xs, w1g, w1l, b1g, b1l, w2, b2)


def _ple_kernel(x1_ref, moe_ref, p_ref, lnp_ref, wpg_ref, wpp_ref, lnf_ref, o_ref):
    x2 = x1_ref[...] + moe_ref[...]
    gate = jax.nn.sigmoid(jnp.dot(_rms(x2, lnp_ref[...]).astype(BF16), wpg_ref[...],
                                  preferred_element_type=F32))
    proj = jnp.dot(p_ref[...].astype(BF16), wpp_ref[...], preferred_element_type=F32)
    o_ref[...] = _rms(x2 + gate * proj, lnf_ref[...])


def _ple(x1, moe, p2, ln_ple, wpg, wpp, ln_final):
    n, d = x1.shape
    pd = p2.shape[1]
    tm = OUT_TM
    row = lambda i: (i, 0)
    const = lambda i: (0, 0)
    return pl.pallas_call(
        _ple_kernel,
        grid=(n // tm,),
        in_specs=[
            pl.BlockSpec((tm, d), row), pl.BlockSpec((tm, d), row), pl.BlockSpec((tm, pd), row),
            pl.BlockSpec((1, d), const), pl.BlockSpec((d, d), const),
            pl.BlockSpec((pd, d), const), pl.BlockSpec((1, d), const),
        ],
        out_specs=pl.BlockSpec((tm, d), row),
        out_shape=jax.ShapeDtypeStruct((n, d), F32),
        compiler_params=_cparams(("parallel",)),
        name="ple_final",
    )(x1, moe, p2, ln_ple, wpg, wpp, ln_final)


def _layer(x2, p2, bsz, seq, ln_mix, w_in, w_gk, b_gk, gla_norm, rel_bias, w_out, ln_moe,
           w_router, b_router, w1, b1, w2, b2, ln_ple, w_ple_gate, w_ple_proj, ln_out):
    n, d = x2.shape
    qk, gv, aw = GLA_HEADS * GLA_DK, GLA_HEADS * GLA_DV, ATT_HEADS * ATT_DH
    names = ("q_g", "k_g", "v_g", "gk_low", "r_g", "q_a", "k_a", "v_a", "gt_a", "gt_b")
    widths = (qk, qk, gv, GLA_RANK, gv, aw, aw, aw, d, d)
    src = dict(zip(names, np.cumsum((0,) + widths[:-1]).tolist()))
    wid = dict(zip(names, widths))
    order = [nm for nm in names if nm != "gk_low"]
    col, off = {}, 0
    for nm in order:
        col[nm] = off
        off += wid[nm]
    w_main = jnp.concatenate([w_in[:, src[nm]:src[nm] + wid[nm]] for nm in order], axis=1).astype(BF16)
    w_low = jnp.pad(w_in[:, src["gk_low"]:src["gk_low"] + GLA_RANK],
                    ((0, 0), (0, LANES - GLA_RANK))).astype(BF16)

    z, gk = _in_proj(x2, ln_mix.reshape(1, d), w_main, w_low)
    z3 = z.reshape(bsz, seq, -1)
    gk3 = gk.reshape(bsz, seq, LANES)

    wgk_pad = jnp.pad(w_gk, ((0, LANES - GLA_RANK), (0, 0)))
    ya = _gla(z3, gk3, wgk_pad, b_gk.reshape(1, qk), gla_norm.reshape(1, GLA_DV), col)
    yb = _attn(z3, _attn_bias(rel_bias), col)

    wr_pad = jnp.pad(w_router, ((0, 0), (0, LANES - N_EXPERTS)))
    br_pad = jnp.pad(b_router, (0, LANES - N_EXPERTS)).reshape(1, LANES)
    x1, xm, topi, topw = _outproj(x2, ya.reshape(n, d), yb.reshape(n, d), w_out.astype(BF16),
                                  ln_moe.reshape(1, d), wr_pad, br_pad)

    rank, cnt = _ranks(topi)
    counts = cnt[0, :N_EXPERTS]
    ntile = (counts + EXP_TM - 1) // EXP_TM
    tile_end = jnp.cumsum(ntile)
    tile_start = tile_end - ntile
    idx = topi[:, :TOP_K]
    pos = tile_start[idx] * EXP_TM + rank[:, :TOP_K]
    rows_total = n * TOP_K + N_EXPERTS * EXP_TM
    tiles = jnp.arange(rows_total // EXP_TM, dtype=I32)
    te = jnp.minimum(jnp.searchsorted(tile_end, tiles, side="right"), N_EXPERTS - 1).astype(I32)
    tv = jnp.clip(counts[te] - (tiles - tile_start[te]) * EXP_TM, 0, EXP_TM)
    tv = jnp.where(tiles < tile_end[-1], tv, 0).astype(I32)

    tok = jnp.zeros((rows_total,), I32).at[pos.reshape(-1)].set(
        jnp.repeat(jnp.arange(n, dtype=I32), TOP_K))
    xs = xm[tok]

    f = w1.shape[2] // 2
    ys = _experts(te, tv, xs,
                  w1[:, :, 0::2].astype(BF16), w1[:, :, 1::2].astype(BF16),
                  b1[:, 0::2].reshape(N_EXPERTS, 1, f), b1[:, 1::2].reshape(N_EXPERTS, 1, f),
                  w2.astype(BF16), b2.reshape(N_EXPERTS, 1, d))

    moe = jnp.einsum("nk,nkd->nd", topw[:, :TOP_K], ys[pos])

    return _ple(x1, moe, p2, ln_ple.reshape(1, d), w_ple_gate.astype(BF16),
                w_ple_proj.astype(BF16), ln_out.reshape(1, d))


def kernel(x, p, ln_mix, w_in, w_gk, b_gk, gla_norm, rel_bias, w_out, ln_moe, w_router, b_router,
           w1, b1, w2, b2, ln_ple, w_ple_gate, w_ple_proj, ln_final):
    bsz, seq, d = x.shape
    depth = p.shape[0]
    assert depth == 1, "the final RMSNorm is fused into the last layer's kernel"
    x2 = x.reshape(bsz * seq, d)
    out = _layer(x2, p[0].reshape(bsz * seq, -1), bsz, seq, ln_mix[0], w_in[0], w_gk[0], b_gk[0],
                 gla_norm[0], rel_bias[0], w_out[0], ln_moe[0], w_router[0], b_router[0],
                 w1[0], b1[0], w2[0], b2[0], ln_ple[0], w_ple_gate[0], w_ple_proj[0], ln_final)
    return out.reshape(bsz, seq, d)
```

```python
import functools

import numpy as np
import jax
import jax.numpy as jnp
from jax import lax
from jax.experimental import pallas as pl
from jax.experimental.pallas import tpu as pltpu

F32 = jnp.float32
BF16 = jnp.bfloat16
I32 = jnp.int32

LANES = 128
CHUNK = 64
GLA_HEADS = 4
GLA_DK = 128
GLA_DV = 256
GLA_RANK = 16
GLA_TAU = 16.0
ATT_HEADS = 16
ATT_DH = 64
ATT_PAST = 8
REL_CLIP = 256
N_EXPERTS = 32
TOP_K = 4
SWIGLU_ALPHA = 1.702
SWIGLU_LIMIT = 7.0
EPS = 1e-6

VMEM_LIMIT = 48 * 1024 * 1024

IN_TM, IN_TN = 1024, 1024
GLA_TT = 128
ATT_TQ = 256
OUT_TM = 512
RANK_TB = 512
EXP_TM = 256
DISPATCH_TOKENS = 1024
PLE_TB = 256
DMA_UNROLL = 8


def _cparams(sem):
    return pltpu.CompilerParams(dimension_semantics=sem, vmem_limit_bytes=VMEM_LIMIT)


def _split_bf16(a):
    hi = a.astype(BF16)
    lo = (a - hi.astype(F32)).astype(BF16)
    return hi, lo


def _dot_split(a, b):
    a_hi, a_lo = _split_bf16(a)
    b_hi, b_lo = _split_bf16(b)
    d = functools.partial(jnp.dot, preferred_element_type=F32)
    return d(a_hi, b_hi) + (d(a_hi, b_lo) + d(a_lo, b_hi))


def _rms(x, gain):
    ms = jnp.mean(x * x, axis=-1, keepdims=True)
    return x * lax.rsqrt(ms + EPS) * gain


_NT = (((1,), (1,)), ((), ()))
_TN = (((0,), (0,)), ((), ()))


def _in_proj_kernel(x_ref, g_ref, w_ref, wlow_ref, z_ref, gk_ref, xn_ref):
    @pl.when(pl.program_id(1) == 0)
    def _():
        xn = _rms(x_ref[...], g_ref[...]).astype(BF16)
        xn_ref[...] = xn
        gk_ref[...] = jnp.dot(xn, wlow_ref[...], preferred_element_type=F32)

    z_ref[...] = jnp.dot(xn_ref[...], w_ref[...], preferred_element_type=F32).astype(BF16)


def _in_proj(x2, ln, w_main, w_low):
    n, d = x2.shape
    ncol = w_main.shape[1]
    return pl.pallas_call(
        _in_proj_kernel,
        grid=(n // IN_TM, ncol // IN_TN),
        in_specs=[
            pl.BlockSpec((IN_TM, d), lambda i, j: (i, 0)),
            pl.BlockSpec((1, d), lambda i, j: (0, 0)),
            pl.BlockSpec((d, IN_TN), lambda i, j: (0, j)),
            pl.BlockSpec((d, LANES), lambda i, j: (0, 0)),
        ],
        out_specs=[
            pl.BlockSpec((IN_TM, IN_TN), lambda i, j: (i, j)),
            pl.BlockSpec((IN_TM, LANES), lambda i, j: (i, 0)),
        ],
        out_shape=[
            jax.ShapeDtypeStruct((n, ncol), BF16),
            jax.ShapeDtypeStruct((n, LANES), F32),
        ],
        scratch_shapes=[pltpu.VMEM((IN_TM, d), BF16)],
        compiler_params=_cparams(("parallel", "arbitrary")),
        name="in_proj",
    )(x2, ln, w_main, w_low)


_GLA_LEVELS = (8, 16, 32)
_GLA_BOT = 8


def _gla_consts(tt):
    t = np.arange(tt)
    same_chunk = (t[:, None] // CHUNK) == (t[None, :] // CHUNK)
    tri = (same_chunk & (t[None, :] <= t[:, None])).astype(np.float32)
    mlev = []
    for h in _GLA_LEVELS:
        blk = (t[:, None] // (2 * h)) == (t[None, :] // (2 * h))
        m = blk & ((t[:, None] % (2 * h)) >= h) & ((t[None, :] % (2 * h)) < h)
        mlev.append(m.astype(np.float32))
    mbot = []
    for s in range(_GLA_BOT):
        m = (t[None, :] == (t[:, None] // _GLA_BOT) * _GLA_BOT + s) & ((t[:, None] % _GLA_BOT) >= s)
        mbot.append(m.astype(np.float32))
    return (jnp.asarray(tri, BF16), jnp.asarray(np.stack(mlev), F32),
            jnp.asarray(np.stack(mbot), F32), jnp.ones((GLA_DK, tt), BF16))


def _gla_kernel(q_ref, k_ref, v_ref, r_ref, ga_ref, gk_ref, wgk_ref, bgk_ref, gn_ref,
                tri_ref, mlev_ref, mbot_ref, ones_ref, o_ref, st_ref):
    tt = q_ref.shape[0]

    @pl.when(pl.program_id(2) == 0)
    def _():
        st_ref[...] = jnp.zeros_like(st_ref)

    q = q_ref[...].astype(F32) * (GLA_DK ** -0.5)
    k = k_ref[...].astype(F32)
    v = v_ref[...]

    xg = _dot_split(gk_ref[...], wgk_ref[...]) + bgk_ref[...]
    g = -(jnp.maximum(-xg, 0.0) + jnp.log(1.0 + jnp.exp(-jnp.abs(xg)))) * (1.0 / GLA_TAU)

    g_hi, g_lo = _split_bf16(g)
    tri = tri_ref[...]
    b = (jnp.dot(tri, g_hi, preferred_element_type=F32)
         + jnp.dot(tri, g_lo, preferred_element_type=F32))

    s_intra = jnp.zeros((tt, tt), F32)
    for li, h in enumerate(_GLA_LEVELS):
        b3 = b.reshape(tt // (2 * h), 2 * h, GLA_DK)
        bm = b3[:, h - 1:h, :]
        eq = jnp.exp(jnp.minimum(b3 - bm, 0.0)).reshape(tt, GLA_DK)
        ek = jnp.exp(jnp.minimum(bm - b3, 0.0)).reshape(tt, GLA_DK)
        sc = lax.dot_general((q * eq).astype(BF16), (k * ek).astype(BF16), _NT,
                             preferred_element_type=F32)
        s_intra = s_intra + sc * mlev_ref[li]
    nb = tt // _GLA_BOT
    b3 = b.reshape(nb, _GLA_BOT, GLA_DK)
    q3 = q.reshape(nb, _GLA_BOT, GLA_DK)
    k3 = k.reshape(nb, _GLA_BOT, GLA_DK)
    ones = ones_ref[...]
    for s in range(_GLA_BOT):
        e = jnp.exp(jnp.minimum(b3 - b3[:, s:s + 1, :], 0.0))
        a = (q3 * e * k3[:, s:s + 1, :]).reshape(tt, GLA_DK).astype(BF16)
        s_intra = s_intra + jnp.dot(a, ones, preferred_element_type=F32) * mbot_ref[s]
    o_intra = jnp.dot(s_intra.astype(BF16), v, preferred_element_type=F32)

    st = st_ref[...]
    outs = []
    for c in range(tt // CHUNK):
        lo = c * CHUNK
        bc = b[lo:lo + CHUNK]
        bl = b[lo + CHUNK - 1:lo + CHUNK]
        qe = (q[lo:lo + CHUNK] * jnp.exp(bc)).astype(BF16)
        outs.append(lax.dot_general(qe, st.astype(BF16), _NT, preferred_element_type=F32))
        kd = (k[lo:lo + CHUNK] * jnp.exp(bl - bc)).astype(BF16)
        upd = lax.dot_general(v[lo:lo + CHUNK], kd, _TN, preferred_element_type=F32)
        st = st * jnp.exp(bl) + upd
    st_ref[...] = st
    o = o_intra + jnp.concatenate(outs, axis=0)

    r = r_ref[...].astype(F32)
    ya = _rms(o, gn_ref[...]) * (r * jax.nn.sigmoid(r))
    o_ref[...] = (jax.nn.sigmoid(ga_ref[...].astype(F32)) * ya).astype(BF16)


def _gla(z3, gk3, wgk_pad, bgk, gnorm, col):
    b, t, _ = z3.shape
    tt = GLA_TT
    tri, mlev, mbot, ones = _gla_consts(tt)

    def zspec(width, off):
        return pl.BlockSpec((None, tt, width), lambda bi, h, ti, o=off // width: (bi, ti, o + h))

    const2 = lambda bi, h, ti: (0, 0)
    const3 = lambda bi, h, ti: (0, 0, 0)
    return pl.pallas_call(
        _gla_kernel,
        grid=(b, GLA_HEADS, t // tt),
        in_specs=[
            zspec(GLA_DK, col["q_g"]), zspec(GLA_DK, col["k_g"]), zspec(GLA_DV, col["v_g"]),
            zspec(GLA_DV, col["r_g"]), zspec(GLA_DV, col["gt_a"]),
            pl.BlockSpec((None, tt, LANES), lambda bi, h, ti: (bi, ti, 0)),
            pl.BlockSpec((LANES, GLA_DK), lambda bi, h, ti: (0, h)),
            pl.BlockSpec((1, GLA_DK), lambda bi, h, ti: (0, h)),
            pl.BlockSpec((1, GLA_DV), const2),
            pl.BlockSpec((tt, tt), const2),
            pl.BlockSpec((len(_GLA_LEVELS), tt, tt), const3),
            pl.BlockSpec((_GLA_BOT, tt, tt), const3),
            pl.BlockSpec((GLA_DK, tt), const2),
        ],
        out_specs=pl.BlockSpec((None, tt, GLA_DV), lambda bi, h, ti: (bi, ti, h)),
        out_shape=jax.ShapeDtypeStruct((b, t, GLA_HEADS * GLA_DV), BF16),
        scratch_shapes=[pltpu.VMEM((GLA_DV, GLA_DK), F32)],
        compiler_params=_cparams(("parallel", "parallel", "arbitrary")),
        name="gla",
    )(z3, z3, z3, z3, z3, gk3, wgk_pad, bgk, gnorm, tri, mlev, mbot, ones)


_ATT_NKB = 3


def _attn_bias(rel_bias):
    tq = ATT_TQ
    t = np.arange(tq)[:, None]
    w = np.arange(_ATT_NKB * tq)[None, :]
    s = w - (_ATT_NKB - 1) * tq
    rel = np.clip(t - s, -REL_CLIP, REL_CLIP) + REL_CLIP
    dc = t // CHUNK - np.floor_divide(s, CHUNK)
    valid = (dc >= 0) & (dc <= ATT_PAST)
    tab = rel_bias.astype(F32)[:, rel]
    return jnp.where(jnp.asarray(valid)[None], tab, -jnp.inf)


def _attn_kernel(q_ref, k0_ref, k1_ref, k2_ref, v0_ref, v1_ref, v2_ref, gb_ref, bias_ref, o_ref):
    j = pl.program_id(2)
    tq = q_ref.shape[0]
    q = q_ref[...]
    ks = (k0_ref[...], k1_ref[...], k2_ref[...])
    vs = (v0_ref[...], v1_ref[...], v2_ref[...])
    lane = lax.broadcasted_iota(I32, (1, LANES), 1)
    scale = ATT_DH ** -0.5
    heads = []
    for hh in range(LANES // ATT_DH):
        in_head = (lane // ATT_DH) == hh
        qh = jnp.where(in_head, q, jnp.zeros_like(q))
        parts = []
        for i in range(_ATT_NKB):
            s = lax.dot_general(qh, ks[i], _NT, preferred_element_type=F32)
            s = s * scale + bias_ref[hh, :, i * tq:(i + 1) * tq]
            if i < _ATT_NKB - 1:
                s = s + jnp.where(j - (_ATT_NKB - 1) + i >= 0, 0.0, -jnp.inf)
            parts.append(s)
        m = functools.reduce(jnp.maximum, [jnp.max(p, axis=-1, keepdims=True) for p in parts])
        ps = [jnp.exp(p - m) for p in parts]
        l = functools.reduce(jnp.add, [jnp.sum(p, axis=-1, keepdims=True) for p in ps])
        pv = functools.reduce(
            jnp.add, [jnp.dot(p.astype(BF16), vv, preferred_element_type=F32) for p, vv in zip(ps, vs)])
        heads.append(pv / l)
    o = jnp.where((lane // ATT_DH) == 0, heads[0], heads[1])
    o_ref[...] = (jax.nn.sigmoid(gb_ref[...].astype(F32)) * o).astype(BF16)


def _attn(z3, bias, col):
    b, t, _ = z3.shape
    tq = ATT_TQ
    hp = ATT_HEADS * ATT_DH // LANES

    def cur(off):
        return pl.BlockSpec((None, tq, LANES), lambda h, bi, j, o=off // LANES: (bi, j, o + h))

    def past(off, back):
        return pl.BlockSpec((None, tq, LANES),
                            lambda h, bi, j, o=off // LANES: (bi, jnp.maximum(j - back, 0), o + h))

    return pl.pallas_call(
        _attn_kernel,
        grid=(hp, b, t // tq),
        in_specs=[
            cur(col["q_a"]),
            past(col["k_a"], 2), past(col["k_a"], 1), cur(col["k_a"]),
            past(col["v_a"], 2), past(col["v_a"], 1), cur(col["v_a"]),
            cur(col["gt_b"]),
            pl.BlockSpec((LANES // ATT_DH, tq, _ATT_NKB * tq), lambda h, bi, j: (h, 0, 0)),
        ],
        out_specs=pl.BlockSpec((None, tq, LANES), lambda h, bi, j: (bi, j, h)),
        out_shape=jax.ShapeDtypeStruct((b, t, ATT_HEADS * ATT_DH), BF16),
        compiler_params=_cparams(("parallel", "parallel", "parallel")),
        name="band_attn",
    )(z3, z3, z3, z3, z3, z3, z3, z3, bias)


def _outproj_kernel(x_ref, ya_ref, yb_ref, wo_ref, lnm_ref, wr_ref, br_ref,
                    x1_ref, xm_ref, topi_ref, topw_ref):
    h = (ya_ref[...].astype(F32) + yb_ref[...].astype(F32)).astype(BF16)
    x1 = x_ref[...] + jnp.dot(h, wo_ref[...], preferred_element_type=F32)
    x1_ref[...] = x1
    xm = _rms(x1, lnm_ref[...])
    xm_ref[...] = xm

    tm = xm.shape[0]
    lane = lax.broadcasted_iota(I32, (tm, LANES), 1)
    lanef = lane.astype(F32)
    logits = _dot_split(xm, wr_ref[...]) + br_ref[...]
    l = jnp.where(lane < N_EXPERTS, logits, -jnp.inf)
    vals, idxs = [], []
    for _ in range(TOP_K):
        m = jnp.max(l, axis=-1, keepdims=True)
        idx = jnp.min(jnp.where(l == m, lanef, float(LANES)), axis=-1, keepdims=True)
        vals.append(m)
        idxs.append(idx)
        l = jnp.where(lanef == idx, -jnp.inf, l)
    es = [jnp.exp(vv - vals[0]) for vv in vals]
    tot = functools.reduce(jnp.add, es)
    topw = jnp.zeros((tm, LANES), F32)
    topi = jnp.zeros((tm, LANES), F32)
    for kk in range(TOP_K):
        topw = jnp.where(lane == kk, es[kk] / tot, topw)
        topi = jnp.where(lane == kk, idxs[kk], topi)
    topw_ref[...] = topw
    topi_ref[...] = topi.astype(I32)


def _outproj(x2, ya, yb, w_out, ln_moe, wr_pad, br_pad):
    n, d = x2.shape
    tm = OUT_TM
    row = lambda i: (i, 0)
    const = lambda i: (0, 0)
    return pl.pallas_call(
        _outproj_kernel,
        grid=(n // tm,),
        in_specs=[
            pl.BlockSpec((tm, d), row), pl.BlockSpec((tm, d), row), pl.BlockSpec((tm, d), row),
            pl.BlockSpec((d, d), const), pl.BlockSpec((1, d), const),
            pl.BlockSpec((d, LANES), const), pl.BlockSpec((1, LANES), const),
        ],
        out_specs=[
            pl.BlockSpec((tm, d), row), pl.BlockSpec((tm, d), row),
            pl.BlockSpec((tm, LANES), row), pl.BlockSpec((tm, LANES), row),
        ],
        out_shape=[
            jax.ShapeDtypeStruct((n, d), F32), jax.ShapeDtypeStruct((n, d), F32),
            jax.ShapeDtypeStruct((n, LANES), I32), jax.ShapeDtypeStruct((n, LANES), F32),
        ],
        compiler_params=_cparams(("parallel",)),
        name="out_proj_router",
    )(x2, ya, yb, w_out, ln_moe, wr_pad, br_pad)


def _rank_kernel(topi_ref, tri_ref, rank_ref, cnt_ref, carry_ref):
    @pl.when(pl.program_id(0) == 0)
    def _():
        carry_ref[...] = jnp.zeros_like(carry_ref)

    ti = topi_ref[...]
    tb = ti.shape[0]
    lane = lax.broadcasted_iota(I32, (tb, LANES), 1)
    sel = [lane == ti[:, kk:kk + 1] for kk in range(TOP_K)]
    oh = functools.reduce(jnp.add, [jnp.where(s, 1.0, 0.0) for s in sel])
    rank = carry_ref[...] + jnp.dot(tri_ref[...], oh.astype(BF16), preferred_element_type=F32)
    out = jnp.zeros((tb, LANES), F32)
    for kk in range(TOP_K):
        rk = jnp.sum(jnp.where(sel[kk], rank, 0.0), axis=-1, keepdims=True)
        out = jnp.where(lane == kk, rk, out)
    rank_ref[...] = out.astype(I32)
    carry_ref[...] = carry_ref[...] + jnp.sum(oh, axis=0, keepdims=True)
    cnt_ref[...] = carry_ref[...].astype(I32)


def _ranks(topi):
    n = topi.shape[0]
    tb = RANK_TB
    t = np.arange(tb)
    tri = jnp.asarray((t[None, :] < t[:, None]).astype(np.float32), BF16)
    return pl.pallas_call(
        _rank_kernel,
        grid=(n // tb,),
        in_specs=[pl.BlockSpec((tb, LANES), lambda i: (i, 0)),
                  pl.BlockSpec((tb, tb), lambda i: (0, 0))],
        out_specs=[pl.BlockSpec((tb, LANES), lambda i: (i, 0)),
                   pl.BlockSpec((1, LANES), lambda i: (0, 0))],
        out_shape=[jax.ShapeDtypeStruct((n, LANES), I32), jax.ShapeDtypeStruct((1, LANES), I32)],
        scratch_shapes=[pltpu.VMEM((1, LANES), F32)],
        compiler_params=_cparams(("arbitrary",)),
        name="route_rank",
    )(topi, tri)


def _row_copy(src_hbm, src_row, dst, dst_row, sem):
    return pltpu.make_async_copy(src_hbm.at[pl.ds(src_row, 1)], dst.at[pl.ds(dst_row, 1)], sem)


def _rows_wait(ref, nrows, sem):
    pltpu.make_async_copy(ref.at[pl.ds(0, nrows)], ref.at[pl.ds(0, nrows)], sem).wait()


def _dispatch_kernel(tv_ref, pos_hbm, xm_hbm, xs_hbm, pos_smem, zeros_ref, sem_pos, sem_rows, sem_fill):
    i = pl.program_id(0)
    ch = pos_smem.shape[0]
    tokens = ch // TOP_K
    cp = pltpu.make_async_copy(pos_hbm.at[pl.ds(i * ch, ch)], pos_smem, sem_pos)
    cp.start()

    @pl.when(i == 0)
    def _():
        tm = zeros_ref.shape[0]
        zeros_ref[...] = jnp.zeros_like(zeros_ref)

        def fill(t):
            return pltpu.make_async_copy(zeros_ref, xs_hbm.at[pl.ds(t * tm, tm)], sem_fill)

        def start(t, carry):
            @pl.when(tv_ref[t] < tm)
            def _():
                fill(t).start()
            return carry

        def wait(t, carry):
            @pl.when(tv_ref[t] < tm)
            def _():
                fill(t).wait()
            return carry

        ntiles = xs_hbm.shape[0] // tm
        lax.fori_loop(0, ntiles, start, 0)
        lax.fori_loop(0, ntiles, wait, 0)

    cp.wait()

    def body(t, carry):
        for kk in range(TOP_K):
            _row_copy(xm_hbm, i * tokens + t, xs_hbm, pos_smem[t * TOP_K + kk], sem_rows).start()
        return carry

    lax.fori_loop(0, tokens, body, 0, unroll=DMA_UNROLL)
    _rows_wait(xs_hbm, ch, sem_rows)


def _dispatch(tile_valid, pos_flat, xm, rows_total):
    n, d = xm.shape
    ch = DISPATCH_TOKENS * TOP_K
    return pl.pallas_call(
        _dispatch_kernel,
        grid_spec=pltpu.PrefetchScalarGridSpec(
            num_scalar_prefetch=1,
            grid=(n // DISPATCH_TOKENS,),
            in_specs=[pl.BlockSpec(memory_space=pl.ANY), pl.BlockSpec(memory_space=pl.ANY)],
            out_specs=pl.BlockSpec(memory_space=pl.ANY),
            scratch_shapes=[pltpu.SMEM((ch,), I32), pltpu.VMEM((EXP_TM, d), F32),
                            pltpu.SemaphoreType.DMA, pltpu.SemaphoreType.DMA,
                            pltpu.SemaphoreType.DMA],
        ),
        out_shape=jax.ShapeDtypeStruct((rows_total, d), F32),
        compiler_params=_cparams(("arbitrary",)),
        name="dispatch",
    )(tile_valid, pos_flat, xm)


def _expert_kernel(te_ref, tv_ref, xs_ref, w1g_ref, w1l_ref, b1g_ref, b1l_ref, w2_ref, b2_ref, ys_ref):
    t = pl.program_id(0)
    nvalid = tv_ref[t]

    @pl.when(nvalid > 0)
    def _():
        x = xs_ref[...].astype(BF16)
        hg = lax.dot_general(x, w1g_ref[...], _NT, preferred_element_type=F32) + b1g_ref[...]
        hl = lax.dot_general(x, w1l_ref[...], _NT, preferred_element_type=F32) + b1l_ref[...]
        glu = jnp.minimum(hg, SWIGLU_LIMIT)
        lin = jnp.clip(hl, -SWIGLU_LIMIT, SWIGLU_LIMIT)
        act = glu * jax.nn.sigmoid(SWIGLU_ALPHA * glu) * (lin + 1.0)
        ys_ref[...] = jnp.dot(act.astype(BF16), w2_ref[...], preferred_element_type=F32) + b2_ref[...]

    @pl.when(nvalid <= 0)
    def _():
        ys_ref[...] = jnp.zeros_like(ys_ref)


def _experts(tile_expert, tile_valid, xs, w1g, w1l, b1g, b1l, w2, b2):
    p, d = xs.shape
    f = w1g.shape[1]
    tm = EXP_TM
    wmap = lambda t, te, tv: (te[t], 0, 0)
    return pl.pallas_call(
        _expert_kernel,
        grid_spec=pltpu.PrefetchScalarGridSpec(
            num_scalar_prefetch=2,
            grid=(p // tm,),
            in_specs=[
                pl.BlockSpec((tm, d), lambda t, te, tv: (t, 0)),
                pl.BlockSpec((None, f, d), wmap), pl.BlockSpec((None, f, d), wmap),
                pl.BlockSpec((None, 1, f), wmap), pl.BlockSpec((None, 1, f), wmap),
                pl.BlockSpec((None, f, d), wmap), pl.BlockSpec((None, 1, d), wmap),
            ],
            out_specs=pl.BlockSpec((tm, d), lambda t, te, tv: (t, 0)),
        ),
        out_shape=jax.ShapeDtypeStruct((p, d), F32),
        compiler_params=_cparams(("arbitrary",)),
        name="experts",
    )(tile_expert, tile_valid, xs, w1g, w1l, b1g, b1l, w2, b2)


def _ple_kernel(pos_hbm, ys_hbm, topw_ref, x1_ref, p_ref, lnp_ref, wpg_ref, wpp_ref, lnf_ref, o_ref,
                pos_smem, ybuf, sem_pos, sem_rows):
    i = pl.program_id(0)
    nsteps = pl.num_programs(0)
    tb = x1_ref.shape[0]
    ch = tb * TOP_K
    slot = i % 2

    def pos_copy(step, sl):
        return pltpu.make_async_copy(pos_hbm.at[pl.ds(step * ch, ch)], pos_smem.at[sl], sem_pos.at[sl])

    def issue_gathers(sl):
        def body(t, carry):
            for kk in range(TOP_K):
                _row_copy(ys_hbm, pos_smem[sl, t * TOP_K + kk], ybuf.at[sl, kk], t,
                          sem_rows.at[sl]).start()
            return carry
        lax.fori_loop(0, tb, body, 0, unroll=DMA_UNROLL)

    @pl.when(i == 0)
    def _():
        pos_copy(0, 0).start()
        pos_copy(0, 0).wait()
        issue_gathers(0)

        @pl.when(nsteps > 1)
        def _():
            pos_copy(1, 1).start()

    @pl.when(i + 1 < nsteps)
    def _():
        pos_copy(i + 1, 1 - slot).wait()
        issue_gathers(1 - slot)

    @pl.when(i + 2 < nsteps)
    def _():
        pos_copy(i + 2, slot).start()

    for kk in range(TOP_K):
        _rows_wait(ybuf.at[slot, kk], tb, sem_rows.at[slot])

    topw = topw_ref[...]
    moe = functools.reduce(jnp.add, [topw[:, kk:kk + 1] * ybuf[slot, kk] for kk in range(TOP_K)])
    x2 = x1_ref[...] + moe
    gate = jax.nn.sigmoid(jnp.dot(_rms(x2, lnp_ref[...]).astype(BF16), wpg_ref[...],
                                  preferred_element_type=F32))
    proj = jnp.dot(p_ref[...].astype(BF16), wpp_ref[...], preferred_element_type=F32)
    o_ref[...] = _rms(x2 + gate * proj, lnf_ref[...])


def _ple(pos_flat, ys, topw, x1, p2, ln_ple, wpg, wpp, ln_final):
    n, d = x1.shape
    pd = p2.shape[1]
    tb = PLE_TB
    row = lambda i: (i, 0)
    const = lambda i: (0, 0)
    return pl.pallas_call(
        _ple_kernel,
        grid=(n // tb,),
        in_specs=[
            pl.BlockSpec(memory_space=pl.ANY), pl.BlockSpec(memory_space=pl.ANY),
            pl.BlockSpec((tb, LANES), row), pl.BlockSpec((tb, d), row), pl.BlockSpec((tb, pd), row),
            pl.BlockSpec((1, d), const), pl.BlockSpec((d, d), const),
            pl.BlockSpec((pd, d), const), pl.BlockSpec((1, d), const),
        ],
        out_specs=pl.BlockSpec((tb, d), row),
        out_shape=jax.ShapeDtypeStruct((n, d), F32),
        scratch_shapes=[
            pltpu.SMEM((2, tb * TOP_K), I32),
            pltpu.VMEM((2, TOP_K, tb, d), F32),
            pltpu.SemaphoreType.DMA((2,)),
            pltpu.SemaphoreType.DMA((2,)),
        ],
        compiler_params=_cparams(("arbitrary",)),
        name="combine_ple_final",
    )(pos_flat, ys, topw, x1, p2, ln_ple, wpg, wpp, ln_final)


def _layer(x2, p2, bsz, seq, ln_mix, w_in, w_gk, b_gk, gla_norm, rel_bias, w_out, ln_moe,
           w_router, b_router, w1, b1, w2, b2, ln_ple, w_ple_gate, w_ple_proj, ln_out):
    n, d = x2.shape
    qk, gv, aw = GLA_HEADS * GLA_DK, GLA_HEADS * GLA_DV, ATT_HEADS * ATT_DH
    names = ("q_g", "k_g", "v_g", "gk_low", "r_g", "q_a", "k_a", "v_a", "gt_a", "gt_b")
    widths = (qk, qk, gv, GLA_RANK, gv, aw, aw, aw, d, d)
    src = dict(zip(names, np.cumsum((0,) + widths[:-1]).tolist()))
    wid = dict(zip(names, widths))
    order = [nm for nm in names if nm != "gk_low"]
    col, off = {}, 0
    for nm in order:
        col[nm] = off
        off += wid[nm]
    w_main = jnp.concatenate([w_in[:, src[nm]:src[nm] + wid[nm]] for nm in order], axis=1).astype(BF16)
    w_low = jnp.pad(w_in[:, src["gk_low"]:src["gk_low"] + GLA_RANK],
                    ((0, 0), (0, LANES - GLA_RANK))).astype(BF16)

    z, gk = _in_proj(x2, ln_mix.reshape(1, d), w_main, w_low)
    z3 = z.reshape(bsz, seq, -1)
    gk3 = gk.reshape(bsz, seq, LANES)

    wgk_pad = jnp.pad(w_gk, ((0, LANES - GLA_RANK), (0, 0)))
    ya = _gla(z3, gk3, wgk_pad, b_gk.reshape(1, qk), gla_norm.reshape(1, GLA_DV), col)
    yb = _attn(z3, _attn_bias(rel_bias), col)

    wr_pad = jnp.pad(w_router, ((0, 0), (0, LANES - N_EXPERTS)))
    br_pad = jnp.pad(b_router, (0, LANES - N_EXPERTS)).reshape(1, LANES)
    x1, xm, topi, topw = _outproj(x2, ya.reshape(n, d), yb.reshape(n, d), w_out.astype(BF16),
                                  ln_moe.reshape(1, d), wr_pad, br_pad)

    rank, cnt = _ranks(topi)
    counts = cnt[0, :N_EXPERTS]
    ntile = (counts + EXP_TM - 1) // EXP_TM
    tile_end = jnp.cumsum(ntile)
    tile_start = tile_end - ntile
    idx = topi[:, :TOP_K]
    pos = tile_start[idx] * EXP_TM + rank[:, :TOP_K]
    rows_total = n * TOP_K + N_EXPERTS * EXP_TM
    tiles = jnp.arange(rows_total // EXP_TM, dtype=I32)
    te = jnp.sum((tiles[:, None] >= tile_end[None, :]).astype(I32), axis=1)
    te = jnp.minimum(te, N_EXPERTS - 1)
    tv = jnp.clip(counts[te] - (tiles - tile_start[te]) * EXP_TM, 0, EXP_TM)
    tv = jnp.where(tiles < tile_end[-1], tv, 0).astype(I32)

    pos_flat = pos.reshape(-1).astype(I32)
    xs = _dispatch(tv, pos_flat, xm, rows_total)

    f = w1.shape[2] // 2
    w1t = jnp.swapaxes(w1, 1, 2)
    ys = _experts(te, tv, xs,
                  w1t[:, 0::2, :].astype(BF16), w1t[:, 1::2, :].astype(BF16),
                  b1[:, 0::2].reshape(N_EXPERTS, 1, f), b1[:, 1::2].reshape(N_EXPERTS, 1, f),
                  w2.astype(BF16), b2.reshape(N_EXPERTS, 1, d))

    return _ple(pos_flat, ys, topw, x1, p2, ln_ple.reshape(1, d), w_ple_gate.astype(BF16),
                w_ple_proj.astype(BF16), ln_out.reshape(1, d))


def kernel(x, p, ln_mix, w_in, w_gk, b_gk, gla_norm, rel_bias, w_out, ln_moe, w_router, b_router,
           w1, b1, w2, b2, ln_ple, w_ple_gate, w_ple_proj, ln_final):
    bsz, seq, d = x.shape
    depth = p.shape[0]
    assert depth == 1, "the final RMSNorm is fused into the last layer's kernel"
    x2 = x.reshape(bsz * seq, d)
    out = _layer(x2, p[0].reshape(bsz * seq, -1), bsz, seq, ln_mix[0], w_in[0], w_gk[0], b_gk[0],
                 gla_norm[0], rel_bias[0], w_out[0], ln_moe[0], w_router[0], b_router[0],
                 w1[0], b1[0], w2[0], b2[0], ln_ple[0], w_ple_gate[0], w_ple_proj[0], ln_final)
    return out.reshape(bsz, seq, d)
```

```python
import functools

import numpy as np
import jax
import jax.numpy as jnp
from jax import lax
from jax.experimental import pallas as pl
from jax.experimental.pallas import tpu as pltpu

F32 = jnp.float32
BF16 = jnp.bfloat16
I32 = jnp.int32

LANES = 128
CHUNK = 64
GLA_HEADS = 4
GLA_DK = 128
GLA_DV = 256
GLA_RANK = 16
GLA_TAU = 16.0
ATT_HEADS = 16
ATT_DH = 64
ATT_PAST = 8
REL_CLIP = 256
N_EXPERTS = 32
TOP_K = 4
SWIGLU_ALPHA = 1.702
SWIGLU_LIMIT = 7.0
EPS = 1e-6

VMEM_LIMIT = 48 * 1024 * 1024

IN_TM, IN_TN = 1024, 1024
GLA_TT = 128
ATT_TQ = 256
OUT_TM = 512
RANK_TB = 512
EXP_TM = 256
DISPATCH_TOKENS = 512
PLE_TB = 256
DMA_UNROLL = 8


def _cparams(sem):
    return pltpu.CompilerParams(dimension_semantics=sem, vmem_limit_bytes=VMEM_LIMIT)


def _split_bf16(a):
    hi = a.astype(BF16)
    lo = (a - hi.astype(F32)).astype(BF16)
    return hi, lo


def _dot_split(a, b):
    a_hi, a_lo = _split_bf16(a)
    b_hi, b_lo = _split_bf16(b)
    d = functools.partial(jnp.dot, preferred_element_type=F32)
    return d(a_hi, b_hi) + (d(a_hi, b_lo) + d(a_lo, b_hi))


def _rms(x, gain):
    ms = jnp.mean(x * x, axis=-1, keepdims=True)
    return x * lax.rsqrt(ms + EPS) * gain


_NT = (((1,), (1,)), ((), ()))
_TN = (((0,), (0,)), ((), ()))


def _in_proj_kernel(x_ref, g_ref, w_ref, wlow_ref, z_ref, gk_ref, xn_ref):
    @pl.when(pl.program_id(1) == 0)
    def _():
        xn = _rms(x_ref[...], g_ref[...]).astype(BF16)
        xn_ref[...] = xn
        gk_ref[...] = jnp.dot(xn, wlow_ref[...], preferred_element_type=F32)

    z_ref[...] = jnp.dot(xn_ref[...], w_ref[...], preferred_element_type=F32).astype(BF16)


def _in_proj(x2, ln, w_main, w_low):
    n, d = x2.shape
    ncol = w_main.shape[1]
    return pl.pallas_call(
        _in_proj_kernel,
        grid=(n // IN_TM, ncol // IN_TN),
        in_specs=[
            pl.BlockSpec((IN_TM, d), lambda i, j: (i, 0)),
            pl.BlockSpec((1, d), lambda i, j: (0, 0)),
            pl.BlockSpec((d, IN_TN), lambda i, j: (0, j)),
            pl.BlockSpec((d, LANES), lambda i, j: (0, 0)),
        ],
        out_specs=[
            pl.BlockSpec((IN_TM, IN_TN), lambda i, j: (i, j)),
            pl.BlockSpec((IN_TM, LANES), lambda i, j: (i, 0)),
        ],
        out_shape=[
            jax.ShapeDtypeStruct((n, ncol), BF16),
            jax.ShapeDtypeStruct((n, LANES), F32),
        ],
        scratch_shapes=[pltpu.VMEM((IN_TM, d), BF16)],
        compiler_params=_cparams(("parallel", "arbitrary")),
        name="in_proj",
    )(x2, ln, w_main, w_low)


_GLA_LEVELS = (8, 16, 32)
_GLA_BOT = 8


def _gla_consts(tt):
    t = np.arange(tt)
    same_chunk = (t[:, None] // CHUNK) == (t[None, :] // CHUNK)
    tri = (same_chunk & (t[None, :] <= t[:, None])).astype(np.float32)
    mlev = []
    for h in _GLA_LEVELS:
        blk = (t[:, None] // (2 * h)) == (t[None, :] // (2 * h))
        m = blk & ((t[:, None] % (2 * h)) >= h) & ((t[None, :] % (2 * h)) < h)
        mlev.append(m.astype(np.float32))
    mbot = []
    for s in range(_GLA_BOT):
        m = (t[None, :] == (t[:, None] // _GLA_BOT) * _GLA_BOT + s) & ((t[:, None] % _GLA_BOT) >= s)
        mbot.append(m.astype(np.float32))
    return (jnp.asarray(tri, BF16), jnp.asarray(np.stack(mlev), F32),
            jnp.asarray(np.stack(mbot), F32), jnp.ones((GLA_DK, tt), BF16))


def _gla_kernel(q_ref, k_ref, v_ref, r_ref, ga_ref, gk_ref, wgk_ref, bgk_ref, gn_ref,
                tri_ref, mlev_ref, mbot_ref, ones_ref, o_ref, st_ref):
    tt = q_ref.shape[0]

    @pl.when(pl.program_id(2) == 0)
    def _():
        st_ref[...] = jnp.zeros_like(st_ref)

    q = q_ref[...].astype(F32) * (GLA_DK ** -0.5)
    k = k_ref[...].astype(F32)
    v = v_ref[...]

    xg = _dot_split(gk_ref[...], wgk_ref[...]) + bgk_ref[...]
    g = -(jnp.maximum(-xg, 0.0) + jnp.log(1.0 + jnp.exp(-jnp.abs(xg)))) * (1.0 / GLA_TAU)

    g_hi, g_lo = _split_bf16(g)
    tri = tri_ref[...]
    b = (jnp.dot(tri, g_hi, preferred_element_type=F32)
         + jnp.dot(tri, g_lo, preferred_element_type=F32))

    s_intra = jnp.zeros((tt, tt), F32)
    for li, h in enumerate(_GLA_LEVELS):
        b3 = b.reshape(tt // (2 * h), 2 * h, GLA_DK)
        bm = b3[:, h - 1:h, :]
        eq = jnp.exp(jnp.minimum(b3 - bm, 0.0)).reshape(tt, GLA_DK)
        ek = jnp.exp(jnp.minimum(bm - b3, 0.0)).reshape(tt, GLA_DK)
        sc = lax.dot_general((q * eq).astype(BF16), (k * ek).astype(BF16), _NT,
                             preferred_element_type=F32)
        s_intra = s_intra + sc * mlev_ref[li]
    nb = tt // _GLA_BOT
    b3 = b.reshape(nb, _GLA_BOT, GLA_DK)
    q3 = q.reshape(nb, _GLA_BOT, GLA_DK)
    k3 = k.reshape(nb, _GLA_BOT, GLA_DK)
    ones = ones_ref[...]
    for s in range(_GLA_BOT):
        e = jnp.exp(jnp.minimum(b3 - b3[:, s:s + 1, :], 0.0))
        a = (q3 * e * k3[:, s:s + 1, :]).reshape(tt, GLA_DK).astype(BF16)
        s_intra = s_intra + jnp.dot(a, ones, preferred_element_type=F32) * mbot_ref[s]
    o_intra = jnp.dot(s_intra.astype(BF16), v, preferred_element_type=F32)

    st = st_ref[...]
    outs = []
    for c in range(tt // CHUNK):
        lo = c * CHUNK
        bc = b[lo:lo + CHUNK]
        bl = b[lo + CHUNK - 1:lo + CHUNK]
        qe = (q[lo:lo + CHUNK] * jnp.exp(bc)).astype(BF16)
        outs.append(lax.dot_general(qe, st.astype(BF16), _NT, preferred_element_type=F32))
        kd = (k[lo:lo + CHUNK] * jnp.exp(bl - bc)).astype(BF16)
        upd = lax.dot_general(v[lo:lo + CHUNK], kd, _TN, preferred_element_type=F32)
        st = st * jnp.exp(bl) + upd
    st_ref[...] = st
    o = o_intra + jnp.concatenate(outs, axis=0)

    r = r_ref[...].astype(F32)
    ya = _rms(o, gn_ref[...]) * (r * jax.nn.sigmoid(r))
    o_ref[...] = (jax.nn.sigmoid(ga_ref[...].astype(F32)) * ya).astype(BF16)


def _gla(z3, gk3, wgk_pad, bgk, gnorm, col):
    b, t, _ = z3.shape
    tt = GLA_TT
    tri, mlev, mbot, ones = _gla_consts(tt)

    def zspec(width, off):
        return pl.BlockSpec((None, tt, width), lambda bi, h, ti, o=off // width: (bi, ti, o + h))

    const2 = lambda bi, h, ti: (0, 0)
    const3 = lambda bi, h, ti: (0, 0, 0)
    return pl.pallas_call(
        _gla_kernel,
        grid=(b, GLA_HEADS, t // tt),
        in_specs=[
            zspec(GLA_DK, col["q_g"]), zspec(GLA_DK, col["k_g"]), zspec(GLA_DV, col["v_g"]),
            zspec(GLA_DV, col["r_g"]), zspec(GLA_DV, col["gt_a"]),
            pl.BlockSpec((None, tt, LANES), lambda bi, h, ti: (bi, ti, 0)),
            pl.BlockSpec((LANES, GLA_DK), lambda bi, h, ti: (0, h)),
            pl.BlockSpec((1, GLA_DK), lambda bi, h, ti: (0, h)),
            pl.BlockSpec((1, GLA_DV), const2),
            pl.BlockSpec((tt, tt), const2),
            pl.BlockSpec((len(_GLA_LEVELS), tt, tt), const3),
            pl.BlockSpec((_GLA_BOT, tt, tt), const3),
            pl.BlockSpec((GLA_DK, tt), const2),
        ],
        out_specs=pl.BlockSpec((None, tt, GLA_DV), lambda bi, h, ti: (bi, ti, h)),
        out_shape=jax.ShapeDtypeStruct((b, t, GLA_HEADS * GLA_DV), BF16),
        scratch_shapes=[pltpu.VMEM((GLA_DV, GLA_DK), F32)],
        compiler_params=_cparams(("parallel", "parallel", "arbitrary")),
        name="gla",
    )(z3, z3, z3, z3, z3, gk3, wgk_pad, bgk, gnorm, tri, mlev, mbot, ones)


_ATT_NKB = 3


def _attn_bias(rel_bias):
    tq = ATT_TQ
    t = np.arange(tq)[:, None]
    w = np.arange(_ATT_NKB * tq)[None, :]
    s = w - (_ATT_NKB - 1) * tq
    rel = np.clip(t - s, -REL_CLIP, REL_CLIP) + REL_CLIP
    dc = t // CHUNK - np.floor_divide(s, CHUNK)
    valid = (dc >= 0) & (dc <= ATT_PAST)
    tab = rel_bias.astype(F32)[:, rel]
    return jnp.where(jnp.asarray(valid)[None], tab, -jnp.inf)


def _attn_kernel(q_ref, k0_ref, k1_ref, k2_ref, v0_ref, v1_ref, v2_ref, gb_ref, bias_ref, o_ref):
    j = pl.program_id(2)
    tq = q_ref.shape[0]
    q = q_ref[...]
    ks = (k0_ref[...], k1_ref[...], k2_ref[...])
    vs = (v0_ref[...], v1_ref[...], v2_ref[...])
    lane = lax.broadcasted_iota(I32, (1, LANES), 1)
    scale = ATT_DH ** -0.5
    heads = []
    for hh in range(LANES // ATT_DH):
        in_head = (lane // ATT_DH) == hh
        qh = jnp.where(in_head, q, jnp.zeros_like(q))
        parts = []
        for i in range(_ATT_NKB):
            s = lax.dot_general(qh, ks[i], _NT, preferred_element_type=F32)
            s = s * scale + bias_ref[hh, :, i * tq:(i + 1) * tq]
            if i < _ATT_NKB - 1:
                s = s + jnp.where(j - (_ATT_NKB - 1) + i >= 0, 0.0, -jnp.inf)
            parts.append(s)
        m = functools.reduce(jnp.maximum, [jnp.max(p, axis=-1, keepdims=True) for p in parts])
        ps = [jnp.exp(p - m) for p in parts]
        l = functools.reduce(jnp.add, [jnp.sum(p, axis=-1, keepdims=True) for p in ps])
        pv = functools.reduce(
            jnp.add, [jnp.dot(p.astype(BF16), vv, preferred_element_type=F32) for p, vv in zip(ps, vs)])
        heads.append(pv / l)
    o = jnp.where((lane // ATT_DH) == 0, heads[0], heads[1])
    o_ref[...] = (jax.nn.sigmoid(gb_ref[...].astype(F32)) * o).astype(BF16)


def _attn(z3, bias, col):
    b, t, _ = z3.shape
    tq = ATT_TQ
    hp = ATT_HEADS * ATT_DH // LANES

    def cur(off):
        return pl.BlockSpec((None, tq, LANES), lambda h, bi, j, o=off // LANES: (bi, j, o + h))

    def past(off, back):
        return pl.BlockSpec((None, tq, LANES),
                            lambda h, bi, j, o=off // LANES: (bi, jnp.maximum(j - back, 0), o + h))

    return pl.pallas_call(
        _attn_kernel,
        grid=(hp, b, t // tq),
        in_specs=[
            cur(col["q_a"]),
            past(col["k_a"], 2), past(col["k_a"], 1), cur(col["k_a"]),
            past(col["v_a"], 2), past(col["v_a"], 1), cur(col["v_a"]),
            cur(col["gt_b"]),
            pl.BlockSpec((LANES // ATT_DH, tq, _ATT_NKB * tq), lambda h, bi, j: (h, 0, 0)),
        ],
        out_specs=pl.BlockSpec((None, tq, LANES), lambda h, bi, j: (bi, j, h)),
        out_shape=jax.ShapeDtypeStruct((b, t, ATT_HEADS * ATT_DH), BF16),
        compiler_params=_cparams(("parallel", "parallel", "parallel")),
        name="band_attn",
    )(z3, z3, z3, z3, z3, z3, z3, z3, bias)


def _outproj_kernel(x_ref, ya_ref, yb_ref, wo_ref, lnm_ref, wr_ref, br_ref,
                    x1_ref, xm_ref, topi_ref, topw_ref, cnt_ref):
    h = (ya_ref[...].astype(F32) + yb_ref[...].astype(F32)).astype(BF16)
    x1 = x_ref[...] + jnp.dot(h, wo_ref[...], preferred_element_type=F32)
    x1_ref[...] = x1
    xm = _rms(x1, lnm_ref[...])
    xm_ref[...] = xm

    tm = xm.shape[0]
    lane = lax.broadcasted_iota(I32, (tm, LANES), 1)
    lanef = lane.astype(F32)
    logits = _dot_split(xm, wr_ref[...]) + br_ref[...]
    l = jnp.where(lane < N_EXPERTS, logits, -jnp.inf)
    vals, idxs = [], []
    picked = jnp.zeros((tm, LANES), F32)
    for _ in range(TOP_K):
        m = jnp.max(l, axis=-1, keepdims=True)
        idx = jnp.min(jnp.where(l == m, lanef, float(LANES)), axis=-1, keepdims=True)
        vals.append(m)
        idxs.append(idx)
        hit = lanef == idx
        picked = jnp.where(hit, 1.0, picked)
        l = jnp.where(hit, -jnp.inf, l)
    es = [jnp.exp(vv - vals[0]) for vv in vals]
    tot = functools.reduce(jnp.add, es)
    topw = jnp.zeros((tm, LANES), F32)
    topi = jnp.zeros((tm, LANES), F32)
    for kk in range(TOP_K):
        topw = jnp.where(lane == kk, es[kk] / tot, topw)
        topi = jnp.where(lane == kk, idxs[kk], topi)
    topw_ref[...] = topw
    topi_ref[...] = topi.astype(I32)

    @pl.when(pl.program_id(0) == 0)
    def _():
        cnt_ref[...] = jnp.zeros_like(cnt_ref)

    cnt_ref[...] = cnt_ref[...] + jnp.sum(picked, axis=0, keepdims=True)


def _outproj(x2, ya, yb, w_out, ln_moe, wr_pad, br_pad):
    n, d = x2.shape
    tm = OUT_TM
    row = lambda i: (i, 0)
    const = lambda i: (0, 0)
    return pl.pallas_call(
        _outproj_kernel,
        grid=(n // tm,),
        in_specs=[
            pl.BlockSpec((tm, d), row), pl.BlockSpec((tm, d), row), pl.BlockSpec((tm, d), row),
            pl.BlockSpec((d, d), const), pl.BlockSpec((1, d), const),
            pl.BlockSpec((d, LANES), const), pl.BlockSpec((1, LANES), const),
        ],
        out_specs=[
            pl.BlockSpec((tm, d), row), pl.BlockSpec((tm, d), row),
            pl.BlockSpec((tm, LANES), row), pl.BlockSpec((tm, LANES), row),
            pl.BlockSpec((1, LANES), const),
        ],
        out_shape=[
            jax.ShapeDtypeStruct((n, d), F32), jax.ShapeDtypeStruct((n, d), F32),
            jax.ShapeDtypeStruct((n, LANES), I32), jax.ShapeDtypeStruct((n, LANES), F32),
            jax.ShapeDtypeStruct((1, LANES), F32),
        ],
        compiler_params=_cparams(("arbitrary",)),
        name="out_proj_router",
    )(x2, ya, yb, w_out, ln_moe, wr_pad, br_pad)


def _pos_kernel(topi_ref, start_ref, tri_ref, pos_ref, carry_ref):
    @pl.when(pl.program_id(0) == 0)
    def _():
        carry_ref[...] = start_ref[...]

    ti = topi_ref[...]
    tb = ti.shape[0]
    lane = lax.broadcasted_iota(I32, (tb, LANES), 1)
    sel = [lane == ti[:, kk:kk + 1] for kk in range(TOP_K)]
    oh = functools.reduce(jnp.add, [jnp.where(s, 1.0, 0.0) for s in sel])
    row = carry_ref[...] + jnp.dot(tri_ref[...], oh.astype(BF16), preferred_element_type=F32)
    out = jnp.zeros((tb, LANES), F32)
    for kk in range(TOP_K):
        rk = jnp.sum(jnp.where(sel[kk], row, 0.0), axis=-1, keepdims=True)
        out = jnp.where(lane == kk, rk, out)
    pos_ref[...] = out.astype(I32)
    carry_ref[...] = carry_ref[...] + jnp.sum(oh, axis=0, keepdims=True)


def _positions(topi, start_rows):
    n = topi.shape[0]
    tb = RANK_TB
    t = np.arange(tb)
    tri = jnp.asarray((t[None, :] < t[:, None]).astype(np.float32), BF16)
    return pl.pallas_call(
        _pos_kernel,
        grid=(n // tb,),
        in_specs=[pl.BlockSpec((tb, LANES), lambda i: (i, 0)),
                  pl.BlockSpec((1, LANES), lambda i: (0, 0)),
                  pl.BlockSpec((tb, tb), lambda i: (0, 0))],
        out_specs=pl.BlockSpec((tb, LANES), lambda i: (i, 0)),
        out_shape=jax.ShapeDtypeStruct((n, LANES), I32),
        scratch_shapes=[pltpu.VMEM((1, LANES), F32)],
        compiler_params=_cparams(("arbitrary",)),
        name="route_positions",
    )(topi, start_rows, tri)


def _row_copy(src, src_row, dst, dst_row, sem):
    return pltpu.make_async_copy(src.at[pl.ds(src_row, 1)], dst.at[pl.ds(dst_row, 1)], sem)


def _rows_wait(ref, nrows, sem):
    pltpu.make_async_copy(ref.at[pl.ds(0, nrows)], ref.at[pl.ds(0, nrows)], sem).wait()


def _dispatch_kernel(tv_ref, pos_hbm, xm_ref, xs_hbm, pos_smem, zeros_ref, sem_pos, sem_rows, sem_fill):
    i = pl.program_id(0)
    nsteps = pl.num_programs(0)
    tokens = xm_ref.shape[0]
    ch = tokens * TOP_K
    slot = i % 2

    def pos_copy(step, sl):
        return pltpu.make_async_copy(pos_hbm.at[pl.ds(step * ch, ch)], pos_smem.at[sl], sem_pos.at[sl])

    @pl.when(i == 0)
    def _():
        pos_copy(0, 0).start()
        tm = zeros_ref.shape[0]
        zeros_ref[...] = jnp.zeros_like(zeros_ref)

        def fill(t):
            return pltpu.make_async_copy(zeros_ref, xs_hbm.at[pl.ds(t * tm, tm)], sem_fill)

        def start(t, carry):
            @pl.when(tv_ref[t] < tm)
            def _():
                fill(t).start()
            return carry

        def wait(t, carry):
            @pl.when(tv_ref[t] < tm)
            def _():
                fill(t).wait()
            return carry

        ntiles = xs_hbm.shape[0] // tm
        lax.fori_loop(0, ntiles, start, 0)
        lax.fori_loop(0, ntiles, wait, 0)

    @pl.when(i + 1 < nsteps)
    def _():
        pos_copy(i + 1, 1 - slot).start()

    pos_copy(i, slot).wait()

    def body(t, carry):
        for kk in range(TOP_K):
            _row_copy(xm_ref, t, xs_hbm, pos_smem[slot, t * TOP_K + kk], sem_rows).start()
        return carry

    lax.fori_loop(0, tokens, body, 0, unroll=DMA_UNROLL)
    _rows_wait(xs_hbm, ch, sem_rows)


def _dispatch(tile_valid, pos_flat, xm, rows_total):
    n, d = xm.shape
    tokens = DISPATCH_TOKENS
    return pl.pallas_call(
        _dispatch_kernel,
        grid_spec=pltpu.PrefetchScalarGridSpec(
            num_scalar_prefetch=1,
            grid=(n // tokens,),
            in_specs=[pl.BlockSpec(memory_space=pl.ANY),
                      pl.BlockSpec((tokens, d), lambda i, tv: (i, 0))],
            out_specs=pl.BlockSpec(memory_space=pl.ANY),
            scratch_shapes=[pltpu.SMEM((2, tokens * TOP_K), I32), pltpu.VMEM((EXP_TM, d), F32),
                            pltpu.SemaphoreType.DMA((2,)), pltpu.SemaphoreType.DMA,
                            pltpu.SemaphoreType.DMA],
        ),
        out_shape=jax.ShapeDtypeStruct((rows_total, d), F32),
        compiler_params=_cparams(("arbitrary",)),
        name="dispatch",
    )(tile_valid, pos_flat, xm)


def _pair_order(a, axis):
    shp = a.shape
    f = shp[axis]
    a = a.reshape(shp[:axis] + (f // LANES, 2, LANES // 2) + shp[axis + 1:])
    return jnp.swapaxes(a, axis + 1, axis + 2).reshape(shp)


def _expert_kernel(te_ref, tv_ref, xs_ref, w1_ref, b1g_ref, b1l_ref, w2_ref, b2_ref, ys_ref,
                   w1g_ref, w1l_ref):
    t = pl.program_id(0)
    nvalid = tv_ref[t]
    new_expert = (t == 0) | (te_ref[t] != te_ref[jnp.maximum(t - 1, 0)])

    @pl.when((nvalid > 0) & new_expert)
    def _():
        even = lax.broadcasted_iota(I32, (1, LANES), 1) % 2 == 0
        for m in range(w1g_ref.shape[1] // LANES):
            a = w1_ref[:, 2 * m * LANES:(2 * m + 1) * LANES]
            b = w1_ref[:, (2 * m + 1) * LANES:(2 * m + 2) * LANES]
            w1g_ref[:, m * LANES:(m + 1) * LANES] = jnp.where(even, a, pltpu.roll(b, 1, 1)).astype(BF16)
            w1l_ref[:, m * LANES:(m + 1) * LANES] = jnp.where(even, pltpu.roll(a, LANES - 1, 1), b).astype(BF16)

    @pl.when(nvalid > 0)
    def _():
        x = xs_ref[...].astype(BF16)
        hg = jnp.dot(x, w1g_ref[...], preferred_element_type=F32) + b1g_ref[...]
        hl = jnp.dot(x, w1l_ref[...], preferred_element_type=F32) + b1l_ref[...]
        glu = jnp.minimum(hg, SWIGLU_LIMIT)
        lin = jnp.clip(hl, -SWIGLU_LIMIT, SWIGLU_LIMIT)
        act = glu * jax.nn.sigmoid(SWIGLU_ALPHA * glu) * (lin + 1.0)
        ys_ref[...] = jnp.dot(act.astype(BF16), w2_ref[...], preferred_element_type=F32) + b2_ref[...]

    @pl.when(nvalid <= 0)
    def _():
        ys_ref[...] = jnp.zeros_like(ys_ref)


def _experts(tile_expert, tile_valid, xs, w1, b1g, b1l, w2, b2):
    p, d = xs.shape
    f = w2.shape[1]
    tm = EXP_TM
    wmap = lambda t, te, tv: (te[t], 0, 0)
    return pl.pallas_call(
        _expert_kernel,
        grid_spec=pltpu.PrefetchScalarGridSpec(
            num_scalar_prefetch=2,
            grid=(p // tm,),
            in_specs=[
                pl.BlockSpec((tm, d), lambda t, te, tv: (t, 0)),
                pl.BlockSpec((None, d, 2 * f), wmap),
                pl.BlockSpec((None, 1, f), wmap), pl.BlockSpec((None, 1, f), wmap),
                pl.BlockSpec((None, f, d), wmap), pl.BlockSpec((None, 1, d), wmap),
            ],
            out_specs=pl.BlockSpec((tm, d), lambda t, te, tv: (t, 0)),
            scratch_shapes=[pltpu.VMEM((d, f), BF16), pltpu.VMEM((d, f), BF16)],
        ),
        out_shape=jax.ShapeDtypeStruct((p, d), F32),
        compiler_params=_cparams(("arbitrary",)),
        name="experts",
    )(tile_expert, tile_valid, xs, w1, b1g, b1l, w2, b2)


def _ple_kernel(pos_hbm, ys_hbm, topw_ref, x1_ref, p_ref, lnp_ref, wpg_ref, wpp_ref, lnf_ref, o_ref,
                pos_smem, ybuf, sem_pos, sem_rows):
    i = pl.program_id(0)
    nsteps = pl.num_programs(0)
    tb = x1_ref.shape[0]
    ch = tb * TOP_K
    slot = i % 2

    def pos_copy(step, sl):
        return pltpu.make_async_copy(pos_hbm.at[pl.ds(step * ch, ch)], pos_smem.at[sl], sem_pos.at[sl])

    def issue_gathers(sl):
        def body(t, carry):
            for kk in range(TOP_K):
                _row_copy(ys_hbm, pos_smem[sl, t * TOP_K + kk], ybuf.at[sl, kk], t,
                          sem_rows.at[sl]).start()
            return carry
        lax.fori_loop(0, tb, body, 0, unroll=DMA_UNROLL)

    @pl.when(i == 0)
    def _():
        pos_copy(0, 0).start()
        pos_copy(0, 0).wait()
        issue_gathers(0)

        @pl.when(nsteps > 1)
        def _():
            pos_copy(1, 1).start()

    @pl.when(i + 1 < nsteps)
    def _():
        pos_copy(i + 1, 1 - slot).wait()
        issue_gathers(1 - slot)

    @pl.when(i + 2 < nsteps)
    def _():
        pos_copy(i + 2, slot).start()

    for kk in range(TOP_K):
        _rows_wait(ybuf.at[slot, kk], tb, sem_rows.at[slot])

    topw = topw_ref[...]
    moe = functools.reduce(jnp.add, [topw[:, kk:kk + 1] * ybuf[slot, kk] for kk in range(TOP_K)])
    x2 = x1_ref[...] + moe
    gate = jax.nn.sigmoid(jnp.dot(_rms(x2, lnp_ref[...]).astype(BF16), wpg_ref[...],
                                  preferred_element_type=F32))
    proj = jnp.dot(p_ref[...].astype(BF16), wpp_ref[...], preferred_element_type=F32)
    o_ref[...] = _rms(x2 + gate * proj, lnf_ref[...])


def _ple(pos_flat, ys, topw, x1, p2, ln_ple, wpg, wpp, ln_final):
    n, d = x1.shape
    pd = p2.shape[1]
    tb = PLE_TB
    row = lambda i: (i, 0)
    const = lambda i: (0, 0)
    return pl.pallas_call(
        _ple_kernel,
        grid=(n // tb,),
        in_specs=[
            pl.BlockSpec(memory_space=pl.ANY), pl.BlockSpec(memory_space=pl.ANY),
            pl.BlockSpec((tb, LANES), row), pl.BlockSpec((tb, d), row), pl.BlockSpec((tb, pd), row),
            pl.BlockSpec((1, d), const), pl.BlockSpec((d, d), const),
            pl.BlockSpec((pd, d), const), pl.BlockSpec((1, d), const),
        ],
        out_specs=pl.BlockSpec((tb, d), row),
        out_shape=jax.ShapeDtypeStruct((n, d), F32),
        scratch_shapes=[
            pltpu.SMEM((2, tb * TOP_K), I32),
            pltpu.VMEM((2, TOP_K, tb, d), F32),
            pltpu.SemaphoreType.DMA((2,)),
            pltpu.SemaphoreType.DMA((2,)),
        ],
        compiler_params=_cparams(("arbitrary",)),
        name="combine_ple_final",
    )(pos_flat, ys, topw, x1, p2, ln_ple, wpg, wpp, ln_final)


def _layer(x2, p2, bsz, seq, ln_mix, w_in, w_gk, b_gk, gla_norm, rel_bias, w_out, ln_moe,
           w_router, b_router, w1, b1, w2, b2, ln_ple, w_ple_gate, w_ple_proj, ln_out):
    n, d = x2.shape
    qk, gv, aw = GLA_HEADS * GLA_DK, GLA_HEADS * GLA_DV, ATT_HEADS * ATT_DH
    names = ("q_g", "k_g", "v_g", "gk_low", "r_g", "q_a", "k_a", "v_a", "gt_a", "gt_b")
    widths = (qk, qk, gv, GLA_RANK, gv, aw, aw, aw, d, d)
    src = dict(zip(names, np.cumsum((0,) + widths[:-1]).tolist()))
    wid = dict(zip(names, widths))
    order = [nm for nm in names if nm != "gk_low"]
    col, off = {}, 0
    for nm in order:
        col[nm] = off
        off += wid[nm]
    w_main = jnp.concatenate([w_in[:, src[nm]:src[nm] + wid[nm]] for nm in order], axis=1).astype(BF16)
    w_low = jnp.pad(w_in[:, src["gk_low"]:src["gk_low"] + GLA_RANK],
                    ((0, 0), (0, LANES - GLA_RANK))).astype(BF16)

    z, gk = _in_proj(x2, ln_mix.reshape(1, d), w_main, w_low)
    z3 = z.reshape(bsz, seq, -1)
    gk3 = gk.reshape(bsz, seq, LANES)

    wgk_pad = jnp.pad(w_gk, ((0, LANES - GLA_RANK), (0, 0)))
    ya = _gla(z3, gk3, wgk_pad, b_gk.reshape(1, qk), gla_norm.reshape(1, GLA_DV), col)
    yb = _attn(z3, _attn_bias(rel_bias), col)

    wr_pad = jnp.pad(w_router, ((0, 0), (0, LANES - N_EXPERTS)))
    br_pad = jnp.pad(b_router, (0, LANES - N_EXPERTS)).reshape(1, LANES)
    x1, xm, topi, topw, cnt = _outproj(x2, ya.reshape(n, d), yb.reshape(n, d), w_out.astype(BF16),
                                       ln_moe.reshape(1, d), wr_pad, br_pad)

    counts = cnt[0, :N_EXPERTS].astype(I32)
    ntile = (counts + EXP_TM - 1) // EXP_TM
    tile_end = jnp.cumsum(ntile)
    tile_start = tile_end - ntile
    rows_total = n * TOP_K + N_EXPERTS * EXP_TM
    tiles = jnp.arange(rows_total // EXP_TM, dtype=I32)
    onehot = (tiles[:, None] >= tile_start[None, :]) & (tiles[:, None] < tile_end[None, :])
    te = jnp.sum(jnp.where(onehot, jnp.arange(N_EXPERTS, dtype=I32)[None, :], 0), axis=1)
    tv = jnp.sum(jnp.where(onehot, counts[None, :] - (tiles[:, None] - tile_start[None, :]) * EXP_TM, 0),
                 axis=1)
    te = jnp.where(tiles < tile_end[-1], te, N_EXPERTS - 1).astype(I32)
    tv = jnp.clip(tv, 0, EXP_TM).astype(I32)
    start_rows = jnp.pad((tile_start * EXP_TM).astype(F32), (0, LANES - N_EXPERTS)).reshape(1, LANES)

    pos_flat = _positions(topi, start_rows)[:, :TOP_K].reshape(-1)
    xs = _dispatch(tv, pos_flat, xm, rows_total)

    ys = _experts(te, tv, xs, w1,
                  _pair_order(b1[:, 0::2], 1).reshape(N_EXPERTS, 1, -1),
                  _pair_order(b1[:, 1::2], 1).reshape(N_EXPERTS, 1, -1),
                  _pair_order(w2, 1).astype(BF16), b2.reshape(N_EXPERTS, 1, d))

    return _ple(pos_flat, ys, topw, x1, p2, ln_ple.reshape(1, d), w_ple_gate.astype(BF16),
                w_ple_proj.astype(BF16), ln_out.reshape(1, d))


def kernel(x, p, ln_mix, w_in, w_gk, b_gk, gla_norm, rel_bias, w_out, ln_moe, w_router, b_router,
           w1, b1, w2, b2, ln_ple, w_ple_gate, w_ple_proj, ln_final):
    bsz, seq, d = x.shape
    depth = p.shape[0]
    assert depth == 1, "the final RMSNorm is fused into the last layer's kernel"
    x2 = x.reshape(bsz * seq, d)
    out = _layer(x2, p[0].reshape(bsz * seq, -1), bsz, seq, ln_mix[0], w_in[0], w_gk[0], b_gk[0],
                 gla_norm[0], rel_bias[0], w_out[0], ln_moe[0], w_router[0], b_router[0],
                 w1[0], b1[0], w2[0], b2[0], ln_ple[0], w_ple_gate[0], w_ple_proj[0], ln_final)
    return out.reshape(bsz, seq, d)
```

```python
import functools

import numpy as np
import jax
import jax.numpy as jnp
from jax import lax
from jax.experimental import pallas as pl
from jax.experimental.pallas import tpu as pltpu

F32 = jnp.float32
BF16 = jnp.bfloat16
I32 = jnp.int32

LANES = 128
SUBLANES = 8
CHUNK = 64
GLA_HEADS = 4
GLA_DK = 128
GLA_DV = 256
GLA_RANK = 16
GLA_TAU = 16.0
ATT_HEADS = 16
ATT_DH = 64
ATT_PAST = 8
REL_CLIP = 256
N_EXPERTS = 32
TOP_K = 4
SWIGLU_ALPHA = 1.702
SWIGLU_LIMIT = 7.0
EPS = 1e-6

VMEM_LIMIT = 48 * 1024 * 1024

IN_TM, IN_TN = 1024, 1024
GLA_TT = 128
ATT_TQ = 256
ATT_STEP_HEADS = 4
OUT_TM = 512
RANK_TB = 512
EXP_TM = 512
DISPATCH_TOKENS = 512
PLE_TB = 256
DMA_UNROLL = 8


def _cparams(sem):
    return pltpu.CompilerParams(dimension_semantics=sem, vmem_limit_bytes=VMEM_LIMIT)


def _split_bf16(a):
    hi = a.astype(BF16)
    lo = (a - hi.astype(F32)).astype(BF16)
    return hi, lo


def _dot_split(a, b):
    a_hi, a_lo = _split_bf16(a)
    b_hi, b_lo = _split_bf16(b)
    d = functools.partial(jnp.dot, preferred_element_type=F32)
    return d(a_hi, b_hi) + (d(a_hi, b_lo) + d(a_lo, b_hi))


def _rms(x, gain):
    ms = jnp.mean(x * x, axis=-1, keepdims=True)
    return x * lax.rsqrt(ms + EPS) * gain


_NT = (((1,), (1,)), ((), ()))
_TN = (((0,), (0,)), ((), ()))


def _in_proj_kernel(x_ref, g_ref, w_ref, wlow_ref, z_ref, gk_ref, xn_ref):
    @pl.when(pl.program_id(1) == 0)
    def _():
        xn = _rms(x_ref[...], g_ref[...]).astype(BF16)
        xn_ref[...] = xn
        gk_ref[...] = jnp.dot(xn, wlow_ref[...], preferred_element_type=F32)

    z_ref[...] = jnp.dot(xn_ref[...], w_ref[...], preferred_element_type=F32).astype(BF16)


def _in_proj(x2, ln, w_main, w_low):
    n, d = x2.shape
    ncol = w_main.shape[1]
    return pl.pallas_call(
        _in_proj_kernel,
        grid=(n // IN_TM, ncol // IN_TN),
        in_specs=[
            pl.BlockSpec((IN_TM, d), lambda i, j: (i, 0)),
            pl.BlockSpec((1, d), lambda i, j: (0, 0)),
            pl.BlockSpec((d, IN_TN), lambda i, j: (0, j)),
            pl.BlockSpec((d, LANES), lambda i, j: (0, 0)),
        ],
        out_specs=[
            pl.BlockSpec((IN_TM, IN_TN), lambda i, j: (i, j)),
            pl.BlockSpec((IN_TM, LANES), lambda i, j: (i, 0)),
        ],
        out_shape=[
            jax.ShapeDtypeStruct((n, ncol), BF16),
            jax.ShapeDtypeStruct((n, LANES), F32),
        ],
        scratch_shapes=[pltpu.VMEM((IN_TM, d), BF16)],
        compiler_params=_cparams(("parallel", "arbitrary")),
        name="in_proj",
    )(x2, ln, w_main, w_low)


_GLA_LEVELS = (8, 16, 32)
_GLA_BOT = 8


def _gla_consts(tt):
    t = np.arange(tt)
    same_chunk = (t[:, None] // CHUNK) == (t[None, :] // CHUNK)
    tri = (same_chunk & (t[None, :] <= t[:, None])).astype(np.float32)
    mlev = []
    for h in _GLA_LEVELS:
        blk = (t[:, None] // (2 * h)) == (t[None, :] // (2 * h))
        m = blk & ((t[:, None] % (2 * h)) >= h) & ((t[None, :] % (2 * h)) < h)
        mlev.append(m.astype(np.float32))
    mbot = []
    for s in range(_GLA_BOT):
        m = (t[None, :] == (t[:, None] // _GLA_BOT) * _GLA_BOT + s) & ((t[:, None] % _GLA_BOT) >= s)
        mbot.append(m.astype(np.float32))
    return (jnp.asarray(tri, BF16), jnp.asarray(np.stack(mlev), F32),
            jnp.asarray(np.stack(mbot), F32), jnp.ones((GLA_DK, tt), BF16))


def _gla_kernel(q_ref, k_ref, v_ref, r_ref, ga_ref, gk_ref, wgk_ref, bgk_ref, gn_ref,
                tri_ref, mlev_ref, mbot_ref, ones_ref, o_ref, st_ref):
    @pl.when(pl.program_id(1) == 0)
    def _():
        st_ref[...] = jnp.zeros_like(st_ref)

    xg = _dot_split(gk_ref[...], wgk_ref[...]) + bgk_ref[...]
    g = -(jnp.maximum(-xg, 0.0) + jnp.log(1.0 + jnp.exp(-jnp.abs(xg)))) * (1.0 / GLA_TAU)
    g_hi, g_lo = _split_bf16(g)
    tri = tri_ref[...]
    b_all = (jnp.dot(tri, g_hi, preferred_element_type=F32)
             + jnp.dot(tri, g_lo, preferred_element_type=F32))

    for hh in range(GLA_HEADS):
        kc = slice(hh * GLA_DK, (hh + 1) * GLA_DK)
        vc = slice(hh * GLA_DV, (hh + 1) * GLA_DV)
        y, st = _gla_head(q_ref[:, kc], k_ref[:, kc], v_ref[:, vc], b_all[:, kc], st_ref[hh],
                          mlev_ref, mbot_ref, ones_ref[...])
        st_ref[hh] = st
        r = r_ref[:, vc].astype(F32)
        ya = _rms(y, gn_ref[...]) * (r * jax.nn.sigmoid(r))
        o_ref[:, vc] = (jax.nn.sigmoid(ga_ref[:, vc].astype(F32)) * ya).astype(BF16)


def _gla_head(q, k, v, b, st, mlev_ref, mbot_ref, ones):
    tt = q.shape[0]
    q = q.astype(F32) * (GLA_DK ** -0.5)
    k = k.astype(F32)

    s_intra = jnp.zeros((tt, tt), F32)
    for li, h in enumerate(_GLA_LEVELS):
        b3 = b.reshape(tt // (2 * h), 2 * h, GLA_DK)
        bm = b3[:, h - 1:h, :]
        eq = jnp.exp(jnp.minimum(b3 - bm, 0.0)).reshape(tt, GLA_DK)
        ek = jnp.exp(jnp.minimum(bm - b3, 0.0)).reshape(tt, GLA_DK)
        sc = lax.dot_general((q * eq).astype(BF16), (k * ek).astype(BF16), _NT,
                             preferred_element_type=F32)
        s_intra = s_intra + sc * mlev_ref[li]
    nb = tt // _GLA_BOT
    b3 = b.reshape(nb, _GLA_BOT, GLA_DK)
    q3 = q.reshape(nb, _GLA_BOT, GLA_DK)
    k3 = k.reshape(nb, _GLA_BOT, GLA_DK)
    for s in range(_GLA_BOT):
        e = jnp.exp(jnp.minimum(b3 - b3[:, s:s + 1, :], 0.0))
        a = (q3 * e * k3[:, s:s + 1, :]).reshape(tt, GLA_DK).astype(BF16)
        s_intra = s_intra + jnp.dot(a, ones, preferred_element_type=F32) * mbot_ref[s]
    o_intra = jnp.dot(s_intra.astype(BF16), v, preferred_element_type=F32)

    outs = []
    for c in range(tt // CHUNK):
        lo = c * CHUNK
        bc = b[lo:lo + CHUNK]
        bl = b[lo + CHUNK - 1:lo + CHUNK]
        qe = (q[lo:lo + CHUNK] * jnp.exp(bc)).astype(BF16)
        outs.append(lax.dot_general(qe, st.astype(BF16), _NT, preferred_element_type=F32))
        kd = (k[lo:lo + CHUNK] * jnp.exp(bl - bc)).astype(BF16)
        upd = lax.dot_general(v[lo:lo + CHUNK], kd, _TN, preferred_element_type=F32)
        st = st * jnp.exp(bl) + upd
    return o_intra + jnp.concatenate(outs, axis=0), st


def _gla(z3, gk3, wgk_pad, bgk, gnorm, col):
    b, t, _ = z3.shape
    tt = GLA_TT
    tri, mlev, mbot, ones = _gla_consts(tt)
    qk, gv = GLA_HEADS * GLA_DK, GLA_HEADS * GLA_DV

    def zspec(width, off):
        return pl.BlockSpec((None, tt, width), lambda bi, ti, o=off // width: (bi, ti, o))

    const2 = lambda bi, ti: (0, 0)
    const3 = lambda bi, ti: (0, 0, 0)
    return pl.pallas_call(
        _gla_kernel,
        grid=(b, t // tt),
        in_specs=[
            zspec(qk, col["q_g"]), zspec(qk, col["k_g"]), zspec(gv, col["v_g"]),
            zspec(gv, col["r_g"]), zspec(gv, col["gt_a"]),
            pl.BlockSpec((None, tt, LANES), lambda bi, ti: (bi, ti, 0)),
            pl.BlockSpec((LANES, qk), const2),
            pl.BlockSpec((1, qk), const2),
            pl.BlockSpec((1, GLA_DV), const2),
            pl.BlockSpec((tt, tt), const2),
            pl.BlockSpec((len(_GLA_LEVELS), tt, tt), const3),
            pl.BlockSpec((_GLA_BOT, tt, tt), const3),
            pl.BlockSpec((GLA_DK, tt), const2),
        ],
        out_specs=pl.BlockSpec((None, tt, gv), lambda bi, ti: (bi, ti, 0)),
        out_shape=jax.ShapeDtypeStruct((b, t, gv), BF16),
        scratch_shapes=[pltpu.VMEM((GLA_HEADS, GLA_DV, GLA_DK), F32)],
        compiler_params=_cparams(("parallel", "arbitrary")),
        name="gla",
    )(z3, z3, z3, z3, z3, gk3, wgk_pad, bgk, gnorm, tri, mlev, mbot, ones)


_ATT_NKB = 3


def _attn_bias(rel_bias):
    tq = ATT_TQ
    nk = _ATT_NKB * tq
    back = nk - tq
    nheads = rel_bias.shape[0]
    span = nk + tq - 1
    dist = np.clip(np.arange(span) - (tq - 1), -REL_CLIP, REL_CLIP) + REL_CLIP
    g = rel_bias.astype(F32)[:, dist]
    x = jnp.pad(g[:, ::-1], ((0, 0), (0, 1)))
    tab = jnp.tile(x, (1, tq))[:, :tq * span].reshape(nheads, tq, span)[:, :, tq - 1:tq - 1 + nk]
    t = np.arange(tq)[:, None]
    w = np.arange(nk)[None, :]
    dc = t // CHUNK - np.floor_divide(w - back, CHUNK)
    band = (dc >= 0) & (dc <= ATT_PAST)
    valid = np.stack([band & (w // tq >= _ATT_NKB - 1 - e) for e in range(_ATT_NKB)])
    return jnp.where(jnp.asarray(valid)[:, None], tab[None], -jnp.inf)


def _attn_kernel(q_ref, k0_ref, k1_ref, k2_ref, v0_ref, v1_ref, v2_ref, gb_ref, bias_ref, o_ref):
    tq = q_ref.shape[0]
    lane = lax.broadcasted_iota(I32, (1, LANES), 1)
    scale = jnp.asarray(ATT_DH ** -0.5, BF16)
    krefs = (k0_ref, k1_ref, k2_ref)
    vrefs = (v0_ref, v1_ref, v2_ref)
    for lb in range(q_ref.shape[1] // LANES):
        cols = slice(lb * LANES, (lb + 1) * LANES)
        q = q_ref[:, cols] * scale
        ks = [r[:, cols] for r in krefs]
        vs = [r[:, cols] for r in vrefs]
        heads = []
        for hh in range(LANES // ATT_DH):
            in_head = (lane // ATT_DH) == hh
            qh = jnp.where(in_head, q, jnp.zeros_like(q))
            parts = [lax.dot_general(qh, ks[i], _NT, preferred_element_type=F32)
                     + bias_ref[lb * (LANES // ATT_DH) + hh, :, i * tq:(i + 1) * tq]
                     for i in range(_ATT_NKB)]
            m = functools.reduce(jnp.maximum, [jnp.max(p, axis=-1, keepdims=True) for p in parts])
            pv = functools.reduce(jnp.add, [
                jnp.dot(jnp.exp(p - m).astype(BF16), jnp.where(in_head, vv, jnp.ones_like(vv)),
                        preferred_element_type=F32) for p, vv in zip(parts, vs)])
            heads.append(pv / pltpu.roll(pv, ATT_DH, 1))
        o = jnp.where((lane // ATT_DH) == 0, heads[0], heads[1])
        o_ref[:, cols] = (jax.nn.sigmoid(gb_ref[:, cols].astype(F32)) * o).astype(BF16)


def _attn(z3, bias, col):
    b, t, _ = z3.shape
    tq = ATT_TQ
    width = ATT_STEP_HEADS * ATT_DH
    steps = ATT_HEADS // ATT_STEP_HEADS

    def cur(off):
        return pl.BlockSpec((None, tq, width), lambda h, bi, j, o=off // width: (bi, j, o + h))

    def past(off, back):
        return pl.BlockSpec((None, tq, width),
                            lambda h, bi, j, o=off // width: (bi, jnp.maximum(j - back, 0), o + h))

    return pl.pallas_call(
        _attn_kernel,
        grid=(steps, b, t // tq),
        in_specs=[
            cur(col["q_a"]),
            past(col["k_a"], 2), past(col["k_a"], 1), cur(col["k_a"]),
            past(col["v_a"], 2), past(col["v_a"], 1), cur(col["v_a"]),
            cur(col["gt_b"]),
            pl.BlockSpec((None, ATT_STEP_HEADS, tq, _ATT_NKB * tq),
                         lambda h, bi, j: (jnp.minimum(j, _ATT_NKB - 1), h, 0, 0)),
        ],
        out_specs=pl.BlockSpec((None, tq, width), lambda h, bi, j: (bi, j, h)),
        out_shape=jax.ShapeDtypeStruct((b, t, ATT_HEADS * ATT_DH), BF16),
        compiler_params=_cparams(("parallel", "parallel", "parallel")),
        name="band_attn",
    )(z3, z3, z3, z3, z3, z3, z3, z3, bias)


def _outproj_kernel(x_ref, ya_ref, yb_ref, wo_ref, lnm_ref, wr_ref, br_ref,
                    x1_ref, xm_ref, topi_ref, topw_ref, cnt_ref):
    h = (ya_ref[...].astype(F32) + yb_ref[...].astype(F32)).astype(BF16)
    x1 = x_ref[...] + jnp.dot(h, wo_ref[...], preferred_element_type=F32)
    x1_ref[...] = x1
    xm = _rms(x1, lnm_ref[...])
    xm_ref[...] = xm

    tm = xm.shape[0]
    lane = lax.broadcasted_iota(I32, (tm, LANES), 1)
    lanef = lane.astype(F32)
    logits = _dot_split(xm, wr_ref[...]) + br_ref[...]
    l = jnp.where(lane < N_EXPERTS, logits, -jnp.inf)
    vals, idxs = [], []
    picked = jnp.zeros((tm, LANES), F32)
    for _ in range(TOP_K):
        m = jnp.max(l, axis=-1, keepdims=True)
        idx = jnp.min(jnp.where(l == m, lanef, float(LANES)), axis=-1, keepdims=True)
        vals.append(m)
        idxs.append(idx)
        hit = lanef == idx
        picked = jnp.where(hit, 1.0, picked)
        l = jnp.where(hit, -jnp.inf, l)
    es = [jnp.exp(vv - vals[0]) for vv in vals]
    tot = functools.reduce(jnp.add, es)
    topw = jnp.zeros((tm, LANES), F32)
    topi = jnp.zeros((tm, LANES), F32)
    for kk in range(TOP_K):
        topw = jnp.where(lane == kk, es[kk] / tot, topw)
        topi = jnp.where(lane == kk, idxs[kk], topi)
    topw_ref[...] = topw
    topi_ref[...] = topi.astype(I32)

    @pl.when(pl.program_id(0) == 0)
    def _():
        cnt_ref[...] = jnp.zeros_like(cnt_ref)

    cnt_ref[...] = cnt_ref[...] + jnp.sum(picked, axis=0, keepdims=True)


def _outproj(x2, ya, yb, w_out, ln_moe, wr_pad, br_pad):
    n, d = x2.shape
    tm = OUT_TM
    row = lambda i: (i, 0)
    const = lambda i: (0, 0)
    return pl.pallas_call(
        _outproj_kernel,
        grid=(n // tm,),
        in_specs=[
            pl.BlockSpec((tm, d), row), pl.BlockSpec((tm, d), row), pl.BlockSpec((tm, d), row),
            pl.BlockSpec((d, d), const), pl.BlockSpec((1, d), const),
            pl.BlockSpec((d, LANES), const), pl.BlockSpec((1, LANES), const),
        ],
        out_specs=[
            pl.BlockSpec((tm, d), row), pl.BlockSpec((tm, d), row),
            pl.BlockSpec((tm, LANES), row), pl.BlockSpec((tm, LANES), row),
            pl.BlockSpec((1, LANES), const),
        ],
        out_shape=[
            jax.ShapeDtypeStruct((n, d), F32), jax.ShapeDtypeStruct((n, d), F32),
            jax.ShapeDtypeStruct((n, LANES), I32), jax.ShapeDtypeStruct((n, LANES), F32),
            jax.ShapeDtypeStruct((1, LANES), F32),
        ],
        compiler_params=_cparams(("arbitrary",)),
        name="out_proj_router",
    )(x2, ya, yb, w_out, ln_moe, wr_pad, br_pad)


def _pos_kernel(topi_ref, start_ref, tri_ref, pos_ref, carry_ref):
    @pl.when(pl.program_id(0) == 0)
    def _():
        carry_ref[...] = start_ref[...]

    ti = topi_ref[...]
    tb = ti.shape[0]
    lane = lax.broadcasted_iota(I32, (tb, LANES), 1)
    sel = [lane == ti[:, kk:kk + 1] for kk in range(TOP_K)]
    oh = functools.reduce(jnp.add, [jnp.where(s, 1.0, 0.0) for s in sel])
    row = carry_ref[...] + jnp.dot(tri_ref[...], oh.astype(BF16), preferred_element_type=F32)
    out = jnp.zeros((tb, LANES), F32)
    for kk in range(TOP_K):
        rk = jnp.sum(jnp.where(sel[kk], row, 0.0), axis=-1, keepdims=True)
        out = jnp.where(lane == kk, rk, out)
    pos_ref[...] = jnp.transpose(out)[:pos_ref.shape[0]].astype(I32)
    carry_ref[...] = carry_ref[...] + jnp.sum(oh, axis=0, keepdims=True)


def _positions(topi, start_rows):
    n = topi.shape[0]
    tb = RANK_TB
    t = np.arange(tb)
    tri = jnp.asarray((t[None, :] < t[:, None]).astype(np.float32), BF16)
    return pl.pallas_call(
        _pos_kernel,
        grid=(n // tb,),
        in_specs=[pl.BlockSpec((tb, LANES), lambda i: (i, 0)),
                  pl.BlockSpec((1, LANES), lambda i: (0, 0)),
                  pl.BlockSpec((tb, tb), lambda i: (0, 0))],
        out_specs=pl.BlockSpec((SUBLANES, tb), lambda i: (0, i)),
        out_shape=jax.ShapeDtypeStruct((SUBLANES, n), I32),
        scratch_shapes=[pltpu.VMEM((1, LANES), F32)],
        compiler_params=_cparams(("arbitrary",)),
        name="route_positions",
    )(topi, start_rows, tri)


def _row_copy(src, src_row, dst, dst_row, sem):
    return pltpu.make_async_copy(src.at[pl.ds(src_row, 1)], dst.at[pl.ds(dst_row, 1)], sem)


def _rows_wait(ref, nrows, sem):
    pltpu.make_async_copy(ref.at[pl.ds(0, nrows)], ref.at[pl.ds(0, nrows)], sem).wait()


def _dispatch_kernel(tv_ref, pos_hbm, xm_ref, xs_hbm, pos_smem, zeros_ref, sem_pos, sem_rows, sem_fill):
    i = pl.program_id(0)
    nsteps = pl.num_programs(0)
    tokens = xm_ref.shape[0]
    ch = tokens * TOP_K
    slot = i % 2

    def pos_copy(step, sl):
        return pltpu.make_async_copy(pos_hbm.at[pl.ds(step * ch, ch)], pos_smem.at[sl], sem_pos.at[sl])

    @pl.when(i == 0)
    def _():
        pos_copy(0, 0).start()
        tm = zeros_ref.shape[0]
        zeros_ref[...] = jnp.zeros_like(zeros_ref)

        def fill(t):
            return pltpu.make_async_copy(zeros_ref, xs_hbm.at[pl.ds(t * tm, tm)], sem_fill)

        def start(t, carry):
            @pl.when(tv_ref[t] < tm)
            def _():
                fill(t).start()
            return carry

        def wait(t, carry):
            @pl.when(tv_ref[t] < tm)
            def _():
                fill(t).wait()
            return carry

        ntiles = xs_hbm.shape[0] // tm
        lax.fori_loop(0, ntiles, start, 0)
        lax.fori_loop(0, ntiles, wait, 0)

    @pl.when(i + 1 < nsteps)
    def _():
        pos_copy(i + 1, 1 - slot).start()

    pos_copy(i, slot).wait()

    def body(t, carry):
        for kk in range(TOP_K):
            _row_copy(xm_ref, t, xs_hbm, pos_smem[slot, kk * tokens + t], sem_rows).start()
        return carry

    lax.fori_loop(0, tokens, body, 0, unroll=DMA_UNROLL)
    _rows_wait(xs_hbm, ch, sem_rows)


def _dispatch(tile_valid, pos_flat, xm, rows_total):
    n, d = xm.shape
    tokens = DISPATCH_TOKENS
    return pl.pallas_call(
        _dispatch_kernel,
        grid_spec=pltpu.PrefetchScalarGridSpec(
            num_scalar_prefetch=1,
            grid=(n // tokens,),
            in_specs=[pl.BlockSpec(memory_space=pl.ANY),
                      pl.BlockSpec((tokens, d), lambda i, tv: (i, 0))],
            out_specs=pl.BlockSpec(memory_space=pl.ANY),
            scratch_shapes=[pltpu.SMEM((2, tokens * TOP_K), I32), pltpu.VMEM((EXP_TM, d), F32),
                            pltpu.SemaphoreType.DMA((2,)), pltpu.SemaphoreType.DMA,
                            pltpu.SemaphoreType.DMA],
        ),
        out_shape=jax.ShapeDtypeStruct((rows_total, d), F32),
        compiler_params=_cparams(("arbitrary",)),
        name="dispatch",
    )(tile_valid, pos_flat, xm)


def _pair_order(a, axis):
    shp = a.shape
    f = shp[axis]
    a = a.reshape(shp[:axis] + (f // LANES, 2, LANES // 2) + shp[axis + 1:])
    return jnp.swapaxes(a, axis + 1, axis + 2).reshape(shp)


def _expert_kernel(te_ref, tv_ref, xs_ref, w1_ref, b1g_ref, b1l_ref, w2_ref, b2_ref, ys_ref,
                   w1g_ref, w1l_ref):
    t = pl.program_id(0)
    nvalid = tv_ref[t]
    new_expert = (t == 0) | (te_ref[t] != te_ref[jnp.maximum(t - 1, 0)])

    @pl.when((nvalid > 0) & new_expert)
    def _():
        even = lax.broadcasted_iota(I32, (1, LANES), 1) % 2 == 0
        for m in range(w1g_ref.shape[1] // LANES):
            a = w1_ref[:, 2 * m * LANES:(2 * m + 1) * LANES]
            b = w1_ref[:, (2 * m + 1) * LANES:(2 * m + 2) * LANES]
            w1g_ref[:, m * LANES:(m + 1) * LANES] = jnp.where(even, a, pltpu.roll(b, 1, 1)).astype(BF16)
            w1l_ref[:, m * LANES:(m + 1) * LANES] = jnp.where(even, pltpu.roll(a, LANES - 1, 1), b).astype(BF16)

    @pl.when(nvalid > 0)
    def _():
        x = xs_ref[...].astype(BF16)
        hg = jnp.dot(x, w1g_ref[...], preferred_element_type=F32) + b1g_ref[...]
        hl = jnp.dot(x, w1l_ref[...], preferred_element_type=F32) + b1l_ref[...]
        glu = jnp.minimum(hg, SWIGLU_LIMIT)
        lin = jnp.clip(hl, -SWIGLU_LIMIT, SWIGLU_LIMIT)
        act = glu * jax.nn.sigmoid(SWIGLU_ALPHA * glu) * (lin + 1.0)
        ys_ref[...] = jnp.dot(act.astype(BF16), w2_ref[...], preferred_element_type=F32) + b2_ref[...]

    @pl.when(nvalid <= 0)
    def _():
        ys_ref[...] = jnp.zeros_like(ys_ref)


def _experts(tile_expert, tile_valid, xs, w1, b1g, b1l, w2, b2):
    p, d = xs.shape
    f = w2.shape[1]
    tm = EXP_TM
    wmap = lambda t, te, tv: (te[t], 0, 0)
    return pl.pallas_call(
        _expert_kernel,
        grid_spec=pltpu.PrefetchScalarGridSpec(
            num_scalar_prefetch=2,
            grid=(p // tm,),
            in_specs=[
                pl.BlockSpec((tm, d), lambda t, te, tv: (t, 0)),
                pl.BlockSpec((None, d, 2 * f), wmap),
                pl.BlockSpec((None, 1, f), wmap), pl.BlockSpec((None, 1, f), wmap),
                pl.BlockSpec((None, f, d), wmap), pl.BlockSpec((None, 1, d), wmap),
            ],
            out_specs=pl.BlockSpec((tm, d), lambda t, te, tv: (t, 0)),
            scratch_shapes=[pltpu.VMEM((d, f), BF16), pltpu.VMEM((d, f), BF16)],
        ),
        out_shape=jax.ShapeDtypeStruct((p, d), F32),
        compiler_params=_cparams(("arbitrary",)),
        name="experts",
    )(tile_expert, tile_valid, xs, w1, b1g, b1l, w2, b2)


def _ple_kernel(pos_hbm, ys_hbm, topw_ref, x1_ref, p_ref, lnp_ref, wpg_ref, wpp_ref, lnf_ref, o_ref,
                pos_smem, ybuf, sem_pos, sem_rows):
    i = pl.program_id(0)
    nsteps = pl.num_programs(0)
    tb = x1_ref.shape[0]
    ch = tb * TOP_K
    slot = i % 2

    def pos_copy(step, sl):
        return pltpu.make_async_copy(pos_hbm.at[pl.ds(step * ch, ch)], pos_smem.at[sl], sem_pos.at[sl])

    def issue_gathers(sl):
        def body(t, carry):
            for kk in range(TOP_K):
                _row_copy(ys_hbm, pos_smem[sl, kk * tb + t], ybuf.at[sl, kk], t,
                          sem_rows.at[sl]).start()
            return carry
        lax.fori_loop(0, tb, body, 0, unroll=DMA_UNROLL)

    @pl.when(i == 0)
    def _():
        pos_copy(0, 0).start()
        pos_copy(0, 0).wait()
        issue_gathers(0)

        @pl.when(nsteps > 1)
        def _():
            pos_copy(1, 1).start()

    @pl.when(i + 1 < nsteps)
    def _():
        pos_copy(i + 1, 1 - slot).wait()
        issue_gathers(1 - slot)

    @pl.when(i + 2 < nsteps)
    def _():
        pos_copy(i + 2, slot).start()

    for kk in range(TOP_K):
        _rows_wait(ybuf.at[slot, kk], tb, sem_rows.at[slot])

    topw = topw_ref[...]
    moe = functools.reduce(jnp.add, [topw[:, kk:kk + 1] * ybuf[slot, kk] for kk in range(TOP_K)])
    x2 = x1_ref[...] + moe
    gate = jax.nn.sigmoid(jnp.dot(_rms(x2, lnp_ref[...]).astype(BF16), wpg_ref[...],
                                  preferred_element_type=F32))
    proj = jnp.dot(p_ref[...].astype(BF16), wpp_ref[...], preferred_element_type=F32)
    o_ref[...] = _rms(x2 + gate * proj, lnf_ref[...])


def _ple(pos_flat, ys, topw, x1, p2, ln_ple, wpg, wpp, ln_final):
    n, d = x1.shape
    pd = p2.shape[1]
    tb = PLE_TB
    row = lambda i: (i, 0)
    const = lambda i: (0, 0)
    return pl.pallas_call(
        _ple_kernel,
        grid=(n // tb,),
        in_specs=[
            pl.BlockSpec(memory_space=pl.ANY), pl.BlockSpec(memory_space=pl.ANY),
            pl.BlockSpec((tb, LANES), row), pl.BlockSpec((tb, d), row), pl.BlockSpec((tb, pd), row),
            pl.BlockSpec((1, d), const), pl.BlockSpec((d, d), const),
            pl.BlockSpec((pd, d), const), pl.BlockSpec((1, d), const),
        ],
        out_specs=pl.BlockSpec((tb, d), row),
        out_shape=jax.ShapeDtypeStruct((n, d), F32),
        scratch_shapes=[
            pltpu.SMEM((2, tb * TOP_K), I32),
            pltpu.VMEM((2, TOP_K, tb, d), F32),
            pltpu.SemaphoreType.DMA((2,)),
            pltpu.SemaphoreType.DMA((2,)),
        ],
        compiler_params=_cparams(("arbitrary",)),
        name="combine_ple_final",
    )(pos_flat, ys, topw, x1, p2, ln_ple, wpg, wpp, ln_final)


def _layer(x2, p2, bsz, seq, ln_mix, w_in, w_gk, b_gk, gla_norm, rel_bias, w_out, ln_moe,
           w_router, b_router, w1, b1, w2, b2, ln_ple, w_ple_gate, w_ple_proj, ln_out):
    n, d = x2.shape
    qk, gv, aw = GLA_HEADS * GLA_DK, GLA_HEADS * GLA_DV, ATT_HEADS * ATT_DH
    names = ("q_g", "k_g", "v_g", "gk_low", "r_g", "q_a", "k_a", "v_a", "gt_a", "gt_b")
    widths = (qk, qk, gv, GLA_RANK, gv, aw, aw, aw, d, d)
    src = dict(zip(names, np.cumsum((0,) + widths[:-1]).tolist()))
    wid = dict(zip(names, widths))
    order = [nm for nm in names if nm != "gk_low"]
    col, off = {}, 0
    for nm in order:
        col[nm] = off
        off += wid[nm]
    w_main = jnp.concatenate([w_in[:, src[nm]:src[nm] + wid[nm]] for nm in order], axis=1).astype(BF16)
    w_low = jnp.pad(w_in[:, src["gk_low"]:src["gk_low"] + GLA_RANK],
                    ((0, 0), (0, LANES - GLA_RANK))).astype(BF16)

    z, gk = _in_proj(x2, ln_mix.reshape(1, d), w_main, w_low)
    z3 = z.reshape(bsz, seq, -1)
    gk3 = gk.reshape(bsz, seq, LANES)

    wgk_pad = jnp.pad(w_gk, ((0, LANES - GLA_RANK), (0, 0)))
    ya = _gla(z3, gk3, wgk_pad, b_gk.reshape(1, qk), gla_norm.reshape(1, GLA_DV), col)
    yb = _attn(z3, _attn_bias(rel_bias), col)

    wr_pad = jnp.pad(w_router, ((0, 0), (0, LANES - N_EXPERTS)))
    br_pad = jnp.pad(b_router, (0, LANES - N_EXPERTS)).reshape(1, LANES)
    x1, xm, topi, topw, cnt = _outproj(x2, ya.reshape(n, d), yb.reshape(n, d), w_out.astype(BF16),
                                       ln_moe.reshape(1, d), wr_pad, br_pad)

    counts = cnt[0, :N_EXPERTS].astype(I32)
    ntile = (counts + EXP_TM - 1) // EXP_TM
    tile_end = jnp.cumsum(ntile)
    tile_start = tile_end - ntile
    rows_total = n * TOP_K + N_EXPERTS * EXP_TM
    tiles = jnp.arange(rows_total // EXP_TM, dtype=I32)
    onehot = (tiles[:, None] >= tile_start[None, :]) & (tiles[:, None] < tile_end[None, :])
    te = jnp.sum(jnp.where(onehot, jnp.arange(N_EXPERTS, dtype=I32)[None, :], 0), axis=1)
    tv = jnp.sum(jnp.where(onehot, counts[None, :] - (tiles[:, None] - tile_start[None, :]) * EXP_TM, 0),
                 axis=1)
    te = jnp.where(tiles < tile_end[-1], te, N_EXPERTS - 1).astype(I32)
    tv = jnp.clip(tv, 0, EXP_TM).astype(I32)
    start_rows = jnp.pad((tile_start * EXP_TM).astype(F32), (0, LANES - N_EXPERTS)).reshape(1, LANES)

    pos_t = _positions(topi, start_rows)[:TOP_K]

    def pos_blocks(tokens):
        return pos_t.reshape(TOP_K, n // tokens, tokens).transpose(1, 0, 2).reshape(-1)

    xs = _dispatch(tv, pos_blocks(DISPATCH_TOKENS), xm, rows_total)

    ys = _experts(te, tv, xs, w1,
                  _pair_order(b1[:, 0::2], 1).reshape(N_EXPERTS, 1, -1),
                  _pair_order(b1[:, 1::2], 1).reshape(N_EXPERTS, 1, -1),
                  _pair_order(w2, 1).astype(BF16), b2.reshape(N_EXPERTS, 1, d))

    return _ple(pos_blocks(PLE_TB), ys, topw, x1, p2, ln_ple.reshape(1, d), w_ple_gate.astype(BF16),
                w_ple_proj.astype(BF16), ln_out.reshape(1, d))


def kernel(x, p, ln_mix, w_in, w_gk, b_gk, gla_norm, rel_bias, w_out, ln_moe, w_router, b_router,
           w1, b1, w2, b2, ln_ple, w_ple_gate, w_ple_proj, ln_final):
    bsz, seq, d = x.shape
    depth = p.shape[0]
    assert depth == 1, "the final RMSNorm is fused into the last layer's kernel"
    x2 = x.reshape(bsz * seq, d)
    out = _layer(x2, p[0].reshape(bsz * seq, -1), bsz, seq, ln_mix[0], w_in[0], w_gk[0], b_gk[0],
                 gla_norm[0], rel_bias[0], w_out[0], ln_moe[0], w_router[0], b_router[0],
                 w1[0], b1[0], w2[0], b2[0], ln_ple[0], w_ple_gate[0], w_ple_proj[0], ln_final)
    return out.reshape(bsz, seq, d)
```

```python
import functools

import numpy as np
import jax
import jax.numpy as jnp
from jax import lax
from jax.experimental import pallas as pl
from jax.experimental.pallas import tpu as pltpu

F32 = jnp.float32
BF16 = jnp.bfloat16
I32 = jnp.int32

LANES = 128
SUBLANES = 8
CHUNK = 64
GLA_HEADS = 4
GLA_DK = 128
GLA_DV = 256
GLA_RANK = 16
GLA_TAU = 16.0
ATT_HEADS = 16
ATT_DH = 64
ATT_PAST = 8
REL_CLIP = 256
N_EXPERTS = 32
TOP_K = 4
SWIGLU_ALPHA = 1.702
SWIGLU_LIMIT = 7.0
EPS = 1e-6

VMEM_LIMIT = 48 * 1024 * 1024

IN_TM, IN_TN = 1024, 1024
GLA_TT = 128
ATT_TQ = 256
ATT_STEP_HEADS = 4
OUT_TM = 512
RANK_TB = 512
EXP_TM = 512
DISPATCH_TOKENS = 512
PLE_TB = 256
DMA_UNROLL = 8


def _cparams(sem):
    return pltpu.CompilerParams(dimension_semantics=sem, vmem_limit_bytes=VMEM_LIMIT)


def _split_bf16(a):
    hi = a.astype(BF16)
    lo = (a - hi.astype(F32)).astype(BF16)
    return hi, lo


def _dot_split(a, b):
    a_hi, a_lo = _split_bf16(a)
    b_hi, b_lo = _split_bf16(b)
    d = functools.partial(jnp.dot, preferred_element_type=F32)
    return d(a_hi, b_hi) + (d(a_hi, b_lo) + d(a_lo, b_hi))


def _rms(x, gain):
    ms = jnp.mean(x * x, axis=-1, keepdims=True)
    return x * lax.rsqrt(ms + EPS) * gain


def _store_token_tiles(ref, val):
    groups = val.shape[1] // LANES
    for c in range(groups):
        ref[pl.ds(c, val.shape[0], stride=groups), :] = val[:, c * LANES:(c + 1) * LANES]


def _load_token_tiles(ref, rows, groups):
    return jnp.concatenate([ref[pl.ds(c, rows, stride=groups), :] for c in range(groups)], axis=1)


_NT = (((1,), (1,)), ((), ()))
_TN = (((0,), (0,)), ((), ()))


def _in_proj_kernel(x_ref, g_ref, w_ref, wlow_ref, z_ref, gk_ref, xn_ref):
    @pl.when(pl.program_id(1) == 0)
    def _():
        xn = _rms(x_ref[...], g_ref[...]).astype(BF16)
        xn_ref[...] = xn
        gk_ref[...] = jnp.dot(xn, wlow_ref[...], preferred_element_type=F32)

    z_ref[...] = jnp.dot(xn_ref[...], w_ref[...], preferred_element_type=F32).astype(BF16)


def _in_proj(x2, ln, w_main, w_low):
    n, d = x2.shape
    ncol = w_main.shape[1]
    return pl.pallas_call(
        _in_proj_kernel,
        grid=(n // IN_TM, ncol // IN_TN),
        in_specs=[
            pl.BlockSpec((IN_TM, d), lambda i, j: (i, 0)),
            pl.BlockSpec((1, d), lambda i, j: (0, 0)),
            pl.BlockSpec((d, IN_TN), lambda i, j: (0, j)),
            pl.BlockSpec((d, LANES), lambda i, j: (0, 0)),
        ],
        out_specs=[
            pl.BlockSpec((IN_TM, IN_TN), lambda i, j: (i, j)),
            pl.BlockSpec((IN_TM, LANES), lambda i, j: (i, 0)),
        ],
        out_shape=[
            jax.ShapeDtypeStruct((n, ncol), BF16),
            jax.ShapeDtypeStruct((n, LANES), F32),
        ],
        scratch_shapes=[pltpu.VMEM((IN_TM, d), BF16)],
        compiler_params=_cparams(("parallel", "arbitrary")),
        name="in_proj",
    )(x2, ln, w_main, w_low)


_GLA_LEVELS = (8, 16, 32)
_GLA_BOT = 8


def _gla_consts(tt):
    t = np.arange(tt)
    same_chunk = (t[:, None] // CHUNK) == (t[None, :] // CHUNK)
    tri = (same_chunk & (t[None, :] <= t[:, None])).astype(np.float32)
    mlev = []
    for h in _GLA_LEVELS:
        blk = (t[:, None] // (2 * h)) == (t[None, :] // (2 * h))
        m = blk & ((t[:, None] % (2 * h)) >= h) & ((t[None, :] % (2 * h)) < h)
        mlev.append(m.astype(np.float32))
    mbot = []
    for s in range(_GLA_BOT):
        m = (t[None, :] == (t[:, None] // _GLA_BOT) * _GLA_BOT + s) & ((t[:, None] % _GLA_BOT) >= s)
        mbot.append(m.astype(np.float32))
    return (jnp.asarray(tri, BF16), jnp.asarray(np.stack(mlev), F32),
            jnp.asarray(np.stack(mbot), F32), jnp.ones((GLA_DK, tt), BF16))


def _gla_kernel(q_ref, k_ref, v_ref, r_ref, ga_ref, gk_ref, wgk_ref, bgk_ref, gn_ref,
                tri_ref, mlev_ref, mbot_ref, ones_ref, o_ref, st_ref):
    @pl.when(pl.program_id(1) == 0)
    def _():
        st_ref[...] = jnp.zeros_like(st_ref)

    xg = _dot_split(gk_ref[...], wgk_ref[...]) + bgk_ref[...]
    g = -(jnp.maximum(-xg, 0.0) + jnp.log(1.0 + jnp.exp(-jnp.abs(xg)))) * (1.0 / GLA_TAU)
    g_hi, g_lo = _split_bf16(g)
    tri = tri_ref[...]
    b_all = (jnp.dot(tri, g_hi, preferred_element_type=F32)
             + jnp.dot(tri, g_lo, preferred_element_type=F32))

    for hh in range(GLA_HEADS):
        kc = slice(hh * GLA_DK, (hh + 1) * GLA_DK)
        vc = slice(hh * GLA_DV, (hh + 1) * GLA_DV)
        y, st = _gla_head(q_ref[:, kc], k_ref[:, kc], v_ref[:, vc], b_all[:, kc], st_ref[hh],
                          mlev_ref, mbot_ref, ones_ref[...])
        st_ref[hh] = st
        r = r_ref[:, vc].astype(F32)
        ya = _rms(y, gn_ref[...]) * (r * jax.nn.sigmoid(r))
        o_ref[:, vc] = (jax.nn.sigmoid(ga_ref[:, vc].astype(F32)) * ya).astype(BF16)


def _gla_head(q, k, v, b, st, mlev_ref, mbot_ref, ones):
    tt = q.shape[0]
    q = q.astype(F32) * (GLA_DK ** -0.5)
    k = k.astype(F32)

    s_intra = jnp.zeros((tt, tt), F32)
    for li, h in enumerate(_GLA_LEVELS):
        b3 = b.reshape(tt // (2 * h), 2 * h, GLA_DK)
        bm = b3[:, h - 1:h, :]
        eq = jnp.exp(jnp.minimum(b3 - bm, 0.0)).reshape(tt, GLA_DK)
        ek = jnp.exp(jnp.minimum(bm - b3, 0.0)).reshape(tt, GLA_DK)
        sc = lax.dot_general((q * eq).astype(BF16), (k * ek).astype(BF16), _NT,
                             preferred_element_type=F32)
        s_intra = s_intra + sc * mlev_ref[li]
    nb = tt // _GLA_BOT
    b3 = b.reshape(nb, _GLA_BOT, GLA_DK)
    q3 = q.reshape(nb, _GLA_BOT, GLA_DK)
    k3 = k.reshape(nb, _GLA_BOT, GLA_DK)
    for s in range(_GLA_BOT):
        e = jnp.exp(jnp.minimum(b3 - b3[:, s:s + 1, :], 0.0))
        a = (q3 * e * k3[:, s:s + 1, :]).reshape(tt, GLA_DK).astype(BF16)
        s_intra = s_intra + jnp.dot(a, ones, preferred_element_type=F32) * mbot_ref[s]
    o_intra = jnp.dot(s_intra.astype(BF16), v, preferred_element_type=F32)

    outs = []
    for c in range(tt // CHUNK):
        lo = c * CHUNK
        bc = b[lo:lo + CHUNK]
        bl = b[lo + CHUNK - 1:lo + CHUNK]
        qe = (q[lo:lo + CHUNK] * jnp.exp(bc)).astype(BF16)
        outs.append(lax.dot_general(qe, st.astype(BF16), _NT, preferred_element_type=F32))
        kd = (k[lo:lo + CHUNK] * jnp.exp(bl - bc)).astype(BF16)
        upd = lax.dot_general(v[lo:lo + CHUNK], kd, _TN, preferred_element_type=F32)
        st = st * jnp.exp(bl) + upd
    return o_intra + jnp.concatenate(outs, axis=0), st


def _gla(z3, gk3, wgk_pad, bgk, gnorm, col):
    b, t, _ = z3.shape
    tt = GLA_TT
    tri, mlev, mbot, ones = _gla_consts(tt)
    qk, gv = GLA_HEADS * GLA_DK, GLA_HEADS * GLA_DV

    def zspec(width, off):
        return pl.BlockSpec((None, tt, width), lambda bi, ti, o=off // width: (bi, ti, o))

    const2 = lambda bi, ti: (0, 0)
    const3 = lambda bi, ti: (0, 0, 0)
    return pl.pallas_call(
        _gla_kernel,
        grid=(b, t // tt),
        in_specs=[
            zspec(qk, col["q_g"]), zspec(qk, col["k_g"]), zspec(gv, col["v_g"]),
            zspec(gv, col["r_g"]), zspec(gv, col["gt_a"]),
            pl.BlockSpec((None, tt, LANES), lambda bi, ti: (bi, ti, 0)),
            pl.BlockSpec((LANES, qk), const2),
            pl.BlockSpec((1, qk), const2),
            pl.BlockSpec((1, GLA_DV), const2),
            pl.BlockSpec((tt, tt), const2),
            pl.BlockSpec((len(_GLA_LEVELS), tt, tt), const3),
            pl.BlockSpec((_GLA_BOT, tt, tt), const3),
            pl.BlockSpec((GLA_DK, tt), const2),
        ],
        out_specs=pl.BlockSpec((None, tt, gv), lambda bi, ti: (bi, ti, 0)),
        out_shape=jax.ShapeDtypeStruct((b, t, gv), BF16),
        scratch_shapes=[pltpu.VMEM((GLA_HEADS, GLA_DV, GLA_DK), F32)],
        compiler_params=_cparams(("parallel", "arbitrary")),
        name="gla",
    )(z3, z3, z3, z3, z3, gk3, wgk_pad, bgk, gnorm, tri, mlev, mbot, ones)


_ATT_NKB = 3


def _attn_bias(rel_bias):
    tq = ATT_TQ
    nk = _ATT_NKB * tq
    back = nk - tq
    nheads = rel_bias.shape[0]
    span = nk + tq - 1
    dist = np.clip(np.arange(span) - (tq - 1), -REL_CLIP, REL_CLIP) + REL_CLIP
    g = rel_bias.astype(F32)[:, dist]
    x = jnp.pad(g[:, ::-1], ((0, 0), (0, 1)))
    tab = jnp.tile(x, (1, tq))[:, :tq * span].reshape(nheads, tq, span)[:, :, tq - 1:tq - 1 + nk]
    t = np.arange(tq)[:, None]
    w = np.arange(nk)[None, :]
    dc = t // CHUNK - np.floor_divide(w - back, CHUNK)
    band = (dc >= 0) & (dc <= ATT_PAST)
    valid = np.stack([band & (w // tq >= _ATT_NKB - 1 - e) for e in range(_ATT_NKB)])
    return jnp.where(jnp.asarray(valid)[:, None], tab[None], -jnp.inf)


def _attn_kernel(q_ref, k0_ref, k1_ref, k2_ref, v0_ref, v1_ref, v2_ref, gb_ref, bias_ref, o_ref):
    tq = q_ref.shape[0]
    lane = lax.broadcasted_iota(I32, (1, LANES), 1)
    scale = jnp.asarray(ATT_DH ** -0.5, BF16)
    krefs = (k0_ref, k1_ref, k2_ref)
    vrefs = (v0_ref, v1_ref, v2_ref)
    for lb in range(q_ref.shape[1] // LANES):
        cols = slice(lb * LANES, (lb + 1) * LANES)
        q = q_ref[:, cols] * scale
        ks = [r[:, cols] for r in krefs]
        vs = [r[:, cols] for r in vrefs]
        heads = []
        for hh in range(LANES // ATT_DH):
            in_head = (lane // ATT_DH) == hh
            qh = jnp.where(in_head, q, jnp.zeros_like(q))
            parts = [lax.dot_general(qh, ks[i], _NT, preferred_element_type=F32)
                     + bias_ref[lb * (LANES // ATT_DH) + hh, :, i * tq:(i + 1) * tq]
                     for i in range(_ATT_NKB)]
            m = functools.reduce(jnp.maximum, [jnp.max(p, axis=-1, keepdims=True) for p in parts])
            pv = functools.reduce(jnp.add, [
                jnp.dot(jnp.exp(p - m).astype(BF16), jnp.where(in_head, vv, jnp.ones_like(vv)),
                        preferred_element_type=F32) for p, vv in zip(parts, vs)])
            heads.append(pv / pltpu.roll(pv, ATT_DH, 1))
        o = jnp.where((lane // ATT_DH) == 0, heads[0], heads[1])
        o_ref[:, cols] = (jax.nn.sigmoid(gb_ref[:, cols].astype(F32)) * o).astype(BF16)


def _attn(z3, bias, col):
    b, t, _ = z3.shape
    tq = ATT_TQ
    width = ATT_STEP_HEADS * ATT_DH
    steps = ATT_HEADS // ATT_STEP_HEADS

    def cur(off):
        return pl.BlockSpec((None, tq, width), lambda h, bi, j, o=off // width: (bi, j, o + h))

    def past(off, back):
        return pl.BlockSpec((None, tq, width),
                            lambda h, bi, j, o=off // width: (bi, jnp.maximum(j - back, 0), o + h))

    return pl.pallas_call(
        _attn_kernel,
        grid=(steps, b, t // tq),
        in_specs=[
            cur(col["q_a"]),
            past(col["k_a"], 2), past(col["k_a"], 1), cur(col["k_a"]),
            past(col["v_a"], 2), past(col["v_a"], 1), cur(col["v_a"]),
            cur(col["gt_b"]),
            pl.BlockSpec((None, ATT_STEP_HEADS, tq, _ATT_NKB * tq),
                         lambda h, bi, j: (jnp.minimum(j, _ATT_NKB - 1), h, 0, 0)),
        ],
        out_specs=pl.BlockSpec((None, tq, width), lambda h, bi, j: (bi, j, h)),
        out_shape=jax.ShapeDtypeStruct((b, t, ATT_HEADS * ATT_DH), BF16),
        compiler_params=_cparams(("parallel", "parallel", "parallel")),
        name="band_attn",
    )(z3, z3, z3, z3, z3, z3, z3, z3, bias)


def _outproj_kernel(x_ref, ya_ref, yb_ref, wo_ref, lnm_ref, wr_ref, br_ref,
                    x1_ref, xm_ref, topi_ref, topw_ref, cnt_ref):
    h = (ya_ref[...].astype(F32) + yb_ref[...].astype(F32)).astype(BF16)
    x1 = x_ref[...] + jnp.dot(h, wo_ref[...], preferred_element_type=F32)
    x1_ref[...] = x1
    xm = _rms(x1, lnm_ref[...])
    _store_token_tiles(xm_ref, xm)

    tm = xm.shape[0]
    lane = lax.broadcasted_iota(I32, (tm, LANES), 1)
    lanef = lane.astype(F32)
    logits = _dot_split(xm, wr_ref[...]) + br_ref[...]
    l = jnp.where(lane < N_EXPERTS, logits, -jnp.inf)
    vals, idxs = [], []
    picked = jnp.zeros((tm, LANES), F32)
    for _ in range(TOP_K):
        m = jnp.max(l, axis=-1, keepdims=True)
        idx = jnp.min(jnp.where(l == m, lanef, float(LANES)), axis=-1, keepdims=True)
        vals.append(m)
        idxs.append(idx)
        hit = lanef == idx
        picked = jnp.where(hit, 1.0, picked)
        l = jnp.where(hit, -jnp.inf, l)
    es = [jnp.exp(vv - vals[0]) for vv in vals]
    tot = functools.reduce(jnp.add, es)
    topw = jnp.zeros((tm, LANES), F32)
    topi = jnp.zeros((tm, LANES), F32)
    for kk in range(TOP_K):
        topw = jnp.where(lane == kk, es[kk] / tot, topw)
        topi = jnp.where(lane == kk, idxs[kk], topi)
    topw_ref[...] = topw
    topi_ref[...] = topi.astype(I32)

    @pl.when(pl.program_id(0) == 0)
    def _():
        cnt_ref[...] = jnp.zeros_like(cnt_ref)

    cnt_ref[...] = cnt_ref[...] + jnp.sum(picked, axis=0, keepdims=True)


def _outproj(x2, ya, yb, w_out, ln_moe, wr_pad, br_pad):
    n, d = x2.shape
    tm = OUT_TM
    row = lambda i: (i, 0)
    const = lambda i: (0, 0)
    return pl.pallas_call(
        _outproj_kernel,
        grid=(n // tm,),
        in_specs=[
            pl.BlockSpec((tm, d), row), pl.BlockSpec((tm, d), row), pl.BlockSpec((tm, d), row),
            pl.BlockSpec((d, d), const), pl.BlockSpec((1, d), const),
            pl.BlockSpec((d, LANES), const), pl.BlockSpec((1, LANES), const),
        ],
        out_specs=[
            pl.BlockSpec((tm, d), row), pl.BlockSpec((tm * d // LANES, LANES), row),
            pl.BlockSpec((tm, LANES), row), pl.BlockSpec((tm, LANES), row),
            pl.BlockSpec((1, LANES), const),
        ],
        out_shape=[
            jax.ShapeDtypeStruct((n, d), F32), jax.ShapeDtypeStruct((n * d // LANES, LANES), F32),
            jax.ShapeDtypeStruct((n, LANES), I32), jax.ShapeDtypeStruct((n, LANES), F32),
            jax.ShapeDtypeStruct((1, LANES), F32),
        ],
        compiler_params=_cparams(("arbitrary",)),
        name="out_proj_router",
    )(x2, ya, yb, w_out, ln_moe, wr_pad, br_pad)


def _pos_kernel(topi_ref, start_ref, tri_ref, pos_ref, carry_ref):
    @pl.when(pl.program_id(0) == 0)
    def _():
        carry_ref[...] = start_ref[...]

    ti = topi_ref[...]
    tb = ti.shape[0]
    lane = lax.broadcasted_iota(I32, (tb, LANES), 1)
    sel = [lane == ti[:, kk:kk + 1] for kk in range(TOP_K)]
    oh = functools.reduce(jnp.add, [jnp.where(s, 1.0, 0.0) for s in sel])
    row = carry_ref[...] + jnp.dot(tri_ref[...], oh.astype(BF16), preferred_element_type=F32)
    out = jnp.zeros((tb, LANES), F32)
    for kk in range(TOP_K):
        rk = jnp.sum(jnp.where(sel[kk], row, 0.0), axis=-1, keepdims=True)
        out = jnp.where(lane == kk, rk, out)
    pos_ref[...] = jnp.transpose(out)[:pos_ref.shape[0]].astype(I32)
    carry_ref[...] = carry_ref[...] + jnp.sum(oh, axis=0, keepdims=True)


def _positions(topi, start_rows):
    n = topi.shape[0]
    tb = RANK_TB
    t = np.arange(tb)
    tri = jnp.asarray((t[None, :] < t[:, None]).astype(np.float32), BF16)
    return pl.pallas_call(
        _pos_kernel,
        grid=(n // tb,),
        in_specs=[pl.BlockSpec((tb, LANES), lambda i: (i, 0)),
                  pl.BlockSpec((1, LANES), lambda i: (0, 0)),
                  pl.BlockSpec((tb, tb), lambda i: (0, 0))],
        out_specs=pl.BlockSpec((SUBLANES, tb), lambda i: (0, i)),
        out_shape=jax.ShapeDtypeStruct((SUBLANES, n), I32),
        scratch_shapes=[pltpu.VMEM((1, LANES), F32)],
        compiler_params=_cparams(("arbitrary",)),
        name="route_positions",
    )(topi, start_rows, tri)


def _token_copy(src, src_tok, dst, dst_tok, sem):
    s0 = pl.multiple_of(src_tok * SUBLANES, SUBLANES)
    d0 = pl.multiple_of(dst_tok * SUBLANES, SUBLANES)
    return pltpu.make_async_copy(src.at[pl.ds(s0, SUBLANES)], dst.at[pl.ds(d0, SUBLANES)], sem)


def _tokens_wait(ref, ntok, sem):
    pltpu.make_async_copy(ref.at[pl.ds(0, ntok * SUBLANES)], ref.at[pl.ds(0, ntok * SUBLANES)], sem).wait()


def _dispatch_kernel(tv_ref, pos_hbm, xm_ref, xs_hbm, pos_smem, zeros_ref, sem_pos, sem_rows, sem_fill):
    i = pl.program_id(0)
    nsteps = pl.num_programs(0)
    tokens = xm_ref.shape[0] // SUBLANES
    ch = tokens * TOP_K
    slot = i % 2

    def pos_copy(step, sl):
        return pltpu.make_async_copy(pos_hbm.at[pl.ds(step * ch, ch)], pos_smem.at[sl], sem_pos.at[sl])

    @pl.when(i == 0)
    def _():
        pos_copy(0, 0).start()
        tm = zeros_ref.shape[0]
        zeros_ref[...] = jnp.zeros_like(zeros_ref)

        def fill(t):
            return pltpu.make_async_copy(zeros_ref, xs_hbm.at[pl.ds(t * tm, tm)], sem_fill)

        def start(t, carry):
            @pl.when(tv_ref[t] < EXP_TM)
            def _():
                fill(t).start()
            return carry

        def wait(t, carry):
            @pl.when(tv_ref[t] < EXP_TM)
            def _():
                fill(t).wait()
            return carry

        ntiles = xs_hbm.shape[0] // tm
        lax.fori_loop(0, ntiles, start, 0)
        lax.fori_loop(0, ntiles, wait, 0)

    @pl.when(i + 1 < nsteps)
    def _():
        pos_copy(i + 1, 1 - slot).start()

    pos_copy(i, slot).wait()

    def body(t, carry):
        for kk in range(TOP_K):
            _token_copy(xm_ref, t, xs_hbm, pos_smem[slot, kk * tokens + t], sem_rows).start(priority=kk % 2)
        return carry

    lax.fori_loop(0, tokens, body, 0, unroll=DMA_UNROLL)
    _tokens_wait(xs_hbm, ch, sem_rows)


def _dispatch(tile_valid, pos_flat, xm, rows_total):
    tokens = DISPATCH_TOKENS
    n = xm.shape[0] // SUBLANES
    return pl.pallas_call(
        _dispatch_kernel,
        grid_spec=pltpu.PrefetchScalarGridSpec(
            num_scalar_prefetch=1,
            grid=(n // tokens,),
            in_specs=[pl.BlockSpec(memory_space=pl.ANY),
                      pl.BlockSpec((tokens * SUBLANES, LANES), lambda i, tv: (i, 0))],
            out_specs=pl.BlockSpec(memory_space=pl.ANY),
            scratch_shapes=[pltpu.SMEM((2, tokens * TOP_K), I32),
                            pltpu.VMEM((EXP_TM * SUBLANES, LANES), F32),
                            pltpu.SemaphoreType.DMA((2,)), pltpu.SemaphoreType.DMA,
                            pltpu.SemaphoreType.DMA],
        ),
        out_shape=jax.ShapeDtypeStruct((rows_total * SUBLANES, LANES), F32),
        compiler_params=_cparams(("arbitrary",)),
        name="dispatch",
    )(tile_valid, pos_flat, xm)


def _pair_order(a, axis):
    shp = a.shape
    f = shp[axis]
    a = a.reshape(shp[:axis] + (f // LANES, 2, LANES // 2) + shp[axis + 1:])
    return jnp.swapaxes(a, axis + 1, axis + 2).reshape(shp)


def _expert_kernel(te_ref, tv_ref, xs_ref, w1_ref, b1g_ref, b1l_ref, w2_ref, b2_ref, ys_ref,
                   w1g_ref, w1l_ref):
    t = pl.program_id(0)
    nvalid = tv_ref[t]
    new_expert = (t == 0) | (te_ref[t] != te_ref[jnp.maximum(t - 1, 0)])

    @pl.when((nvalid > 0) & new_expert)
    def _():
        even = lax.broadcasted_iota(I32, (1, LANES), 1) % 2 == 0
        for m in range(w1g_ref.shape[1] // LANES):
            a = w1_ref[:, 2 * m * LANES:(2 * m + 1) * LANES]
            b = w1_ref[:, (2 * m + 1) * LANES:(2 * m + 2) * LANES]
            w1g_ref[:, m * LANES:(m + 1) * LANES] = jnp.where(even, a, pltpu.roll(b, 1, 1)).astype(BF16)
            w1l_ref[:, m * LANES:(m + 1) * LANES] = jnp.where(even, pltpu.roll(a, LANES - 1, 1), b).astype(BF16)

    @pl.when(nvalid > 0)
    def _():
        tm = xs_ref.shape[0] // SUBLANES
        x = _load_token_tiles(xs_ref, tm, SUBLANES).astype(BF16)
        hg = jnp.dot(x, w1g_ref[...], preferred_element_type=F32) + b1g_ref[...]
        hl = jnp.dot(x, w1l_ref[...], preferred_element_type=F32) + b1l_ref[...]
        glu = jnp.minimum(hg, SWIGLU_LIMIT)
        lin = jnp.clip(hl, -SWIGLU_LIMIT, SWIGLU_LIMIT)
        act = glu * jax.nn.sigmoid(SWIGLU_ALPHA * glu) * (lin + 1.0)
        y = jnp.dot(act.astype(BF16), w2_ref[...], preferred_element_type=F32) + b2_ref[...]
        _store_token_tiles(ys_ref, y)

    @pl.when(nvalid <= 0)
    def _():
        ys_ref[...] = jnp.zeros_like(ys_ref)


def _experts(tile_expert, tile_valid, xs, w1, b1g, b1l, w2, b2):
    f, d = w2.shape[1:]
    p = xs.shape[0] // SUBLANES
    tm = EXP_TM
    wmap = lambda t, te, tv: (te[t], 0, 0)
    return pl.pallas_call(
        _expert_kernel,
        grid_spec=pltpu.PrefetchScalarGridSpec(
            num_scalar_prefetch=2,
            grid=(p // tm,),
            in_specs=[
                pl.BlockSpec((tm * SUBLANES, LANES), lambda t, te, tv: (t, 0)),
                pl.BlockSpec((None, d, 2 * f), wmap),
                pl.BlockSpec((None, 1, f), wmap), pl.BlockSpec((None, 1, f), wmap),
                pl.BlockSpec((None, f, d), wmap), pl.BlockSpec((None, 1, d), wmap),
            ],
            out_specs=pl.BlockSpec((tm * SUBLANES, LANES), lambda t, te, tv: (t, 0)),
            scratch_shapes=[pltpu.VMEM((d, f), BF16), pltpu.VMEM((d, f), BF16)],
        ),
        out_shape=jax.ShapeDtypeStruct((p * SUBLANES, LANES), F32),
        compiler_params=_cparams(("arbitrary",)),
        name="experts",
    )(tile_expert, tile_valid, xs, w1, b1g, b1l, w2, b2)


def _ple_kernel(pos_hbm, ys_hbm, topw_ref, x1_ref, p_ref, lnp_ref, wpg_ref, wpp_ref, lnf_ref, o_ref,
                pos_smem, ybuf, sem_pos, sem_rows):
    i = pl.program_id(0)
    nsteps = pl.num_programs(0)
    tb = x1_ref.shape[0]
    ch = tb * TOP_K
    slot = i % 2

    def pos_copy(step, sl):
        return pltpu.make_async_copy(pos_hbm.at[pl.ds(step * ch, ch)], pos_smem.at[sl], sem_pos.at[sl])

    def issue_gathers(sl):
        def body(t, carry):
            for kk in range(TOP_K):
                _token_copy(ys_hbm, pos_smem[sl, kk * tb + t], ybuf.at[sl, kk], t,
                            sem_rows.at[sl]).start(priority=kk % 2)
            return carry
        lax.fori_loop(0, tb, body, 0, unroll=DMA_UNROLL)

    @pl.when(i == 0)
    def _():
        pos_copy(0, 0).start()
        pos_copy(0, 0).wait()
        issue_gathers(0)

        @pl.when(nsteps > 1)
        def _():
            pos_copy(1, 1).start()

    @pl.when(i + 1 < nsteps)
    def _():
        pos_copy(i + 1, 1 - slot).wait()
        issue_gathers(1 - slot)

    @pl.when(i + 2 < nsteps)
    def _():
        pos_copy(i + 2, slot).start()

    for kk in range(TOP_K):
        _tokens_wait(ybuf.at[slot, kk], tb, sem_rows.at[slot])

    topw = topw_ref[...]
    groups = x1_ref.shape[1] // LANES
    moe = functools.reduce(jnp.add, [
        topw[:, kk:kk + 1] * _load_token_tiles(ybuf.at[slot, kk], tb, groups) for kk in range(TOP_K)])
    x2 = x1_ref[...] + moe
    gate = jax.nn.sigmoid(jnp.dot(_rms(x2, lnp_ref[...]).astype(BF16), wpg_ref[...],
                                  preferred_element_type=F32))
    proj = jnp.dot(p_ref[...].astype(BF16), wpp_ref[...], preferred_element_type=F32)
    o_ref[...] = _rms(x2 + gate * proj, lnf_ref[...])


def _ple(pos_flat, ys, topw, x1, p2, ln_ple, wpg, wpp, ln_final):
    n, d = x1.shape
    pd = p2.shape[1]
    tb = PLE_TB
    row = lambda i: (i, 0)
    const = lambda i: (0, 0)
    return pl.pallas_call(
        _ple_kernel,
        grid=(n // tb,),
        in_specs=[
            pl.BlockSpec(memory_space=pl.ANY), pl.BlockSpec(memory_space=pl.ANY),
            pl.BlockSpec((tb, LANES), row), pl.BlockSpec((tb, d), row), pl.BlockSpec((tb, pd), row),
            pl.BlockSpec((1, d), const), pl.BlockSpec((d, d), const),
            pl.BlockSpec((pd, d), const), pl.BlockSpec((1, d), const),
        ],
        out_specs=pl.BlockSpec((tb, d), row),
        out_shape=jax.ShapeDtypeStruct((n, d), F32),
        scratch_shapes=[
            pltpu.SMEM((2, tb * TOP_K), I32),
            pltpu.VMEM((2, TOP_K, tb * d // LANES, LANES), F32),
            pltpu.SemaphoreType.DMA((2,)),
            pltpu.SemaphoreType.DMA((2,)),
        ],
        compiler_params=_cparams(("arbitrary",)),
        name="combine_ple_final",
    )(pos_flat, ys, topw, x1, p2, ln_ple, wpg, wpp, ln_final)


def _layer(x2, p2, bsz, seq, ln_mix, w_in, w_gk, b_gk, gla_norm, rel_bias, w_out, ln_moe,
           w_router, b_router, w1, b1, w2, b2, ln_ple, w_ple_gate, w_ple_proj, ln_out):
    n, d = x2.shape
    qk, gv, aw = GLA_HEADS * GLA_DK, GLA_HEADS * GLA_DV, ATT_HEADS * ATT_DH
    names = ("q_g", "k_g", "v_g", "gk_low", "r_g", "q_a", "k_a", "v_a", "gt_a", "gt_b")
    widths = (qk, qk, gv, GLA_RANK, gv, aw, aw, aw, d, d)
    src = dict(zip(names, np.cumsum((0,) + widths[:-1]).tolist()))
    wid = dict(zip(names, widths))
    order = [nm for nm in names if nm != "gk_low"]
    col, off = {}, 0
    for nm in order:
        col[nm] = off
        off += wid[nm]
    w_main = jnp.concatenate([w_in[:, src[nm]:src[nm] + wid[nm]] for nm in order], axis=1).astype(BF16)
    w_low = jnp.pad(w_in[:, src["gk_low"]:src["gk_low"] + GLA_RANK],
                    ((0, 0), (0, LANES - GLA_RANK))).astype(BF16)

    z, gk = _in_proj(x2, ln_mix.reshape(1, d), w_main, w_low)
    z3 = z.reshape(bsz, seq, -1)
    gk3 = gk.reshape(bsz, seq, LANES)

    wgk_pad = jnp.pad(w_gk, ((0, LANES - GLA_RANK), (0, 0)))
    ya = _gla(z3, gk3, wgk_pad, b_gk.reshape(1, qk), gla_norm.reshape(1, GLA_DV), col)
    yb = _attn(z3, _attn_bias(rel_bias), col)

    wr_pad = jnp.pad(w_router, ((0, 0), (0, LANES - N_EXPERTS)))
    br_pad = jnp.pad(b_router, (0, LANES - N_EXPERTS)).reshape(1, LANES)
    x1, xm, topi, topw, cnt = _outproj(x2, ya.reshape(n, d), yb.reshape(n, d), w_out.astype(BF16),
                                       ln_moe.reshape(1, d), wr_pad, br_pad)

    counts = cnt[0, :N_EXPERTS].astype(I32)
    ntile = (counts + EXP_TM - 1) // EXP_TM
    tile_end = jnp.cumsum(ntile)
    tile_start = tile_end - ntile
    rows_total = n * TOP_K + N_EXPERTS * EXP_TM
    tiles = jnp.arange(rows_total // EXP_TM, dtype=I32)
    onehot = (tiles[:, None] >= tile_start[None, :]) & (tiles[:, None] < tile_end[None, :])
    te = jnp.sum(jnp.where(onehot, jnp.arange(N_EXPERTS, dtype=I32)[None, :], 0), axis=1)
    tv = jnp.sum(jnp.where(onehot, counts[None, :] - (tiles[:, None] - tile_start[None, :]) * EXP_TM, 0),
                 axis=1)
    te = jnp.where(tiles < tile_end[-1], te, N_EXPERTS - 1).astype(I32)
    tv = jnp.clip(tv, 0, EXP_TM).astype(I32)
    start_rows = jnp.pad((tile_start * EXP_TM).astype(F32), (0, LANES - N_EXPERTS)).reshape(1, LANES)

    pos_t = _positions(topi, start_rows)[:TOP_K]

    def pos_blocks(tokens):
        return pos_t.reshape(TOP_K, n // tokens, tokens).transpose(1, 0, 2).reshape(-1)

    xs = _dispatch(tv, pos_blocks(DISPATCH_TOKENS), xm, rows_total)

    ys = _experts(te, tv, xs, w1,
                  _pair_order(b1[:, 0::2], 1).reshape(N_EXPERTS, 1, -1),
                  _pair_order(b1[:, 1::2], 1).reshape(N_EXPERTS, 1, -1),
                  _pair_order(w2, 1).astype(BF16), b2.reshape(N_EXPERTS, 1, d))

    return _ple(pos_blocks(PLE_TB), ys, topw, x1, p2, ln_ple.reshape(1, d), w_ple_gate.astype(BF16),
                w_ple_proj.astype(BF16), ln_out.reshape(1, d))


def kernel(x, p, ln_mix, w_in, w_gk, b_gk, gla_norm, rel_bias, w_out, ln_moe, w_router, b_router,
           w1, b1, w2, b2, ln_ple, w_ple_gate, w_ple_proj, ln_final):
    bsz, seq, d = x.shape
    depth = p.shape[0]
    assert depth == 1, "the final RMSNorm is fused into the last layer's kernel"
    x2 = x.reshape(bsz * seq, d)
    out = _layer(x2, p[0].reshape(bsz * seq, -1), bsz, seq, ln_mix[0], w_in[0], w_gk[0], b_gk[0],
                 gla_norm[0], rel_bias[0], w_out[0], ln_moe[0], w_router[0], b_router[0],
                 w1[0], b1[0], w2[0], b2[0], ln_ple[0], w_ple_gate[0], w_ple_proj[0], ln_final)
    return out.reshape(bsz, seq, d)
```

```python
import functools

import numpy as np
import jax
import jax.numpy as jnp
from jax import lax
from jax.experimental import pallas as pl
from jax.experimental.pallas import tpu as pltpu

F32 = jnp.float32
BF16 = jnp.bfloat16
I32 = jnp.int32

LANES = 128
SUBLANES = 8
CHUNK = 64
GLA_HEADS = 4
GLA_DK = 128
GLA_DV = 256
GLA_RANK = 16
GLA_TAU = 16.0
ATT_HEADS = 16
ATT_DH = 64
ATT_PAST = 8
REL_CLIP = 256
N_EXPERTS = 32
TOP_K = 4
SWIGLU_ALPHA = 1.702
SWIGLU_LIMIT = 7.0
EPS = 1e-6

VMEM_LIMIT = 48 * 1024 * 1024

IN_TM, IN_TN = 2048, 1024
GLA_TT = 128
ATT_TQ = 256
ATT_STEP_HEADS = 16
OUT_TM = 512
RANK_TB = 512
EXP_TM = 512
DISPATCH_TOKENS = 512
PLE_TB = 256
DMA_UNROLL = 8


def _cparams(sem):
    return pltpu.CompilerParams(dimension_semantics=sem, vmem_limit_bytes=VMEM_LIMIT)


def _split_bf16(a):
    hi = a.astype(BF16)
    lo = (a - hi.astype(F32)).astype(BF16)
    return hi, lo


def _dot_split(a, b):
    a_hi, a_lo = _split_bf16(a)
    b_hi, b_lo = _split_bf16(b)
    d = functools.partial(jnp.dot, preferred_element_type=F32)
    return d(a_hi, b_hi) + (d(a_hi, b_lo) + d(a_lo, b_hi))


def _rms(x, gain):
    ms = jnp.mean(x * x, axis=-1, keepdims=True)
    return x * lax.rsqrt(ms + EPS) * gain


def _store_token_tiles(ref, val):
    groups = val.shape[1] // LANES
    for c in range(groups):
        ref[pl.ds(c, val.shape[0], stride=groups), :] = val[:, c * LANES:(c + 1) * LANES]


def _load_token_tiles(ref, rows, groups):
    return jnp.concatenate([ref[pl.ds(c, rows, stride=groups), :] for c in range(groups)], axis=1)


_NT = (((1,), (1,)), ((), ()))
_TN = (((0,), (0,)), ((), ()))


def _in_proj_kernel(x_ref, g_ref, w_ref, wlow_ref, z_ref, gk_ref, xn_ref):
    @pl.when(pl.program_id(1) == 0)
    def _():
        xn = _rms(x_ref[...], g_ref[...]).astype(BF16)
        xn_ref[...] = xn
        gk_ref[...] = jnp.dot(xn, wlow_ref[...], preferred_element_type=F32)

    z_ref[...] = jnp.dot(xn_ref[...], w_ref[...], preferred_element_type=F32).astype(BF16)


def _in_proj(x2, ln, w_main, w_low):
    n, d = x2.shape
    ncol = w_main.shape[1]
    return pl.pallas_call(
        _in_proj_kernel,
        grid=(n // IN_TM, ncol // IN_TN),
        in_specs=[
            pl.BlockSpec((IN_TM, d), lambda i, j: (i, 0)),
            pl.BlockSpec((1, d), lambda i, j: (0, 0)),
            pl.BlockSpec((d, IN_TN), lambda i, j: (0, j)),
            pl.BlockSpec((d, LANES), lambda i, j: (0, 0)),
        ],
        out_specs=[
            pl.BlockSpec((IN_TM, IN_TN), lambda i, j: (i, j)),
            pl.BlockSpec((IN_TM, LANES), lambda i, j: (i, 0)),
        ],
        out_shape=[
            jax.ShapeDtypeStruct((n, ncol), BF16),
            jax.ShapeDtypeStruct((n, LANES), F32),
        ],
        scratch_shapes=[pltpu.VMEM((IN_TM, d), BF16)],
        compiler_params=_cparams(("parallel", "arbitrary")),
        name="in_proj",
    )(x2, ln, w_main, w_low)


_GLA_LEVELS = (8, 16, 32)
_GLA_BOT = 8


def _gla_consts(tt):
    t = np.arange(tt)
    same_chunk = (t[:, None] // CHUNK) == (t[None, :] // CHUNK)
    tri = (same_chunk & (t[None, :] <= t[:, None])).astype(np.float32)
    mlev = []
    for h in _GLA_LEVELS:
        blk = (t[:, None] // (2 * h)) == (t[None, :] // (2 * h))
        m = blk & ((t[:, None] % (2 * h)) >= h) & ((t[None, :] % (2 * h)) < h)
        mlev.append(m.astype(np.float32))
    mbot = []
    for s in range(_GLA_BOT):
        m = (t[None, :] == (t[:, None] // _GLA_BOT) * _GLA_BOT + s) & ((t[:, None] % _GLA_BOT) >= s)
        mbot.append(m.astype(np.float32))
    return (jnp.asarray(tri, BF16), jnp.asarray(np.stack(mlev), F32),
            jnp.asarray(np.stack(mbot), F32), jnp.ones((GLA_DK, tt), BF16))


def _gla_kernel(q_ref, k_ref, v_ref, r_ref, ga_ref, gk_ref, wgk_ref, bgk_ref, gn_ref,
                tri_ref, mlev_ref, mbot_ref, ones_ref, o_ref, st_ref):
    @pl.when(pl.program_id(1) == 0)
    def _():
        st_ref[...] = jnp.zeros_like(st_ref)

    xg = _dot_split(gk_ref[...], wgk_ref[...]) + bgk_ref[...]
    g = -(jnp.maximum(-xg, 0.0) + jnp.log(1.0 + jnp.exp(-jnp.abs(xg)))) * (1.0 / GLA_TAU)
    g_hi, g_lo = _split_bf16(g)
    tri = tri_ref[...]
    b_all = (jnp.dot(tri, g_hi, preferred_element_type=F32)
             + jnp.dot(tri, g_lo, preferred_element_type=F32))

    for hh in range(GLA_HEADS):
        kc = slice(hh * GLA_DK, (hh + 1) * GLA_DK)
        vc = slice(hh * GLA_DV, (hh + 1) * GLA_DV)
        y, st = _gla_head(q_ref[:, kc], k_ref[:, kc], v_ref[:, vc], b_all[:, kc], st_ref[hh],
                          mlev_ref, mbot_ref, ones_ref[...])
        st_ref[hh] = st
        r = r_ref[:, vc].astype(F32)
        ya = _rms(y, gn_ref[...]) * (r * jax.nn.sigmoid(r))
        o_ref[:, vc] = (jax.nn.sigmoid(ga_ref[:, vc].astype(F32)) * ya).astype(BF16)


def _gla_head(q, k, v, b, st, mlev_ref, mbot_ref, ones):
    tt = q.shape[0]
    q = q.astype(F32) * (GLA_DK ** -0.5)
    k = k.astype(F32)

    s_intra = jnp.zeros((tt, tt), F32)
    for li, h in enumerate(_GLA_LEVELS):
        b3 = b.reshape(tt // (2 * h), 2 * h, GLA_DK)
        bm = b3[:, h - 1:h, :]
        eq = jnp.exp(jnp.minimum(b3 - bm, 0.0)).reshape(tt, GLA_DK)
        ek = jnp.exp(jnp.minimum(bm - b3, 0.0)).reshape(tt, GLA_DK)
        sc = lax.dot_general((q * eq).astype(BF16), (k * ek).astype(BF16), _NT,
                             preferred_element_type=F32)
        s_intra = s_intra + sc * mlev_ref[li]
    nb = tt // _GLA_BOT
    b3 = b.reshape(nb, _GLA_BOT, GLA_DK)
    q3 = q.reshape(nb, _GLA_BOT, GLA_DK)
    k3 = k.reshape(nb, _GLA_BOT, GLA_DK)
    for s in range(_GLA_BOT):
        e = jnp.exp(jnp.minimum(b3 - b3[:, s:s + 1, :], 0.0))
        a = (q3 * e * k3[:, s:s + 1, :]).reshape(tt, GLA_DK).astype(BF16)
        s_intra = s_intra + jnp.dot(a, ones, preferred_element_type=F32) * mbot_ref[s]
    o_intra = jnp.dot(s_intra.astype(BF16), v, preferred_element_type=F32)

    outs = []
    for c in range(tt // CHUNK):
        lo = c * CHUNK
        bc = b[lo:lo + CHUNK]
        bl = b[lo + CHUNK - 1:lo + CHUNK]
        qe = (q[lo:lo + CHUNK] * jnp.exp(bc)).astype(BF16)
        outs.append(lax.dot_general(qe, st.astype(BF16), _NT, preferred_element_type=F32))
        kd = (k[lo:lo + CHUNK] * jnp.exp(bl - bc)).astype(BF16)
        upd = lax.dot_general(v[lo:lo + CHUNK], kd, _TN, preferred_element_type=F32)
        st = st * jnp.exp(bl) + upd
    return o_intra + jnp.concatenate(outs, axis=0), st


def _gla(z3, gk3, wgk_pad, bgk, gnorm, col):
    b, t, _ = z3.shape
    tt = GLA_TT
    tri, mlev, mbot, ones = _gla_consts(tt)
    qk, gv = GLA_HEADS * GLA_DK, GLA_HEADS * GLA_DV

    def zspec(width, off):
        return pl.BlockSpec((None, tt, width), lambda bi, ti, o=off // width: (bi, ti, o))

    const2 = lambda bi, ti: (0, 0)
    const3 = lambda bi, ti: (0, 0, 0)
    return pl.pallas_call(
        _gla_kernel,
        grid=(b, t // tt),
        in_specs=[
            zspec(qk, col["q_g"]), zspec(qk, col["k_g"]), zspec(gv, col["v_g"]),
            zspec(gv, col["r_g"]), zspec(gv, col["gt_a"]),
            pl.BlockSpec((None, tt, LANES), lambda bi, ti: (bi, ti, 0)),
            pl.BlockSpec((LANES, qk), const2),
            pl.BlockSpec((1, qk), const2),
            pl.BlockSpec((1, GLA_DV), const2),
            pl.BlockSpec((tt, tt), const2),
            pl.BlockSpec((len(_GLA_LEVELS), tt, tt), const3),
            pl.BlockSpec((_GLA_BOT, tt, tt), const3),
            pl.BlockSpec((GLA_DK, tt), const2),
        ],
        out_specs=pl.BlockSpec((None, tt, gv), lambda bi, ti: (bi, ti, 0)),
        out_shape=jax.ShapeDtypeStruct((b, t, gv), BF16),
        scratch_shapes=[pltpu.VMEM((GLA_HEADS, GLA_DV, GLA_DK), F32)],
        compiler_params=_cparams(("parallel", "arbitrary")),
        name="gla",
    )(z3, z3, z3, z3, z3, gk3, wgk_pad, bgk, gnorm, tri, mlev, mbot, ones)


_ATT_NKB = 3


def _attn_bias(rel_bias):
    tq = ATT_TQ
    nk = _ATT_NKB * tq
    back = nk - tq
    nheads = rel_bias.shape[0]
    span = nk + tq - 1
    dist = np.clip(np.arange(span) - (tq - 1), -REL_CLIP, REL_CLIP) + REL_CLIP
    g = rel_bias.astype(F32)[:, dist]
    x = jnp.pad(g[:, ::-1], ((0, 0), (0, 1)))
    tab = jnp.tile(x, (1, tq))[:, :tq * span].reshape(nheads, tq, span)[:, :, tq - 1:tq - 1 + nk]
    t = np.arange(tq)[:, None]
    w = np.arange(nk)[None, :]
    dc = t // CHUNK - np.floor_divide(w - back, CHUNK)
    band = (dc >= 0) & (dc <= ATT_PAST)
    valid = np.stack([band & (w // tq >= _ATT_NKB - 1 - e) for e in range(_ATT_NKB)])
    return jnp.where(jnp.asarray(valid)[:, None], tab[None], -jnp.inf)


def _attn_kernel(q_ref, k0_ref, k1_ref, k2_ref, v0_ref, v1_ref, v2_ref, gb_ref, bias_ref, o_ref):
    tq = q_ref.shape[0]
    first = lax.broadcasted_iota(I32, (1, LANES), 1) < ATT_DH
    scale = jnp.asarray(ATT_DH ** -0.5, BF16)
    krefs = (k0_ref, k1_ref, k2_ref)
    vrefs = (v0_ref, v1_ref, v2_ref)
    for lb in range(q_ref.shape[1] // LANES):
        cols = slice(lb * LANES, (lb + 1) * LANES)
        q = q_ref[:, cols] * scale
        zero = jnp.zeros_like(q)
        qs = jnp.concatenate([jnp.where(first, q, zero), jnp.where(first, zero, q)], axis=0)
        s = jnp.concatenate(
            [lax.dot_general(qs, r[:, cols], _NT, preferred_element_type=F32) for r in krefs], axis=1)
        s = s + jnp.concatenate([bias_ref[2 * lb], bias_ref[2 * lb + 1]], axis=0)
        p = jnp.exp(s - jnp.max(s, axis=-1, keepdims=True))
        l = jnp.sum(p, axis=-1, keepdims=True)
        pb = p.astype(BF16)
        pv = functools.reduce(jnp.add, [
            jnp.dot(pb[:, i * tq:(i + 1) * tq], vrefs[i][:, cols], preferred_element_type=F32)
            for i in range(_ATT_NKB)]) / l
        o = jnp.where(first, pv[:tq], pv[tq:])
        o_ref[:, cols] = (jax.nn.sigmoid(gb_ref[:, cols].astype(F32)) * o).astype(BF16)


def _attn(z3, bias, col):
    b, t, _ = z3.shape
    tq = ATT_TQ
    width = ATT_STEP_HEADS * ATT_DH
    steps = ATT_HEADS // ATT_STEP_HEADS

    def cur(off):
        return pl.BlockSpec((None, tq, width), lambda h, j, bi, o=off // width: (bi, j, o + h))

    def past(off, back):
        return pl.BlockSpec((None, tq, width),
                            lambda h, j, bi, o=off // width: (bi, jnp.maximum(j - back, 0), o + h))

    return pl.pallas_call(
        _attn_kernel,
        grid=(steps, t // tq, b),
        in_specs=[
            cur(col["q_a"]),
            past(col["k_a"], 2), past(col["k_a"], 1), cur(col["k_a"]),
            past(col["v_a"], 2), past(col["v_a"], 1), cur(col["v_a"]),
            cur(col["gt_b"]),
            pl.BlockSpec((None, ATT_STEP_HEADS, tq, _ATT_NKB * tq),
                         lambda h, j, bi: (jnp.minimum(j, _ATT_NKB - 1), h, 0, 0)),
        ],
        out_specs=pl.BlockSpec((None, tq, width), lambda h, j, bi: (bi, j, h)),
        out_shape=jax.ShapeDtypeStruct((b, t, ATT_HEADS * ATT_DH), BF16),
        compiler_params=_cparams(("parallel", "parallel", "parallel")),
        name="band_attn",
    )(z3, z3, z3, z3, z3, z3, z3, z3, bias)


def _outproj_kernel(x_ref, ya_ref, yb_ref, wo_ref, lnm_ref, wr_ref, br_ref,
                    x1_ref, xm_ref, topi_ref, topw_ref, cnt_ref):
    h = (ya_ref[...].astype(F32) + yb_ref[...].astype(F32)).astype(BF16)
    x1 = x_ref[...] + jnp.dot(h, wo_ref[...], preferred_element_type=F32)
    x1_ref[...] = x1
    xm = _rms(x1, lnm_ref[...])
    _store_token_tiles(xm_ref, xm)

    tm = xm.shape[0]
    lane = lax.broadcasted_iota(I32, (tm, LANES), 1)
    lanef = lane.astype(F32)
    logits = _dot_split(xm, wr_ref[...]) + br_ref[...]
    l = jnp.where(lane < N_EXPERTS, logits, -jnp.inf)
    vals, idxs = [], []
    picked = jnp.zeros((tm, LANES), F32)
    for _ in range(TOP_K):
        m = jnp.max(l, axis=-1, keepdims=True)
        idx = jnp.min(jnp.where(l == m, lanef, float(LANES)), axis=-1, keepdims=True)
        vals.append(m)
        idxs.append(idx)
        hit = lanef == idx
        picked = jnp.where(hit, 1.0, picked)
        l = jnp.where(hit, -jnp.inf, l)
    es = [jnp.exp(vv - vals[0]) for vv in vals]
    tot = functools.reduce(jnp.add, es)
    topw = jnp.zeros((tm, LANES), F32)
    topi = jnp.zeros((tm, LANES), F32)
    for kk in range(TOP_K):
        topw = jnp.where(lane == kk, es[kk] / tot, topw)
        topi = jnp.where(lane == kk, idxs[kk], topi)
    topw_ref[...] = topw
    topi_ref[...] = topi.astype(I32)

    @pl.when(pl.program_id(0) == 0)
    def _():
        cnt_ref[...] = jnp.zeros_like(cnt_ref)

    cnt_ref[...] = cnt_ref[...] + jnp.sum(picked, axis=0, keepdims=True)


def _outproj(x2, ya, yb, w_out, ln_moe, wr_pad, br_pad):
    n, d = x2.shape
    tm = OUT_TM
    row = lambda i: (i, 0)
    const = lambda i: (0, 0)
    return pl.pallas_call(
        _outproj_kernel,
        grid=(n // tm,),
        in_specs=[
            pl.BlockSpec((tm, d), row), pl.BlockSpec((tm, d), row), pl.BlockSpec((tm, d), row),
            pl.BlockSpec((d, d), const), pl.BlockSpec((1, d), const),
            pl.BlockSpec((d, LANES), const), pl.BlockSpec((1, LANES), const),
        ],
        out_specs=[
            pl.BlockSpec((tm, d), row), pl.BlockSpec((tm * d // LANES, LANES), row),
            pl.BlockSpec((tm, LANES), row), pl.BlockSpec((tm, LANES), row),
            pl.BlockSpec((1, LANES), const),
        ],
        out_shape=[
            jax.ShapeDtypeStruct((n, d), F32), jax.ShapeDtypeStruct((n * d // LANES, LANES), F32),
            jax.ShapeDtypeStruct((n, LANES), I32), jax.ShapeDtypeStruct((n, LANES), F32),
            jax.ShapeDtypeStruct((1, LANES), F32),
        ],
        compiler_params=_cparams(("arbitrary",)),
        name="out_proj_router",
    )(x2, ya, yb, w_out, ln_moe, wr_pad, br_pad)


def _pos_kernel(topi_ref, start_ref, tri_ref, pos_ref, carry_ref):
    @pl.when(pl.program_id(0) == 0)
    def _():
        carry_ref[...] = start_ref[...]

    ti = topi_ref[...]
    tb = ti.shape[0]
    lane = lax.broadcasted_iota(I32, (tb, LANES), 1)
    sel = [lane == ti[:, kk:kk + 1] for kk in range(TOP_K)]
    oh = functools.reduce(jnp.add, [jnp.where(s, 1.0, 0.0) for s in sel])
    row = carry_ref[...] + jnp.dot(tri_ref[...], oh.astype(BF16), preferred_element_type=F32)
    out = jnp.zeros((tb, LANES), F32)
    for kk in range(TOP_K):
        rk = jnp.sum(jnp.where(sel[kk], row, 0.0), axis=-1, keepdims=True)
        out = jnp.where(lane == kk, rk, out)
    pos_ref[...] = jnp.transpose(out)[:pos_ref.shape[0]].astype(I32)
    carry_ref[...] = carry_ref[...] + jnp.sum(oh, axis=0, keepdims=True)


def _positions(topi, start_rows):
    n = topi.shape[0]
    tb = RANK_TB
    t = np.arange(tb)
    tri = jnp.asarray((t[None, :] < t[:, None]).astype(np.float32), BF16)
    return pl.pallas_call(
        _pos_kernel,
        grid=(n // tb,),
        in_specs=[pl.BlockSpec((tb, LANES), lambda i: (i, 0)),
                  pl.BlockSpec((1, LANES), lambda i: (0, 0)),
                  pl.BlockSpec((tb, tb), lambda i: (0, 0))],
        out_specs=pl.BlockSpec((SUBLANES, tb), lambda i: (0, i)),
        out_shape=jax.ShapeDtypeStruct((SUBLANES, n), I32),
        scratch_shapes=[pltpu.VMEM((1, LANES), F32)],
        compiler_params=_cparams(("arbitrary",)),
        name="route_positions",
    )(topi, start_rows, tri)


def _token_copy(src, src_tok, dst, dst_tok, sem):
    s0 = pl.multiple_of(src_tok * SUBLANES, SUBLANES)
    d0 = pl.multiple_of(dst_tok * SUBLANES, SUBLANES)
    return pltpu.make_async_copy(src.at[pl.ds(s0, SUBLANES)], dst.at[pl.ds(d0, SUBLANES)], sem)


def _tokens_wait(ref, ntok, sem):
    pltpu.make_async_copy(ref.at[pl.ds(0, ntok * SUBLANES)], ref.at[pl.ds(0, ntok * SUBLANES)], sem).wait()


def _dispatch_kernel(tv_ref, pos_hbm, xm_ref, xs_hbm, pos_smem, zeros_ref, sem_pos, sem_rows, sem_fill):
    i = pl.program_id(0)
    nsteps = pl.num_programs(0)
    tokens = xm_ref.shape[0] // SUBLANES
    ch = tokens * TOP_K
    slot = i % 2

    def pos_copy(step, sl):
        return pltpu.make_async_copy(pos_hbm.at[pl.ds(step * ch, ch)], pos_smem.at[sl], sem_pos.at[sl])

    @pl.when(i == 0)
    def _():
        pos_copy(0, 0).start()
        tm = zeros_ref.shape[0]
        zeros_ref[...] = jnp.zeros_like(zeros_ref)

        def fill(t):
            return pltpu.make_async_copy(zeros_ref, xs_hbm.at[pl.ds(t * tm, tm)], sem_fill)

        def start(t, carry):
            @pl.when(tv_ref[t] < EXP_TM)
            def _():
                fill(t).start()
            return carry

        def wait(t, carry):
            @pl.when(tv_ref[t] < EXP_TM)
            def _():
                fill(t).wait()
            return carry

        ntiles = xs_hbm.shape[0] // tm
        lax.fori_loop(0, ntiles, start, 0)
        lax.fori_loop(0, ntiles, wait, 0)

    @pl.when(i + 1 < nsteps)
    def _():
        pos_copy(i + 1, 1 - slot).start()

    pos_copy(i, slot).wait()

    def body(t, carry):
        for kk in range(TOP_K):
            _token_copy(xm_ref, t, xs_hbm, pos_smem[slot, kk * tokens + t], sem_rows).start(priority=kk % 2)
        return carry

    lax.fori_loop(0, tokens, body, 0, unroll=DMA_UNROLL)
    _tokens_wait(xs_hbm, ch, sem_rows)


def _dispatch(tile_valid, pos_flat, xm, rows_total):
    tokens = DISPATCH_TOKENS
    n = xm.shape[0] // SUBLANES
    return pl.pallas_call(
        _dispatch_kernel,
        grid_spec=pltpu.PrefetchScalarGridSpec(
            num_scalar_prefetch=1,
            grid=(n // tokens,),
            in_specs=[pl.BlockSpec(memory_space=pl.ANY),
                      pl.BlockSpec((tokens * SUBLANES, LANES), lambda i, tv: (i, 0))],
            out_specs=pl.BlockSpec(memory_space=pl.ANY),
            scratch_shapes=[pltpu.SMEM((2, tokens * TOP_K), I32),
                            pltpu.VMEM((EXP_TM * SUBLANES, LANES), F32),
                            pltpu.SemaphoreType.DMA((2,)), pltpu.SemaphoreType.DMA,
                            pltpu.SemaphoreType.DMA],
        ),
        out_shape=jax.ShapeDtypeStruct((rows_total * SUBLANES, LANES), F32),
        compiler_params=_cparams(("arbitrary",)),
        name="dispatch",
    )(tile_valid, pos_flat, xm)


def _pair_order(a, axis):
    shp = a.shape
    f = shp[axis]
    a = a.reshape(shp[:axis] + (f // LANES, 2, LANES // 2) + shp[axis + 1:])
    return jnp.swapaxes(a, axis + 1, axis + 2).reshape(shp)


def _expert_kernel(te_ref, tv_ref, xs_ref, w1_ref, b1g_ref, b1l_ref, w2_ref, b2_ref, ys_ref,
                   w1g_ref, w1l_ref):
    t = pl.program_id(0)
    nvalid = tv_ref[t]
    new_expert = (t == 0) | (te_ref[t] != te_ref[jnp.maximum(t - 1, 0)])

    @pl.when((nvalid > 0) & new_expert)
    def _():
        even = lax.broadcasted_iota(I32, (1, LANES), 1) % 2 == 0
        for m in range(w1g_ref.shape[1] // LANES):
            a = w1_ref[:, 2 * m * LANES:(2 * m + 1) * LANES]
            b = w1_ref[:, (2 * m + 1) * LANES:(2 * m + 2) * LANES]
            w1g_ref[:, m * LANES:(m + 1) * LANES] = jnp.where(even, a, pltpu.roll(b, 1, 1)).astype(BF16)
            w1l_ref[:, m * LANES:(m + 1) * LANES] = jnp.where(even, pltpu.roll(a, LANES - 1, 1), b).astype(BF16)

    @pl.when(nvalid > 0)
    def _():
        tm = xs_ref.shape[0] // SUBLANES
        x = _load_token_tiles(xs_ref, tm, SUBLANES).astype(BF16)
        hg = jnp.dot(x, w1g_ref[...], preferred_element_type=F32) + b1g_ref[...]
        hl = jnp.dot(x, w1l_ref[...], preferred_element_type=F32) + b1l_ref[...]
        glu = jnp.minimum(hg, SWIGLU_LIMIT)
        lin = jnp.clip(hl, -SWIGLU_LIMIT, SWIGLU_LIMIT)
        act = glu * jax.nn.sigmoid(SWIGLU_ALPHA * glu) * (lin + 1.0)
        y = jnp.dot(act.astype(BF16), w2_ref[...], preferred_element_type=F32) + b2_ref[...]
        _store_token_tiles(ys_ref, y)

    @pl.when(nvalid <= 0)
    def _():
        ys_ref[...] = jnp.zeros_like(ys_ref)


def _experts(tile_expert, tile_valid, xs, w1, b1g, b1l, w2, b2):
    f, d = w2.shape[1:]
    p = xs.shape[0] // SUBLANES
    tm = EXP_TM
    wmap = lambda t, te, tv: (te[t], 0, 0)
    return pl.pallas_call(
        _expert_kernel,
        grid_spec=pltpu.PrefetchScalarGridSpec(
            num_scalar_prefetch=2,
            grid=(p // tm,),
            in_specs=[
                pl.BlockSpec((tm * SUBLANES, LANES), lambda t, te, tv: (t, 0)),
                pl.BlockSpec((None, d, 2 * f), wmap),
                pl.BlockSpec((None, 1, f), wmap), pl.BlockSpec((None, 1, f), wmap),
                pl.BlockSpec((None, f, d), wmap), pl.BlockSpec((None, 1, d), wmap),
            ],
            out_specs=pl.BlockSpec((tm * SUBLANES, LANES), lambda t, te, tv: (t, 0)),
            scratch_shapes=[pltpu.VMEM((d, f), BF16), pltpu.VMEM((d, f), BF16)],
        ),
        out_shape=jax.ShapeDtypeStruct((p * SUBLANES, LANES), F32),
        compiler_params=_cparams(("arbitrary",)),
        name="experts",
    )(tile_expert, tile_valid, xs, w1, b1g, b1l, w2, b2)


def _ple_kernel(pos_hbm, ys_hbm, topw_ref, x1_ref, p_ref, lnp_ref, wpg_ref, wpp_ref, lnf_ref, o_ref,
                pos_smem, ybuf, sem_pos, sem_rows):
    i = pl.program_id(0)
    nsteps = pl.num_programs(0)
    tb = x1_ref.shape[0]
    ch = tb * TOP_K
    slot = i % 2

    def pos_copy(step, sl):
        return pltpu.make_async_copy(pos_hbm.at[pl.ds(step * ch, ch)], pos_smem.at[sl], sem_pos.at[sl])

    def issue_gathers(sl):
        def body(t, carry):
            for kk in range(TOP_K):
                _token_copy(ys_hbm, pos_smem[sl, kk * tb + t], ybuf.at[sl, kk], t,
                            sem_rows.at[sl]).start(priority=kk % 2)
            return carry
        lax.fori_loop(0, tb, body, 0, unroll=DMA_UNROLL)

    @pl.when(i == 0)
    def _():
        pos_copy(0, 0).start()
        pos_copy(0, 0).wait()
        issue_gathers(0)

        @pl.when(nsteps > 1)
        def _():
            pos_copy(1, 1).start()

    @pl.when(i + 1 < nsteps)
    def _():
        pos_copy(i + 1, 1 - slot).wait()
        issue_gathers(1 - slot)

    @pl.when(i + 2 < nsteps)
    def _():
        pos_copy(i + 2, slot).start()

    for kk in range(TOP_K):
        _tokens_wait(ybuf.at[slot, kk], tb, sem_rows.at[slot])

    topw = topw_ref[...]
    groups = x1_ref.shape[1] // LANES
    moe = functools.reduce(jnp.add, [
        topw[:, kk:kk + 1] * _load_token_tiles(ybuf.at[slot, kk], tb, groups) for kk in range(TOP_K)])
    x2 = x1_ref[...] + moe
    gate = jax.nn.sigmoid(jnp.dot(_rms(x2, lnp_ref[...]).astype(BF16), wpg_ref[...],
                                  preferred_element_type=F32))
    proj = jnp.dot(p_ref[...].astype(BF16), wpp_ref[...], preferred_element_type=F32)
    o_ref[...] = _rms(x2 + gate * proj, lnf_ref[...])


def _ple(pos_flat, ys, topw, x1, p2, ln_ple, wpg, wpp, ln_final):
    n, d = x1.shape
    pd = p2.shape[1]
    tb = PLE_TB
    row = lambda i: (i, 0)
    const = lambda i: (0, 0)
    return pl.pallas_call(
        _ple_kernel,
        grid=(n // tb,),
        in_specs=[
            pl.BlockSpec(memory_space=pl.ANY), pl.BlockSpec(memory_space=pl.ANY),
            pl.BlockSpec((tb, LANES), row), pl.BlockSpec((tb, d), row), pl.BlockSpec((tb, pd), row),
            pl.BlockSpec((1, d), const), pl.BlockSpec((d, d), const),
            pl.BlockSpec((pd, d), const), pl.BlockSpec((1, d), const),
        ],
        out_specs=pl.BlockSpec((tb, d), row),
        out_shape=jax.ShapeDtypeStruct((n, d), F32),
        scratch_shapes=[
            pltpu.SMEM((2, tb * TOP_K), I32),
            pltpu.VMEM((2, TOP_K, tb * d // LANES, LANES), F32),
            pltpu.SemaphoreType.DMA((2,)),
            pltpu.SemaphoreType.DMA((2,)),
        ],
        compiler_params=_cparams(("arbitrary",)),
        name="combine_ple_final",
    )(pos_flat, ys, topw, x1, p2, ln_ple, wpg, wpp, ln_final)


def _layer(x2, p2, bsz, seq, ln_mix, w_in, w_gk, b_gk, gla_norm, rel_bias, w_out, ln_moe,
           w_router, b_router, w1, b1, w2, b2, ln_ple, w_ple_gate, w_ple_proj, ln_out):
    n, d = x2.shape
    qk, gv, aw = GLA_HEADS * GLA_DK, GLA_HEADS * GLA_DV, ATT_HEADS * ATT_DH
    names = ("q_g", "k_g", "v_g", "gk_low", "r_g", "q_a", "k_a", "v_a", "gt_a", "gt_b")
    widths = (qk, qk, gv, GLA_RANK, gv, aw, aw, aw, d, d)
    src = dict(zip(names, np.cumsum((0,) + widths[:-1]).tolist()))
    wid = dict(zip(names, widths))
    order = [nm for nm in names if nm != "gk_low"]
    col, off = {}, 0
    for nm in order:
        col[nm] = off
        off += wid[nm]
    w_main = jnp.concatenate([w_in[:, src[nm]:src[nm] + wid[nm]] for nm in order], axis=1).astype(BF16)
    w_low = jnp.pad(w_in[:, src["gk_low"]:src["gk_low"] + GLA_RANK],
                    ((0, 0), (0, LANES - GLA_RANK))).astype(BF16)

    z, gk = _in_proj(x2, ln_mix.reshape(1, d), w_main, w_low)
    z3 = z.reshape(bsz, seq, -1)
    gk3 = gk.reshape(bsz, seq, LANES)

    wgk_pad = jnp.pad(w_gk, ((0, LANES - GLA_RANK), (0, 0)))
    ya = _gla(z3, gk3, wgk_pad, b_gk.reshape(1, qk), gla_norm.reshape(1, GLA_DV), col)
    yb = _attn(z3, _attn_bias(rel_bias), col)

    wr_pad = jnp.pad(w_router, ((0, 0), (0, LANES - N_EXPERTS)))
    br_pad = jnp.pad(b_router, (0, LANES - N_EXPERTS)).reshape(1, LANES)
    x1, xm, topi, topw, cnt = _outproj(x2, ya.reshape(n, d), yb.reshape(n, d), w_out.astype(BF16),
                                       ln_moe.reshape(1, d), wr_pad, br_pad)

    counts = cnt[0, :N_EXPERTS].astype(I32)
    ntile = (counts + EXP_TM - 1) // EXP_TM
    tile_end = jnp.cumsum(ntile)
    tile_start = tile_end - ntile
    rows_total = n * TOP_K + N_EXPERTS * EXP_TM
    tiles = jnp.arange(rows_total // EXP_TM, dtype=I32)
    onehot = (tiles[:, None] >= tile_start[None, :]) & (tiles[:, None] < tile_end[None, :])
    te = jnp.sum(jnp.where(onehot, jnp.arange(N_EXPERTS, dtype=I32)[None, :], 0), axis=1)
    tv = jnp.sum(jnp.where(onehot, counts[None, :] - (tiles[:, None] - tile_start[None, :]) * EXP_TM, 0),
                 axis=1)
    te = jnp.where(tiles < tile_end[-1], te, N_EXPERTS - 1).astype(I32)
    tv = jnp.clip(tv, 0, EXP_TM).astype(I32)
    start_rows = jnp.pad((tile_start * EXP_TM).astype(F32), (0, LANES - N_EXPERTS)).reshape(1, LANES)

    pos_t = _positions(topi, start_rows)[:TOP_K]

    def pos_blocks(tokens):
        return pos_t.reshape(TOP_K, n // tokens, tokens).transpose(1, 0, 2).reshape(-1)

    xs = _dispatch(tv, pos_blocks(DISPATCH_TOKENS), xm, rows_total)

    ys = _experts(te, tv, xs, w1,
                  _pair_order(b1[:, 0::2], 1).reshape(N_EXPERTS, 1, -1),
                  _pair_order(b1[:, 1::2], 1).reshape(N_EXPERTS, 1, -1),
                  _pair_order(w2, 1).astype(BF16), b2.reshape(N_EXPERTS, 1, d))

    return _ple(pos_blocks(PLE_TB), ys, topw, x1, p2, ln_ple.reshape(1, d), w_ple_gate.astype(BF16),
                w_ple_proj.astype(BF16), ln_out.reshape(1, d))


def kernel(x, p, ln_mix, w_in, w_gk, b_gk, gla_norm, rel_bias, w_out, ln_moe, w_router, b_router,
           w1, b1, w2, b2, ln_ple, w_ple_gate, w_ple_proj, ln_final):
    bsz, seq, d = x.shape
    depth = p.shape[0]
    assert depth == 1, "the final RMSNorm is fused into the last layer's kernel"
    x2 = x.reshape(bsz * seq, d)
    out = _layer(x2, p[0].reshape(bsz * seq, -1), bsz, seq, ln_mix[0], w_in[0], w_gk[0], b_gk[0],
                 gla_norm[0], rel_bias[0], w_out[0], ln_moe[0], w_router[0], b_router[0],
                 w1[0], b1[0], w2[0], b2[0], ln_ple[0], w_ple_gate[0], w_ple_proj[0], ln_final)
    return out.reshape(bsz, seq, d)
```

```python
import functools

import numpy as np
import jax
import jax.numpy as jnp
from jax import lax
from jax.experimental import pallas as pl
from jax.experimental.pallas import tpu as pltpu

F32 = jnp.float32
BF16 = jnp.bfloat16
I32 = jnp.int32

LANES = 128
SUBLANES = 8
CHUNK = 64
GLA_HEADS = 4
GLA_DK = 128
GLA_DV = 256
GLA_RANK = 16
GLA_TAU = 16.0
ATT_HEADS = 16
ATT_DH = 64
ATT_PAST = 8
REL_CLIP = 256
N_EXPERTS = 32
TOP_K = 4
SWIGLU_ALPHA = 1.702
SWIGLU_LIMIT = 7.0
EPS = 1e-6
LOG2E = 1.4426950408889634

VMEM_LIMIT = 48 * 1024 * 1024

IN_TM, IN_TN = 2048, 1024
GLA_TT = 128
ATT_TQ = 256
ATT_STEP_HEADS = 16
OUT_TM = 512
RANK_TB = 512
EXP_TM = 512
DISPATCH_TOKENS = 512
PLE_TB = 256
DMA_UNROLL = 8


def _cparams(sem):
    return pltpu.CompilerParams(dimension_semantics=sem, vmem_limit_bytes=VMEM_LIMIT)


def _split_bf16(a):
    hi = a.astype(BF16)
    lo = (a - hi.astype(F32)).astype(BF16)
    return hi, lo


def _dot_split(a, b):
    a_hi, a_lo = _split_bf16(a)
    b_hi, b_lo = _split_bf16(b)
    d = functools.partial(jnp.dot, preferred_element_type=F32)
    return d(a_hi, b_hi) + (d(a_hi, b_lo) + d(a_lo, b_hi))


def _rms(x, gain):
    ms = jnp.mean(x * x, axis=-1, keepdims=True)
    return x * lax.rsqrt(ms + EPS) * gain


def _store_token_tiles(ref, val):
    groups = val.shape[1] // LANES
    for c in range(groups):
        ref[pl.ds(c, val.shape[0], stride=groups), :] = val[:, c * LANES:(c + 1) * LANES]


def _load_token_tiles(ref, rows, groups):
    return jnp.concatenate([ref[pl.ds(c, rows, stride=groups), :] for c in range(groups)], axis=1)


_NT = (((1,), (1,)), ((), ()))
_TN = (((0,), (0,)), ((), ()))


def _in_proj_kernel(x_ref, g_ref, w_ref, wlow_ref, z_ref, gk_ref, xn_ref):
    @pl.when(pl.program_id(1) == 0)
    def _():
        xn = _rms(x_ref[...], g_ref[...]).astype(BF16)
        xn_ref[...] = xn
        gk_ref[...] = jnp.dot(xn, wlow_ref[...], preferred_element_type=F32)

    z_ref[...] = jnp.dot(xn_ref[...], w_ref[...], preferred_element_type=F32).astype(BF16)


def _in_proj(x2, ln, w_main, w_low):
    n, d = x2.shape
    ncol = w_main.shape[1]
    return pl.pallas_call(
        _in_proj_kernel,
        grid=(n // IN_TM, ncol // IN_TN),
        in_specs=[
            pl.BlockSpec((IN_TM, d), lambda i, j: (i, 0)),
            pl.BlockSpec((1, d), lambda i, j: (0, 0)),
            pl.BlockSpec((d, IN_TN), lambda i, j: (0, j)),
            pl.BlockSpec((d, LANES), lambda i, j: (0, 0)),
        ],
        out_specs=[
            pl.BlockSpec((IN_TM, IN_TN), lambda i, j: (i, j)),
            pl.BlockSpec((IN_TM, LANES), lambda i, j: (i, 0)),
        ],
        out_shape=[
            jax.ShapeDtypeStruct((n, ncol), BF16),
            jax.ShapeDtypeStruct((n, LANES), F32),
        ],
        scratch_shapes=[pltpu.VMEM((IN_TM, d), BF16)],
        compiler_params=_cparams(("parallel", "arbitrary")),
        name="in_proj",
    )(x2, ln, w_main, w_low)


_GLA_LEVELS = (8, 16, 32)
_GLA_BOT = 8


def _gla_consts(tt):
    t = np.arange(tt)
    same_chunk = (t[:, None] // CHUNK) == (t[None, :] // CHUNK)
    tri = (same_chunk & (t[None, :] <= t[:, None])).astype(np.float32)
    mlev = []
    for h in _GLA_LEVELS:
        blk = (t[:, None] // (2 * h)) == (t[None, :] // (2 * h))
        m = blk & ((t[:, None] % (2 * h)) >= h) & ((t[None, :] % (2 * h)) < h)
        mlev.append(m.astype(np.float32))
    mbot = (((t[None, :] // _GLA_BOT) == (t[:, None] // _GLA_BOT))
            & ((t[None, :] % _GLA_BOT) <= (t[:, None] % _GLA_BOT)))
    spread = (np.arange(_GLA_BOT * GLA_DK)[:, None] // GLA_DK) == (t[None, :] % _GLA_BOT)
    return (jnp.asarray(tri, BF16), jnp.asarray(np.stack(mlev), F32),
            jnp.asarray(mbot.astype(np.float32), F32), jnp.asarray(spread.astype(np.float32), BF16))


def _gla_kernel(q_ref, k_ref, v_ref, r_ref, ga_ref, gk_ref, wgk_ref, bgk_ref, gn_ref,
                tri_ref, mlev_ref, mbot_ref, ones_ref, o_ref, st_ref):
    @pl.when(pl.program_id(1) == 0)
    def _():
        st_ref[...] = jnp.zeros_like(st_ref)

    xg = _dot_split(gk_ref[...], wgk_ref[...]) + bgk_ref[...]
    g = -(jnp.maximum(-xg, 0.0) + jnp.log(1.0 + jnp.exp(-jnp.abs(xg)))) * (1.0 / GLA_TAU)
    g_hi, g_lo = _split_bf16(g)
    tri = tri_ref[...]
    b_all = (jnp.dot(tri, g_hi, preferred_element_type=F32)
             + jnp.dot(tri, g_lo, preferred_element_type=F32))

    for hh in range(GLA_HEADS):
        kc = slice(hh * GLA_DK, (hh + 1) * GLA_DK)
        vc = slice(hh * GLA_DV, (hh + 1) * GLA_DV)
        y, st = _gla_head(q_ref[:, kc], k_ref[:, kc], v_ref[:, vc], b_all[:, kc], st_ref[hh],
                          mlev_ref, mbot_ref, ones_ref[...])
        st_ref[hh] = st
        r = r_ref[:, vc].astype(F32)
        ya = _rms(y, gn_ref[...]) * (r * jax.nn.sigmoid(r))
        o_ref[:, vc] = (jax.nn.sigmoid(ga_ref[:, vc].astype(F32)) * ya).astype(BF16)


def _gla_head(q, k, v, b, st, mlev_ref, mbot_ref, ones):
    tt = q.shape[0]
    q = q.astype(F32) * (GLA_DK ** -0.5)
    k = k.astype(F32)
    b = b * LOG2E

    s_intra = jnp.zeros((tt, tt), F32)
    for li, h in enumerate(_GLA_LEVELS):
        b3 = b.reshape(tt // (2 * h), 2 * h, GLA_DK)
        e = jnp.exp2(-jnp.abs(b3 - b3[:, h - 1:h, :])).reshape(tt, GLA_DK)
        sc = lax.dot_general((q * e).astype(BF16), (k * e).astype(BF16), _NT,
                             preferred_element_type=F32)
        s_intra = s_intra + sc * mlev_ref[li]
    nb = tt // _GLA_BOT
    b3 = b.reshape(nb, _GLA_BOT, GLA_DK)
    q3 = q.reshape(nb, _GLA_BOT, GLA_DK)
    k3 = k.reshape(nb, _GLA_BOT, GLA_DK)
    prods = []
    for s in range(_GLA_BOT):
        e = jnp.exp2(jnp.minimum(b3 - b3[:, s:s + 1, :], 0.0))
        prods.append((q3 * k3[:, s:s + 1, :] * e).reshape(tt, GLA_DK).astype(BF16))
    s_intra = s_intra + jnp.dot(jnp.concatenate(prods, axis=1), ones,
                                preferred_element_type=F32) * mbot_ref[...]
    o_intra = jnp.dot(s_intra.astype(BF16), v, preferred_element_type=F32)

    outs = []
    for c in range(tt // CHUNK):
        lo = c * CHUNK
        bc = b[lo:lo + CHUNK]
        bl = b[lo + CHUNK - 1:lo + CHUNK]
        qe = (q[lo:lo + CHUNK] * jnp.exp2(bc)).astype(BF16)
        outs.append(lax.dot_general(qe, st.astype(BF16), _NT, preferred_element_type=F32))
        kd = (k[lo:lo + CHUNK] * jnp.exp2(bl - bc)).astype(BF16)
        upd = lax.dot_general(v[lo:lo + CHUNK], kd, _TN, preferred_element_type=F32)
        st = st * jnp.exp2(bl) + upd
    return o_intra + jnp.concatenate(outs, axis=0), st


def _gla(z3, gk3, wgk_pad, bgk, gnorm, col):
    b, t, _ = z3.shape
    tt = GLA_TT
    tri, mlev, mbot, ones = _gla_consts(tt)
    qk, gv = GLA_HEADS * GLA_DK, GLA_HEADS * GLA_DV

    def zspec(width, off):
        return pl.BlockSpec((None, tt, width), lambda bi, ti, o=off // width: (bi, ti, o))

    const2 = lambda bi, ti: (0, 0)
    const3 = lambda bi, ti: (0, 0, 0)
    return pl.pallas_call(
        _gla_kernel,
        grid=(b, t // tt),
        in_specs=[
            zspec(qk, col["q_g"]), zspec(qk, col["k_g"]), zspec(gv, col["v_g"]),
            zspec(gv, col["r_g"]), zspec(gv, col["gt_a"]),
            pl.BlockSpec((None, tt, LANES), lambda bi, ti: (bi, ti, 0)),
            pl.BlockSpec((LANES, qk), const2),
            pl.BlockSpec((1, qk), const2),
            pl.BlockSpec((1, GLA_DV), const2),
            pl.BlockSpec((tt, tt), const2),
            pl.BlockSpec((len(_GLA_LEVELS), tt, tt), const3),
            pl.BlockSpec((tt, tt), const2),
            pl.BlockSpec((_GLA_BOT * GLA_DK, tt), const2),
        ],
        out_specs=pl.BlockSpec((None, tt, gv), lambda bi, ti: (bi, ti, 0)),
        out_shape=jax.ShapeDtypeStruct((b, t, gv), BF16),
        scratch_shapes=[pltpu.VMEM((GLA_HEADS, GLA_DV, GLA_DK), F32)],
        compiler_params=_cparams(("parallel", "arbitrary")),
        name="gla",
    )(z3, z3, z3, z3, z3, gk3, wgk_pad, bgk, gnorm, tri, mlev, mbot, ones)


_ATT_NKB = 3


def _attn_bias(rel_bias):
    tq = ATT_TQ
    nk = _ATT_NKB * tq
    back = nk - tq
    nheads = rel_bias.shape[0]
    span = nk + tq - 1
    dist = np.clip(np.arange(span) - (tq - 1), -REL_CLIP, REL_CLIP) + REL_CLIP
    g = rel_bias.astype(F32)[:, dist]
    x = jnp.pad(g[:, ::-1], ((0, 0), (0, 1)))
    tab = jnp.tile(x, (1, tq))[:, :tq * span].reshape(nheads, tq, span)[:, :, tq - 1:tq - 1 + nk]
    t = np.arange(tq)[:, None]
    w = np.arange(nk)[None, :]
    dc = t // CHUNK - np.floor_divide(w - back, CHUNK)
    band = (dc >= 0) & (dc <= ATT_PAST)
    valid = np.stack([band & (w // tq >= _ATT_NKB - 1 - e) for e in range(_ATT_NKB)])
    return jnp.where(jnp.asarray(valid)[:, None], tab[None], -jnp.inf)


def _attn_kernel(q_ref, k0_ref, k1_ref, k2_ref, v0_ref, v1_ref, v2_ref, gb_ref, bias_ref, o_ref):
    tq = q_ref.shape[0]
    first = lax.broadcasted_iota(I32, (1, LANES), 1) < ATT_DH
    scale = jnp.asarray(ATT_DH ** -0.5, BF16)
    krefs = (k0_ref, k1_ref, k2_ref)
    vrefs = (v0_ref, v1_ref, v2_ref)
    for lb in range(q_ref.shape[1] // LANES):
        cols = slice(lb * LANES, (lb + 1) * LANES)
        q = q_ref[:, cols] * scale
        zero = jnp.zeros_like(q)
        qs = jnp.concatenate([jnp.where(first, q, zero), jnp.where(first, zero, q)], axis=0)
        s = jnp.concatenate(
            [lax.dot_general(qs, r[:, cols], _NT, preferred_element_type=F32) for r in krefs], axis=1)
        s = s + jnp.concatenate([bias_ref[2 * lb], bias_ref[2 * lb + 1]], axis=0)
        p = jnp.exp(s - jnp.max(s, axis=-1, keepdims=True))
        l = jnp.sum(p, axis=-1, keepdims=True)
        pb = p.astype(BF16)
        pv = functools.reduce(jnp.add, [
            jnp.dot(pb[:, i * tq:(i + 1) * tq], vrefs[i][:, cols], preferred_element_type=F32)
            for i in range(_ATT_NKB)]) / l
        o = jnp.where(first, pv[:tq], pv[tq:])
        o_ref[:, cols] = (jax.nn.sigmoid(gb_ref[:, cols].astype(F32)) * o).astype(BF16)


def _attn(z3, bias, col):
    b, t, _ = z3.shape
    tq = ATT_TQ
    width = ATT_STEP_HEADS * ATT_DH
    steps = ATT_HEADS // ATT_STEP_HEADS

    def cur(off):
        return pl.BlockSpec((None, tq, width), lambda h, j, bi, o=off // width: (bi, j, o + h))

    def past(off, back):
        return pl.BlockSpec((None, tq, width),
                            lambda h, j, bi, o=off // width: (bi, jnp.maximum(j - back, 0), o + h))

    return pl.pallas_call(
        _attn_kernel,
        grid=(steps, t // tq, b),
        in_specs=[
            cur(col["q_a"]),
            past(col["k_a"], 2), past(col["k_a"], 1), cur(col["k_a"]),
            past(col["v_a"], 2), past(col["v_a"], 1), cur(col["v_a"]),
            cur(col["gt_b"]),
            pl.BlockSpec((None, ATT_STEP_HEADS, tq, _ATT_NKB * tq),
                         lambda h, j, bi: (jnp.minimum(j, _ATT_NKB - 1), h, 0, 0)),
        ],
        out_specs=pl.BlockSpec((None, tq, width), lambda h, j, bi: (bi, j, h)),
        out_shape=jax.ShapeDtypeStruct((b, t, ATT_HEADS * ATT_DH), BF16),
        compiler_params=_cparams(("parallel", "parallel", "parallel")),
        name="band_attn",
    )(z3, z3, z3, z3, z3, z3, z3, z3, bias)


def _outproj_kernel(x_ref, ya_ref, yb_ref, wo_ref, lnm_ref, wr_ref, br_ref,
                    x1_ref, xm_ref, topi_ref, topw_ref, cnt_ref):
    h = (ya_ref[...].astype(F32) + yb_ref[...].astype(F32)).astype(BF16)
    x1 = x_ref[...] + jnp.dot(h, wo_ref[...], preferred_element_type=F32)
    x1_ref[...] = x1
    xm = _rms(x1, lnm_ref[...])
    _store_token_tiles(xm_ref, xm)

    tm = xm.shape[0]
    lane = lax.broadcasted_iota(I32, (tm, LANES), 1)
    lanef = lane.astype(F32)
    logits = _dot_split(xm, wr_ref[...]) + br_ref[...]
    l = jnp.where(lane < N_EXPERTS, logits, -jnp.inf)
    vals, idxs = [], []
    picked = jnp.zeros((tm, LANES), F32)
    for _ in range(TOP_K):
        m = jnp.max(l, axis=-1, keepdims=True)
        idx = jnp.min(jnp.where(l == m, lanef, float(LANES)), axis=-1, keepdims=True)
        vals.append(m)
        idxs.append(idx)
        hit = lanef == idx
        picked = jnp.where(hit, 1.0, picked)
        l = jnp.where(hit, -jnp.inf, l)
    es = [jnp.exp(vv - vals[0]) for vv in vals]
    tot = functools.reduce(jnp.add, es)
    topw = jnp.zeros((tm, LANES), F32)
    topi = jnp.zeros((tm, LANES), F32)
    for kk in range(TOP_K):
        topw = jnp.where(lane == kk, es[kk] / tot, topw)
        topi = jnp.where(lane == kk, idxs[kk], topi)
    topw_ref[...] = topw
    topi_ref[...] = topi.astype(I32)

    @pl.when(pl.program_id(0) == 0)
    def _():
        cnt_ref[...] = jnp.zeros_like(cnt_ref)

    cnt_ref[...] = cnt_ref[...] + jnp.sum(picked, axis=0, keepdims=True)


def _outproj(x2, ya, yb, w_out, ln_moe, wr_pad, br_pad):
    n, d = x2.shape
    tm = OUT_TM
    row = lambda i: (i, 0)
    const = lambda i: (0, 0)
    return pl.pallas_call(
        _outproj_kernel,
        grid=(n // tm,),
        in_specs=[
            pl.BlockSpec((tm, d), row), pl.BlockSpec((tm, d), row), pl.BlockSpec((tm, d), row),
            pl.BlockSpec((d, d), const), pl.BlockSpec((1, d), const),
            pl.BlockSpec((d, LANES), const), pl.BlockSpec((1, LANES), const),
        ],
        out_specs=[
            pl.BlockSpec((tm, d), row), pl.BlockSpec((tm * d // LANES, LANES), row),
            pl.BlockSpec((tm, LANES), row), pl.BlockSpec((tm, LANES), row),
            pl.BlockSpec((1, LANES), const),
        ],
        out_shape=[
            jax.ShapeDtypeStruct((n, d), F32), jax.ShapeDtypeStruct((n * d // LANES, LANES), F32),
            jax.ShapeDtypeStruct((n, LANES), I32), jax.ShapeDtypeStruct((n, LANES), F32),
            jax.ShapeDtypeStruct((1, LANES), F32),
        ],
        compiler_params=_cparams(("arbitrary",)),
        name="out_proj_router",
    )(x2, ya, yb, w_out, ln_moe, wr_pad, br_pad)


def _pos_kernel(topi_ref, start_ref, tri_ref, pos_ref, carry_ref):
    @pl.when(pl.program_id(0) == 0)
    def _():
        carry_ref[...] = start_ref[...]

    ti = topi_ref[...]
    tb = ti.shape[0]
    lane = lax.broadcasted_iota(I32, (tb, LANES), 1)
    sel = [lane == ti[:, kk:kk + 1] for kk in range(TOP_K)]
    oh = functools.reduce(jnp.add, [jnp.where(s, 1.0, 0.0) for s in sel])
    row = carry_ref[...] + jnp.dot(tri_ref[...], oh.astype(BF16), preferred_element_type=F32)
    out = jnp.zeros((tb, LANES), F32)
    for kk in range(TOP_K):
        rk = jnp.sum(jnp.where(sel[kk], row, 0.0), axis=-1, keepdims=True)
        out = jnp.where(lane == kk, rk, out)
    pos_ref[...] = jnp.transpose(out)[:pos_ref.shape[0]].astype(I32)
    carry_ref[...] = carry_ref[...] + jnp.sum(oh, axis=0, keepdims=True)


def _positions(topi, start_rows):
    n = topi.shape[0]
    tb = RANK_TB
    t = np.arange(tb)
    tri = jnp.asarray((t[None, :] < t[:, None]).astype(np.float32), BF16)
    return pl.pallas_call(
        _pos_kernel,
        grid=(n // tb,),
        in_specs=[pl.BlockSpec((tb, LANES), lambda i: (i, 0)),
                  pl.BlockSpec((1, LANES), lambda i: (0, 0)),
                  pl.BlockSpec((tb, tb), lambda i: (0, 0))],
        out_specs=pl.BlockSpec((SUBLANES, tb), lambda i: (0, i)),
        out_shape=jax.ShapeDtypeStruct((SUBLANES, n), I32),
        scratch_shapes=[pltpu.VMEM((1, LANES), F32)],
        compiler_params=_cparams(("arbitrary",)),
        name="route_positions",
    )(topi, start_rows, tri)


def _token_copy(src, src_tok, dst, dst_tok, sem):
    s0 = pl.multiple_of(src_tok * SUBLANES, SUBLANES)
    d0 = pl.multiple_of(dst_tok * SUBLANES, SUBLANES)
    return pltpu.make_async_copy(src.at[pl.ds(s0, SUBLANES)], dst.at[pl.ds(d0, SUBLANES)], sem)


def _tokens_wait(ref, ntok, sem):
    pltpu.make_async_copy(ref.at[pl.ds(0, ntok * SUBLANES)], ref.at[pl.ds(0, ntok * SUBLANES)], sem).wait()


def _dispatch_kernel(tv_ref, pos_hbm, xm_ref, xs_hbm, pos_smem, zeros_ref, sem_pos, sem_rows, sem_fill):
    i = pl.program_id(0)
    nsteps = pl.num_programs(0)
    tokens = xm_ref.shape[0] // SUBLANES
    ch = tokens * TOP_K
    slot = i % 2

    def pos_copy(step, sl):
        return pltpu.make_async_copy(pos_hbm.at[pl.ds(step * ch, ch)], pos_smem.at[sl], sem_pos.at[sl])

    @pl.when(i == 0)
    def _():
        pos_copy(0, 0).start()
        tm = zeros_ref.shape[0]
        zeros_ref[...] = jnp.zeros_like(zeros_ref)

        def fill(t):
            return pltpu.make_async_copy(zeros_ref, xs_hbm.at[pl.ds(t * tm, tm)], sem_fill)

        def start(t, carry):
            @pl.when(tv_ref[t] < EXP_TM)
            def _():
                fill(t).start()
            return carry

        def wait(t, carry):
            @pl.when(tv_ref[t] < EXP_TM)
            def _():
                fill(t).wait()
            return carry

        ntiles = xs_hbm.shape[0] // tm
        lax.fori_loop(0, ntiles, start, 0)
        lax.fori_loop(0, ntiles, wait, 0)

    @pl.when(i + 1 < nsteps)
    def _():
        pos_copy(i + 1, 1 - slot).start()

    pos_copy(i, slot).wait()

    def body(t, carry):
        for kk in range(TOP_K):
            _token_copy(xm_ref, t, xs_hbm, pos_smem[slot, kk * tokens + t], sem_rows).start(priority=kk % 2)
        return carry

    lax.fori_loop(0, tokens, body, 0, unroll=DMA_UNROLL)
    _tokens_wait(xs_hbm, ch, sem_rows)


def _dispatch(tile_valid, pos_flat, xm, rows_total):
    tokens = DISPATCH_TOKENS
    n = xm.shape[0] // SUBLANES
    return pl.pallas_call(
        _dispatch_kernel,
        grid_spec=pltpu.PrefetchScalarGridSpec(
            num_scalar_prefetch=1,
            grid=(n // tokens,),
            in_specs=[pl.BlockSpec(memory_space=pl.ANY),
                      pl.BlockSpec((tokens * SUBLANES, LANES), lambda i, tv: (i, 0))],
            out_specs=pl.BlockSpec(memory_space=pl.ANY),
            scratch_shapes=[pltpu.SMEM((2, tokens * TOP_K), I32),
                            pltpu.VMEM((EXP_TM * SUBLANES, LANES), F32),
                            pltpu.SemaphoreType.DMA((2,)), pltpu.SemaphoreType.DMA,
                            pltpu.SemaphoreType.DMA],
        ),
        out_shape=jax.ShapeDtypeStruct((rows_total * SUBLANES, LANES), F32),
        compiler_params=_cparams(("arbitrary",)),
        name="dispatch",
    )(tile_valid, pos_flat, xm)


def _deinterleave_matrix():
    j = np.arange(2 * LANES)[:, None]
    c = np.arange(2 * LANES)[None, :]
    sel = np.where(c < LANES, j == 2 * c, j == 2 * (c - LANES) + 1)
    return jnp.asarray(sel.astype(np.float32), BF16)


def _expert_kernel(te_ref, tv_ref, xs_ref, w1_ref, b1g_ref, b1l_ref, w2_ref, b2_ref, sel_ref, ys_ref,
                   w1g_ref, w1l_ref, w2p_ref):
    t = pl.program_id(0)
    nvalid = tv_ref[t]
    new_expert = (t == 0) | (te_ref[t] != te_ref[jnp.maximum(t - 1, 0)])

    @pl.when((nvalid > 0) & new_expert)
    def _():
        sel = sel_ref[...]
        for m in range(w1g_ref.shape[1] // LANES):
            pair = w1_ref[:, 2 * m * LANES:(2 * m + 2) * LANES].astype(BF16)
            split = jnp.dot(pair, sel, preferred_element_type=F32).astype(BF16)
            w1g_ref[:, m * LANES:(m + 1) * LANES] = split[:, :LANES]
            w1l_ref[:, m * LANES:(m + 1) * LANES] = split[:, LANES:]
        w2p_ref[...] = w2_ref[...].astype(BF16)

    @pl.when(nvalid > 0)
    def _():
        tm = xs_ref.shape[0] // SUBLANES
        x = _load_token_tiles(xs_ref, tm, SUBLANES).astype(BF16)
        hg = jnp.dot(x, w1g_ref[...], preferred_element_type=F32) + b1g_ref[...]
        hl = jnp.dot(x, w1l_ref[...], preferred_element_type=F32) + b1l_ref[...]
        glu = jnp.minimum(hg, SWIGLU_LIMIT)
        lin = jnp.clip(hl, -SWIGLU_LIMIT, SWIGLU_LIMIT)
        act = glu * jax.nn.sigmoid(SWIGLU_ALPHA * glu) * (lin + 1.0)
        y = jnp.dot(act.astype(BF16), w2p_ref[...], preferred_element_type=F32) + b2_ref[...]
        _store_token_tiles(ys_ref, y)

    @pl.when(nvalid <= 0)
    def _():
        ys_ref[...] = jnp.zeros_like(ys_ref)


def _experts(tile_expert, tile_valid, xs, w1, b1g, b1l, w2, b2):
    f, d = w2.shape[1:]
    p = xs.shape[0] // SUBLANES
    tm = EXP_TM
    wmap = lambda t, te, tv: (te[t], 0, 0)
    return pl.pallas_call(
        _expert_kernel,
        grid_spec=pltpu.PrefetchScalarGridSpec(
            num_scalar_prefetch=2,
            grid=(p // tm,),
            in_specs=[
                pl.BlockSpec((tm * SUBLANES, LANES), lambda t, te, tv: (t, 0)),
                pl.BlockSpec((None, d, 2 * f), wmap),
                pl.BlockSpec((None, 1, f), wmap), pl.BlockSpec((None, 1, f), wmap),
                pl.BlockSpec((None, f, d), wmap), pl.BlockSpec((None, 1, d), wmap),
                pl.BlockSpec((2 * LANES, 2 * LANES), lambda t, te, tv: (0, 0)),
            ],
            out_specs=pl.BlockSpec((tm * SUBLANES, LANES), lambda t, te, tv: (t, 0)),
            scratch_shapes=[pltpu.VMEM((d, f), BF16), pltpu.VMEM((d, f), BF16),
                            pltpu.VMEM((f, d), BF16)],
        ),
        out_shape=jax.ShapeDtypeStruct((p * SUBLANES, LANES), F32),
        compiler_params=_cparams(("arbitrary",)),
        name="experts",
    )(tile_expert, tile_valid, xs, w1, b1g, b1l, w2, b2, _deinterleave_matrix())


def _ple_kernel(pos_hbm, ys_hbm, topw_ref, x1_ref, p_ref, lnp_ref, wpg_ref, wpp_ref, lnf_ref, o_ref,
                pos_smem, ybuf, sem_pos, sem_rows):
    i = pl.program_id(0)
    nsteps = pl.num_programs(0)
    tb = x1_ref.shape[0]
    ch = tb * TOP_K
    slot = i % 2

    def pos_copy(step, sl):
        return pltpu.make_async_copy(pos_hbm.at[pl.ds(step * ch, ch)], pos_smem.at[sl], sem_pos.at[sl])

    def issue_gathers(sl):
        def body(t, carry):
            for kk in range(TOP_K):
                _token_copy(ys_hbm, pos_smem[sl, kk * tb + t], ybuf.at[sl, kk], t,
                            sem_rows.at[sl]).start(priority=kk % 2)
            return carry
        lax.fori_loop(0, tb, body, 0, unroll=DMA_UNROLL)

    @pl.when(i == 0)
    def _():
        pos_copy(0, 0).start()
        pos_copy(0, 0).wait()
        issue_gathers(0)

        @pl.when(nsteps > 1)
        def _():
            pos_copy(1, 1).start()

    @pl.when(i + 1 < nsteps)
    def _():
        pos_copy(i + 1, 1 - slot).wait()
        issue_gathers(1 - slot)

    @pl.when(i + 2 < nsteps)
    def _():
        pos_copy(i + 2, slot).start()

    for kk in range(TOP_K):
        _tokens_wait(ybuf.at[slot, kk], tb, sem_rows.at[slot])

    topw = topw_ref[...]
    groups = x1_ref.shape[1] // LANES
    moe = functools.reduce(jnp.add, [
        topw[:, kk:kk + 1] * _load_token_tiles(ybuf.at[slot, kk], tb, groups) for kk in range(TOP_K)])
    x2 = x1_ref[...] + moe
    gate = jax.nn.sigmoid(jnp.dot(_rms(x2, lnp_ref[...]).astype(BF16), wpg_ref[...],
                                  preferred_element_type=F32))
    proj = jnp.dot(p_ref[...].astype(BF16), wpp_ref[...], preferred_element_type=F32)
    o_ref[...] = _rms(x2 + gate * proj, lnf_ref[...])


def _ple(pos_flat, ys, topw, x1, p2, ln_ple, wpg, wpp, ln_final):
    n, d = x1.shape
    pd = p2.shape[1]
    tb = PLE_TB
    row = lambda i: (i, 0)
    const = lambda i: (0, 0)
    return pl.pallas_call(
        _ple_kernel,
        grid=(n // tb,),
        in_specs=[
            pl.BlockSpec(memory_space=pl.ANY), pl.BlockSpec(memory_space=pl.ANY),
            pl.BlockSpec((tb, LANES), row), pl.BlockSpec((tb, d), row), pl.BlockSpec((tb, pd), row),
            pl.BlockSpec((1, d), const), pl.BlockSpec((d, d), const),
            pl.BlockSpec((pd, d), const), pl.BlockSpec((1, d), const),
        ],
        out_specs=pl.BlockSpec((tb, d), row),
        out_shape=jax.ShapeDtypeStruct((n, d), F32),
        scratch_shapes=[
            pltpu.SMEM((2, tb * TOP_K), I32),
            pltpu.VMEM((2, TOP_K, tb * d // LANES, LANES), F32),
            pltpu.SemaphoreType.DMA((2,)),
            pltpu.SemaphoreType.DMA((2,)),
        ],
        compiler_params=_cparams(("arbitrary",)),
        name="combine_ple_final",
    )(pos_flat, ys, topw, x1, p2, ln_ple, wpg, wpp, ln_final)


def _layer(x2, p2, bsz, seq, ln_mix, w_in, w_gk, b_gk, gla_norm, rel_bias, w_out, ln_moe,
           w_router, b_router, w1, b1, w2, b2, ln_ple, w_ple_gate, w_ple_proj, ln_out):
    n, d = x2.shape
    qk, gv, aw = GLA_HEADS * GLA_DK, GLA_HEADS * GLA_DV, ATT_HEADS * ATT_DH
    names = ("q_g", "k_g", "v_g", "gk_low", "r_g", "q_a", "k_a", "v_a", "gt_a", "gt_b")
    widths = (qk, qk, gv, GLA_RANK, gv, aw, aw, aw, d, d)
    src = dict(zip(names, np.cumsum((0,) + widths[:-1]).tolist()))
    wid = dict(zip(names, widths))
    order = [nm for nm in names if nm != "gk_low"]
    col, off = {}, 0
    for nm in order:
        col[nm] = off
        off += wid[nm]
    w_main = jnp.concatenate([w_in[:, src[nm]:src[nm] + wid[nm]] for nm in order], axis=1).astype(BF16)
    w_low = jnp.pad(w_in[:, src["gk_low"]:src["gk_low"] + GLA_RANK],
                    ((0, 0), (0, LANES - GLA_RANK))).astype(BF16)

    z, gk = _in_proj(x2, ln_mix.reshape(1, d), w_main, w_low)
    z3 = z.reshape(bsz, seq, -1)
    gk3 = gk.reshape(bsz, seq, LANES)

    wgk_pad = jnp.pad(w_gk, ((0, LANES - GLA_RANK), (0, 0)))
    ya = _gla(z3, gk3, wgk_pad, b_gk.reshape(1, qk), gla_norm.reshape(1, GLA_DV), col)
    yb = _attn(z3, _attn_bias(rel_bias), col)

    wr_pad = jnp.pad(w_router, ((0, 0), (0, LANES - N_EXPERTS)))
    br_pad = jnp.pad(b_router, (0, LANES - N_EXPERTS)).reshape(1, LANES)
    x1, xm, topi, topw, cnt = _outproj(x2, ya.reshape(n, d), yb.reshape(n, d), w_out.astype(BF16),
                                       ln_moe.reshape(1, d), wr_pad, br_pad)

    counts = cnt[0, :N_EXPERTS].astype(I32)
    ntile = (counts + EXP_TM - 1) // EXP_TM
    tile_end = jnp.cumsum(ntile)
    tile_start = tile_end - ntile
    rows_total = n * TOP_K + N_EXPERTS * EXP_TM
    tiles = jnp.arange(rows_total // EXP_TM, dtype=I32)
    onehot = (tiles[:, None] >= tile_start[None, :]) & (tiles[:, None] < tile_end[None, :])
    te = jnp.sum(jnp.where(onehot, jnp.arange(N_EXPERTS, dtype=I32)[None, :], 0), axis=1)
    tv = jnp.sum(jnp.where(onehot, counts[None, :] - (tiles[:, None] - tile_start[None, :]) * EXP_TM, 0),
                 axis=1)
    te = jnp.where(tiles < tile_end[-1], te, N_EXPERTS - 1).astype(I32)
    tv = jnp.clip(tv, 0, EXP_TM).astype(I32)
    start_rows = jnp.pad((tile_start * EXP_TM).astype(F32), (0, LANES - N_EXPERTS)).reshape(1, LANES)

    pos_t = _positions(topi, start_rows)[:TOP_K]

    def pos_blocks(tokens):
        return pos_t.reshape(TOP_K, n // tokens, tokens).transpose(1, 0, 2).reshape(-1)

    xs = _dispatch(tv, pos_blocks(DISPATCH_TOKENS), xm, rows_total)

    ys = _experts(te, tv, xs, w1,
                  b1[:, 0::2].reshape(N_EXPERTS, 1, -1), b1[:, 1::2].reshape(N_EXPERTS, 1, -1),
                  w2, b2.reshape(N_EXPERTS, 1, d))

    return _ple(pos_blocks(PLE_TB), ys, topw, x1, p2, ln_ple.reshape(1, d), w_ple_gate.astype(BF16),
                w_ple_proj.astype(BF16), ln_out.reshape(1, d))


def kernel(x, p, ln_mix, w_in, w_gk, b_gk, gla_norm, rel_bias, w_out, ln_moe, w_router, b_router,
           w1, b1, w2, b2, ln_ple, w_ple_gate, w_ple_proj, ln_final):
    bsz, seq, d = x.shape
    depth = p.shape[0]
    assert depth == 1, "the final RMSNorm is fused into the last layer's kernel"
    assert d == SUBLANES * LANES, "token-tile layout needs one vreg tile per token"
    assert seq % max(ATT_TQ, GLA_TT) == 0
    assert (bsz * seq) % max(IN_TM, OUT_TM, RANK_TB, DISPATCH_TOKENS, PLE_TB) == 0
    x2 = x.reshape(bsz * seq, d)
    out = _layer(x2, p[0].reshape(bsz * seq, -1), bsz, seq, ln_mix[0], w_in[0], w_gk[0], b_gk[0],
                 gla_norm[0], rel_bias[0], w_out[0], ln_moe[0], w_router[0], b_router[0],
                 w1[0], b1[0], w2[0], b2[0], ln_ple[0], w_ple_gate[0], w_ple_proj[0], ln_final)
    return out.reshape(bsz, seq, d)
```

```python
import functools

import numpy as np
import jax
import jax.numpy as jnp
from jax import lax
from jax.experimental import pallas as pl
from jax.experimental.pallas import tpu as pltpu

F32 = jnp.float32
BF16 = jnp.bfloat16
I32 = jnp.int32

LANES = 128
SUBLANES = 8
CHUNK = 64
GLA_HEADS = 4
GLA_DK = 128
GLA_DV = 256
GLA_RANK = 16
GLA_TAU = 16.0
ATT_HEADS = 16
ATT_DH = 64
ATT_PAST = 8
REL_CLIP = 256
N_EXPERTS = 32
TOP_K = 4
SWIGLU_ALPHA = 1.702
SWIGLU_LIMIT = 7.0
EPS = 1e-6
LOG2E = 1.4426950408889634

VMEM_LIMIT = 48 * 1024 * 1024

IN_TM, IN_TN = 512, 1024
GLA_TT = 128
ATT_TQ = 256
ATT_STEP_HEADS = 16
OUT_TM = 512
RANK_TB = 512
EXP_TM = 512
DISPATCH_TOKENS = 512
PLE_TB = 256
DMA_UNROLL = 8


def _cparams(sem):
    return pltpu.CompilerParams(dimension_semantics=sem, vmem_limit_bytes=VMEM_LIMIT)


def _split_bf16(a):
    hi = a.astype(BF16)
    lo = (a - hi.astype(F32)).astype(BF16)
    return hi, lo


def _dot_split(a, b):
    a_hi, a_lo = _split_bf16(a)
    b_hi, b_lo = _split_bf16(b)
    d = functools.partial(jnp.dot, preferred_element_type=F32)
    return d(a_hi, b_hi) + (d(a_hi, b_lo) + d(a_lo, b_hi))


def _rms(x, gain):
    ms = jnp.mean(x * x, axis=-1, keepdims=True)
    return x * lax.rsqrt(ms + EPS) * gain


def _store_token_tiles(ref, val):
    groups = val.shape[1] // LANES
    for c in range(groups):
        ref[pl.ds(c, val.shape[0], stride=groups), :] = val[:, c * LANES:(c + 1) * LANES]


def _load_token_tiles(ref, rows, groups):
    return jnp.concatenate([ref[pl.ds(c, rows, stride=groups), :] for c in range(groups)], axis=1)


_NT = (((1,), (1,)), ((), ()))
_TN = (((0,), (0,)), ((), ()))


def _in_proj_kernel(x_ref, g_ref, w_ref, wlow_ref, z_ref, gk_ref):
    xn = _rms(x_ref[...], g_ref[...]).astype(BF16)
    gk_ref[...] = jnp.dot(xn, wlow_ref[...], preferred_element_type=F32)
    for j in range(w_ref.shape[1] // IN_TN):
        cols = slice(j * IN_TN, (j + 1) * IN_TN)
        z_ref[:, cols] = jnp.dot(xn, w_ref[:, cols], preferred_element_type=F32).astype(BF16)


def _in_proj(x2, ln, w_main, w_low):
    n, d = x2.shape
    ncol = w_main.shape[1]
    once = pl.Buffered(1)
    return pl.pallas_call(
        _in_proj_kernel,
        grid=(n // IN_TM,),
        in_specs=[
            pl.BlockSpec((IN_TM, d), lambda i: (i, 0)),
            pl.BlockSpec((1, d), lambda i: (0, 0)),
            pl.BlockSpec((d, ncol), lambda i: (0, 0), pipeline_mode=once),
            pl.BlockSpec((d, LANES), lambda i: (0, 0), pipeline_mode=once),
        ],
        out_specs=[
            pl.BlockSpec((IN_TM, ncol), lambda i: (i, 0)),
            pl.BlockSpec((IN_TM, LANES), lambda i: (i, 0)),
        ],
        out_shape=[
            jax.ShapeDtypeStruct((n, ncol), BF16),
            jax.ShapeDtypeStruct((n, LANES), F32),
        ],
        compiler_params=_cparams(("parallel",)),
        name="in_proj",
    )(x2, ln, w_main, w_low)


_GLA_LEVELS = (8, 16, 32)
_GLA_BOT = 8


def _gla_consts(tt):
    t = np.arange(tt)
    same_chunk = (t[:, None] // CHUNK) == (t[None, :] // CHUNK)
    tri = (same_chunk & (t[None, :] <= t[:, None])).astype(np.float32)
    mlev = []
    for h in _GLA_LEVELS:
        blk = (t[:, None] // (2 * h)) == (t[None, :] // (2 * h))
        m = blk & ((t[:, None] % (2 * h)) >= h) & ((t[None, :] % (2 * h)) < h)
        mlev.append(m.astype(np.float32))
    mbot = (((t[None, :] // _GLA_BOT) == (t[:, None] // _GLA_BOT))
            & ((t[None, :] % _GLA_BOT) <= (t[:, None] % _GLA_BOT)))
    spread = (np.arange(_GLA_BOT * GLA_DK)[:, None] // GLA_DK) == (t[None, :] % _GLA_BOT)
    return (jnp.asarray(tri, BF16), jnp.asarray(np.stack(mlev), F32),
            jnp.asarray(mbot.astype(np.float32), F32), jnp.asarray(spread.astype(np.float32), BF16))


def _gla_kernel(q_ref, k_ref, v_ref, r_ref, ga_ref, gk_ref, wgk_ref, bgk_ref, gn_ref,
                tri_ref, mlev_ref, mbot_ref, ones_ref, o_ref, st_ref):
    @pl.when(pl.program_id(1) == 0)
    def _():
        st_ref[...] = jnp.zeros_like(st_ref)

    xg = _dot_split(gk_ref[...], wgk_ref[...]) + bgk_ref[...]
    g = -(jnp.maximum(-xg, 0.0) + jnp.log(1.0 + jnp.exp(-jnp.abs(xg)))) * (1.0 / GLA_TAU)
    g_hi, g_lo = _split_bf16(g)
    tri = tri_ref[...]
    b_all = (jnp.dot(tri, g_hi, preferred_element_type=F32)
             + jnp.dot(tri, g_lo, preferred_element_type=F32))

    for hh in range(GLA_HEADS):
        kc = slice(hh * GLA_DK, (hh + 1) * GLA_DK)
        vc = slice(hh * GLA_DV, (hh + 1) * GLA_DV)
        y, st = _gla_head(q_ref[:, kc], k_ref[:, kc], v_ref[:, vc], b_all[:, kc], st_ref[hh],
                          mlev_ref, mbot_ref, ones_ref[...])
        st_ref[hh] = st
        r = r_ref[:, vc].astype(F32)
        ya = _rms(y, gn_ref[...]) * (r * jax.nn.sigmoid(r))
        o_ref[:, vc] = (jax.nn.sigmoid(ga_ref[:, vc].astype(F32)) * ya).astype(BF16)


def _gla_head(q, k, v, b, st, mlev_ref, mbot_ref, ones):
    tt = q.shape[0]
    q = q.astype(F32) * (GLA_DK ** -0.5)
    k = k.astype(F32)
    b = b * LOG2E

    s_intra = jnp.zeros((tt, tt), F32)
    for li, h in enumerate(_GLA_LEVELS):
        b3 = b.reshape(tt // (2 * h), 2 * h, GLA_DK)
        e = jnp.exp2(-jnp.abs(b3 - b3[:, h - 1:h, :])).reshape(tt, GLA_DK)
        sc = lax.dot_general((q * e).astype(BF16), (k * e).astype(BF16), _NT,
                             preferred_element_type=F32)
        s_intra = s_intra + sc * mlev_ref[li]
    nb = tt // _GLA_BOT
    b3 = b.reshape(nb, _GLA_BOT, GLA_DK)
    q3 = q.reshape(nb, _GLA_BOT, GLA_DK)
    k3 = k.reshape(nb, _GLA_BOT, GLA_DK)
    prods = []
    for s in range(_GLA_BOT):
        e = jnp.exp2(jnp.minimum(b3 - b3[:, s:s + 1, :], 0.0))
        prods.append((q3 * k3[:, s:s + 1, :] * e).reshape(tt, GLA_DK).astype(BF16))
    s_intra = s_intra + jnp.dot(jnp.concatenate(prods, axis=1), ones,
                                preferred_element_type=F32) * mbot_ref[...]
    o_intra = jnp.dot(s_intra.astype(BF16), v, preferred_element_type=F32)

    outs = []
    for c in range(tt // CHUNK):
        lo = c * CHUNK
        bc = b[lo:lo + CHUNK]
        bl = b[lo + CHUNK - 1:lo + CHUNK]
        qe = (q[lo:lo + CHUNK] * jnp.exp2(bc)).astype(BF16)
        outs.append(lax.dot_general(qe, st.astype(BF16), _NT, preferred_element_type=F32))
        kd = (k[lo:lo + CHUNK] * jnp.exp2(bl - bc)).astype(BF16)
        upd = lax.dot_general(v[lo:lo + CHUNK], kd, _TN, preferred_element_type=F32)
        st = st * jnp.exp2(bl) + upd
    return o_intra + jnp.concatenate(outs, axis=0), st


def _gla(z3, gk3, wgk_pad, bgk, gnorm, col):
    b, t, _ = z3.shape
    tt = GLA_TT
    tri, mlev, mbot, ones = _gla_consts(tt)
    qk, gv = GLA_HEADS * GLA_DK, GLA_HEADS * GLA_DV

    def zspec(width, off):
        return pl.BlockSpec((None, tt, width), lambda bi, ti, o=off // width: (bi, ti, o))

    const2 = lambda bi, ti: (0, 0)
    const3 = lambda bi, ti: (0, 0, 0)
    return pl.pallas_call(
        _gla_kernel,
        grid=(b, t // tt),
        in_specs=[
            zspec(qk, col["q_g"]), zspec(qk, col["k_g"]), zspec(gv, col["v_g"]),
            zspec(gv, col["r_g"]), zspec(gv, col["gt_a"]),
            pl.BlockSpec((None, tt, LANES), lambda bi, ti: (bi, ti, 0)),
            pl.BlockSpec((LANES, qk), const2),
            pl.BlockSpec((1, qk), const2),
            pl.BlockSpec((1, GLA_DV), const2),
            pl.BlockSpec((tt, tt), const2),
            pl.BlockSpec((len(_GLA_LEVELS), tt, tt), const3),
            pl.BlockSpec((tt, tt), const2),
            pl.BlockSpec((_GLA_BOT * GLA_DK, tt), const2),
        ],
        out_specs=pl.BlockSpec((None, tt, gv), lambda bi, ti: (bi, ti, 0)),
        out_shape=jax.ShapeDtypeStruct((b, t, gv), BF16),
        scratch_shapes=[pltpu.VMEM((GLA_HEADS, GLA_DV, GLA_DK), F32)],
        compiler_params=_cparams(("parallel", "arbitrary")),
        name="gla",
    )(z3, z3, z3, z3, z3, gk3, wgk_pad, bgk, gnorm, tri, mlev, mbot, ones)


_ATT_NKB = 3


def _attn_bias(rel_bias):
    tq = ATT_TQ
    nk = _ATT_NKB * tq
    back = nk - tq
    nheads = rel_bias.shape[0]
    span = nk + tq - 1
    dist = np.clip(np.arange(span) - (tq - 1), -REL_CLIP, REL_CLIP) + REL_CLIP
    g = rel_bias.astype(F32)[:, dist]
    x = jnp.pad(g[:, ::-1], ((0, 0), (0, 1)))
    tab = jnp.tile(x, (1, tq))[:, :tq * span].reshape(nheads, tq, span)[:, :, tq - 1:tq - 1 + nk]
    t = np.arange(tq)[:, None]
    w = np.arange(nk)[None, :]
    dc = t // CHUNK - np.floor_divide(w - back, CHUNK)
    band = (dc >= 0) & (dc <= ATT_PAST)
    valid = np.stack([band & (w // tq >= _ATT_NKB - 1 - e) for e in range(_ATT_NKB)])
    return jnp.where(jnp.asarray(valid)[:, None], tab[None], -jnp.inf)


def _attn_kernel(q_ref, k0_ref, k1_ref, k2_ref, v0_ref, v1_ref, v2_ref, gb_ref, bias_ref, o_ref):
    tq = q_ref.shape[0]
    first = lax.broadcasted_iota(I32, (1, LANES), 1) < ATT_DH
    scale = jnp.asarray(ATT_DH ** -0.5, BF16)
    krefs = (k0_ref, k1_ref, k2_ref)
    vrefs = (v0_ref, v1_ref, v2_ref)
    for lb in range(q_ref.shape[1] // LANES):
        cols = slice(lb * LANES, (lb + 1) * LANES)
        q = q_ref[:, cols] * scale
        zero = jnp.zeros_like(q)
        qs = jnp.concatenate([jnp.where(first, q, zero), jnp.where(first, zero, q)], axis=0)
        s = jnp.concatenate(
            [lax.dot_general(qs, r[:, cols], _NT, preferred_element_type=F32) for r in krefs], axis=1)
        s = s + jnp.concatenate([bias_ref[2 * lb], bias_ref[2 * lb + 1]], axis=0)
        p = jnp.exp(s - jnp.max(s, axis=-1, keepdims=True))
        l = jnp.sum(p, axis=-1, keepdims=True)
        pb = p.astype(BF16)
        pv = functools.reduce(jnp.add, [
            jnp.dot(pb[:, i * tq:(i + 1) * tq], vrefs[i][:, cols], preferred_element_type=F32)
            for i in range(_ATT_NKB)]) / l
        o = jnp.where(first, pv[:tq], pv[tq:])
        o_ref[:, cols] = (jax.nn.sigmoid(gb_ref[:, cols].astype(F32)) * o).astype(BF16)


def _attn(z3, bias, col):
    b, t, _ = z3.shape
    tq = ATT_TQ
    width = ATT_STEP_HEADS * ATT_DH
    steps = ATT_HEADS // ATT_STEP_HEADS

    def cur(off):
        return pl.BlockSpec((None, tq, width), lambda h, j, bi, o=off // width: (bi, j, o + h))

    def past(off, back):
        return pl.BlockSpec((None, tq, width),
                            lambda h, j, bi, o=off // width: (bi, jnp.maximum(j - back, 0), o + h))

    return pl.pallas_call(
        _attn_kernel,
        grid=(steps, t // tq, b),
        in_specs=[
            cur(col["q_a"]),
            past(col["k_a"], 2), past(col["k_a"], 1), cur(col["k_a"]),
            past(col["v_a"], 2), past(col["v_a"], 1), cur(col["v_a"]),
            cur(col["gt_b"]),
            pl.BlockSpec((None, ATT_STEP_HEADS, tq, _ATT_NKB * tq),
                         lambda h, j, bi: (jnp.minimum(j, _ATT_NKB - 1), h, 0, 0)),
        ],
        out_specs=pl.BlockSpec((None, tq, width), lambda h, j, bi: (bi, j, h)),
        out_shape=jax.ShapeDtypeStruct((b, t, ATT_HEADS * ATT_DH), BF16),
        compiler_params=_cparams(("parallel", "parallel", "parallel")),
        name="band_attn",
    )(z3, z3, z3, z3, z3, z3, z3, z3, bias)


def _outproj_kernel(x_ref, ya_ref, yb_ref, wo_ref, lnm_ref, wr_ref, br_ref,
                    x1_ref, xm_ref, topi_ref, topw_ref, cnt_ref):
    h = (ya_ref[...].astype(F32) + yb_ref[...].astype(F32)).astype(BF16)
    x1 = x_ref[...] + jnp.dot(h, wo_ref[...], preferred_element_type=F32)
    x1_ref[...] = x1
    xm = _rms(x1, lnm_ref[...])
    _store_token_tiles(xm_ref, xm)

    tm = xm.shape[0]
    lane = lax.broadcasted_iota(I32, (tm, LANES), 1)
    lanef = lane.astype(F32)
    logits = _dot_split(xm, wr_ref[...]) + br_ref[...]
    l = jnp.where(lane < N_EXPERTS, logits, -jnp.inf)
    vals, idxs = [], []
    picked = jnp.zeros((tm, LANES), F32)
    for _ in range(TOP_K):
        m = jnp.max(l, axis=-1, keepdims=True)
        idx = jnp.min(jnp.where(l == m, lanef, float(LANES)), axis=-1, keepdims=True)
        vals.append(m)
        idxs.append(idx)
        hit = lanef == idx
        picked = jnp.where(hit, 1.0, picked)
        l = jnp.where(hit, -jnp.inf, l)
    es = [jnp.exp(vv - vals[0]) for vv in vals]
    tot = functools.reduce(jnp.add, es)
    topw = jnp.zeros((tm, LANES), F32)
    topi = jnp.zeros((tm, LANES), F32)
    for kk in range(TOP_K):
        topw = jnp.where(lane == kk, es[kk] / tot, topw)
        topi = jnp.where(lane == kk, idxs[kk], topi)
    topw_ref[...] = topw
    topi_ref[...] = topi.astype(I32)

    @pl.when(pl.program_id(0) == 0)
    def _():
        cnt_ref[...] = jnp.zeros_like(cnt_ref)

    cnt_ref[...] = cnt_ref[...] + jnp.sum(picked, axis=0, keepdims=True)


def _outproj(x2, ya, yb, w_out, ln_moe, wr_pad, br_pad):
    n, d = x2.shape
    tm = OUT_TM
    row = lambda i: (i, 0)
    const = lambda i: (0, 0)
    return pl.pallas_call(
        _outproj_kernel,
        grid=(n // tm,),
        in_specs=[
            pl.BlockSpec((tm, d), row), pl.BlockSpec((tm, d), row), pl.BlockSpec((tm, d), row),
            pl.BlockSpec((d, d), const), pl.BlockSpec((1, d), const),
            pl.BlockSpec((d, LANES), const), pl.BlockSpec((1, LANES), const),
        ],
        out_specs=[
            pl.BlockSpec((tm, d), row), pl.BlockSpec((tm * d // LANES, LANES), row),
            pl.BlockSpec((tm, LANES), row), pl.BlockSpec((tm, LANES), row),
            pl.BlockSpec((1, LANES), const),
        ],
        out_shape=[
            jax.ShapeDtypeStruct((n, d), F32), jax.ShapeDtypeStruct((n * d // LANES, LANES), F32),
            jax.ShapeDtypeStruct((n, LANES), I32), jax.ShapeDtypeStruct((n, LANES), F32),
            jax.ShapeDtypeStruct((1, LANES), F32),
        ],
        compiler_params=_cparams(("arbitrary",)),
        name="out_proj_router",
    )(x2, ya, yb, w_out, ln_moe, wr_pad, br_pad)


def _pos_kernel(topi_ref, start_ref, tri_ref, pos_ref, carry_ref):
    @pl.when(pl.program_id(0) == 0)
    def _():
        carry_ref[...] = start_ref[...]

    ti = topi_ref[...]
    tb = ti.shape[0]
    lane = lax.broadcasted_iota(I32, (tb, LANES), 1)
    sel = [lane == ti[:, kk:kk + 1] for kk in range(TOP_K)]
    oh = functools.reduce(jnp.add, [jnp.where(s, 1.0, 0.0) for s in sel])
    row = carry_ref[...] + jnp.dot(tri_ref[...], oh.astype(BF16), preferred_element_type=F32)
    out = jnp.zeros((tb, LANES), F32)
    for kk in range(TOP_K):
        rk = jnp.sum(jnp.where(sel[kk], row, 0.0), axis=-1, keepdims=True)
        out = jnp.where(lane == kk, rk, out)
    pos_ref[...] = jnp.transpose(out)[:pos_ref.shape[0]].astype(I32)
    carry_ref[...] = carry_ref[...] + jnp.sum(oh, axis=0, keepdims=True)


def _positions(topi, start_rows):
    n = topi.shape[0]
    tb = RANK_TB
    t = np.arange(tb)
    tri = jnp.asarray((t[None, :] < t[:, None]).astype(np.float32), BF16)
    return pl.pallas_call(
        _pos_kernel,
        grid=(n // tb,),
        in_specs=[pl.BlockSpec((tb, LANES), lambda i: (i, 0)),
                  pl.BlockSpec((1, LANES), lambda i: (0, 0)),
                  pl.BlockSpec((tb, tb), lambda i: (0, 0))],
        out_specs=pl.BlockSpec((SUBLANES, tb), lambda i: (0, i)),
        out_shape=jax.ShapeDtypeStruct((SUBLANES, n), I32),
        scratch_shapes=[pltpu.VMEM((1, LANES), F32)],
        compiler_params=_cparams(("arbitrary",)),
        name="route_positions",
    )(topi, start_rows, tri)


def _token_copy(src, src_tok, dst, dst_tok, sem):
    s0 = pl.multiple_of(src_tok * SUBLANES, SUBLANES)
    d0 = pl.multiple_of(dst_tok * SUBLANES, SUBLANES)
    return pltpu.make_async_copy(src.at[pl.ds(s0, SUBLANES)], dst.at[pl.ds(d0, SUBLANES)], sem)


def _tokens_wait(ref, ntok, sem):
    pltpu.make_async_copy(ref.at[pl.ds(0, ntok * SUBLANES)], ref.at[pl.ds(0, ntok * SUBLANES)], sem).wait()


def _dispatch_kernel(tv_ref, pos_hbm, xm_ref, xs_hbm, pos_smem, zeros_ref, sem_pos, sem_rows, sem_fill):
    i = pl.program_id(0)
    nsteps = pl.num_programs(0)
    tokens = xm_ref.shape[0] // SUBLANES
    ch = tokens * TOP_K
    slot = i % 2

    def pos_copy(step, sl):
        return pltpu.make_async_copy(pos_hbm.at[pl.ds(step * ch, ch)], pos_smem.at[pl.ds(sl * ch, ch)],
                                     sem_pos.at[sl])

    @pl.when(i == 0)
    def _():
        pos_copy(0, 0).start()
        tm = zeros_ref.shape[0]
        zeros_ref[...] = jnp.zeros_like(zeros_ref)

        def fill(t):
            return pltpu.make_async_copy(zeros_ref, xs_hbm.at[pl.ds(t * tm, tm)], sem_fill)

        def start(t, carry):
            @pl.when(tv_ref[t] < EXP_TM)
            def _():
                fill(t).start()
            return carry

        def wait(t, carry):
            @pl.when(tv_ref[t] < EXP_TM)
            def _():
                fill(t).wait()
            return carry

        ntiles = xs_hbm.shape[0] // tm
        lax.fori_loop(0, ntiles, start, 0)
        lax.fori_loop(0, ntiles, wait, 0)

    @pl.when(i + 1 < nsteps)
    def _():
        pos_copy(i + 1, 1 - slot).start()

    pos_copy(i, slot).wait()

    def body(t, carry):
        for kk in range(TOP_K):
            _token_copy(xm_ref, t, xs_hbm, pos_smem[slot * ch + kk * tokens + t],
                        sem_rows).start(priority=kk % 2)
        return carry

    lax.fori_loop(0, tokens, body, 0, unroll=DMA_UNROLL)
    _tokens_wait(xs_hbm, ch, sem_rows)


def _dispatch(tile_valid, pos_flat, xm, rows_total):
    tokens = DISPATCH_TOKENS
    n = xm.shape[0] // SUBLANES
    return pl.pallas_call(
        _dispatch_kernel,
        grid_spec=pltpu.PrefetchScalarGridSpec(
            num_scalar_prefetch=1,
            grid=(n // tokens,),
            in_specs=[pl.BlockSpec(memory_space=pl.ANY),
                      pl.BlockSpec((tokens * SUBLANES, LANES), lambda i, tv: (i, 0))],
            out_specs=pl.BlockSpec(memory_space=pl.ANY),
            scratch_shapes=[pltpu.SMEM((2 * tokens * TOP_K,), I32),
                            pltpu.VMEM((EXP_TM * SUBLANES, LANES), F32),
                            pltpu.SemaphoreType.DMA((2,)), pltpu.SemaphoreType.DMA,
                            pltpu.SemaphoreType.DMA],
        ),
        out_shape=jax.ShapeDtypeStruct((rows_total * SUBLANES, LANES), F32),
        compiler_params=_cparams(("arbitrary",)),
        name="dispatch",
    )(tile_valid, pos_flat, xm)


def _deinterleave_matrix():
    j = np.arange(2 * LANES)[:, None]
    c = np.arange(2 * LANES)[None, :]
    sel = np.where(c < LANES, j == 2 * c, j == 2 * (c - LANES) + 1)
    return jnp.asarray(sel.astype(np.float32), BF16)


def _expert_kernel(te_ref, tv_ref, xs_ref, w1_ref, b1g_ref, b1l_ref, w2_ref, b2_ref, sel_ref, ys_ref,
                   w1g_ref, w1l_ref, w2p_ref):
    t = pl.program_id(0)
    nvalid = tv_ref[t]
    new_expert = (t == 0) | (te_ref[t] != te_ref[jnp.maximum(t - 1, 0)])

    @pl.when((nvalid > 0) & new_expert)
    def _():
        sel = sel_ref[...]
        for m in range(w1g_ref.shape[1] // LANES):
            pair = w1_ref[:, 2 * m * LANES:(2 * m + 2) * LANES].astype(BF16)
            split = jnp.dot(pair, sel, preferred_element_type=F32).astype(BF16)
            w1g_ref[:, m * LANES:(m + 1) * LANES] = split[:, :LANES]
            w1l_ref[:, m * LANES:(m + 1) * LANES] = split[:, LANES:]
        w2p_ref[...] = w2_ref[...].astype(BF16)

    @pl.when(nvalid > 0)
    def _():
        tm = xs_ref.shape[0] // SUBLANES
        x = _load_token_tiles(xs_ref, tm, SUBLANES).astype(BF16)
        hg = jnp.dot(x, w1g_ref[...], preferred_element_type=F32) + b1g_ref[...]
        hl = jnp.dot(x, w1l_ref[...], preferred_element_type=F32) + b1l_ref[...]
        glu = jnp.minimum(hg, SWIGLU_LIMIT)
        lin = jnp.clip(hl, -SWIGLU_LIMIT, SWIGLU_LIMIT)
        act = glu * jax.nn.sigmoid(SWIGLU_ALPHA * glu) * (lin + 1.0)
        y = jnp.dot(act.astype(BF16), w2p_ref[...], preferred_element_type=F32) + b2_ref[...]
        _store_token_tiles(ys_ref, y)

    @pl.when(nvalid <= 0)
    def _():
        ys_ref[...] = jnp.zeros_like(ys_ref)


def _experts(tile_expert, tile_valid, xs, w1, b1g, b1l, w2, b2):
    f, d = w2.shape[1:]
    p = xs.shape[0] // SUBLANES
    tm = EXP_TM
    wmap = lambda t, te, tv: (te[t], 0, 0)
    return pl.pallas_call(
        _expert_kernel,
        grid_spec=pltpu.PrefetchScalarGridSpec(
            num_scalar_prefetch=2,
            grid=(p // tm,),
            in_specs=[
                pl.BlockSpec((tm * SUBLANES, LANES), lambda t, te, tv: (t, 0)),
                pl.BlockSpec((None, d, 2 * f), wmap),
                pl.BlockSpec((None, 1, f), wmap), pl.BlockSpec((None, 1, f), wmap),
                pl.BlockSpec((None, f, d), wmap), pl.BlockSpec((None, 1, d), wmap),
                pl.BlockSpec((2 * LANES, 2 * LANES), lambda t, te, tv: (0, 0)),
            ],
            out_specs=pl.BlockSpec((tm * SUBLANES, LANES), lambda t, te, tv: (t, 0)),
            scratch_shapes=[pltpu.VMEM((d, f), BF16), pltpu.VMEM((d, f), BF16),
                            pltpu.VMEM((f, d), BF16)],
        ),
        out_shape=jax.ShapeDtypeStruct((p * SUBLANES, LANES), F32),
        compiler_params=_cparams(("arbitrary",)),
        name="experts",
    )(tile_expert, tile_valid, xs, w1, b1g, b1l, w2, b2, _deinterleave_matrix())


def _ple_kernel(pos_hbm, ys_hbm, topw_ref, x1_ref, p_ref, lnp_ref, wpg_ref, wpp_ref, lnf_ref, o_ref,
                pos_smem, ybuf, sem_pos, sem_rows):
    i = pl.program_id(0)
    nsteps = pl.num_programs(0)
    tb = x1_ref.shape[0]
    ch = tb * TOP_K
    slot = i % 2

    def pos_copy(step, sl):
        return pltpu.make_async_copy(pos_hbm.at[pl.ds(step * ch, ch)], pos_smem.at[pl.ds(sl * ch, ch)],
                                     sem_pos.at[sl])

    def issue_gathers(sl):
        def body(t, carry):
            for kk in range(TOP_K):
                _token_copy(ys_hbm, pos_smem[sl * ch + kk * tb + t], ybuf.at[sl, kk], t,
                            sem_rows.at[sl]).start(priority=kk % 2)
            return carry
        lax.fori_loop(0, tb, body, 0, unroll=DMA_UNROLL)

    @pl.when(i == 0)
    def _():
        pos_copy(0, 0).start()
        pos_copy(0, 0).wait()
        issue_gathers(0)

        @pl.when(nsteps > 1)
        def _():
            pos_copy(1, 1).start()

    @pl.when(i + 1 < nsteps)
    def _():
        pos_copy(i + 1, 1 - slot).wait()
        issue_gathers(1 - slot)

    @pl.when(i + 2 < nsteps)
    def _():
        pos_copy(i + 2, slot).start()

    for kk in range(TOP_K):
        _tokens_wait(ybuf.at[slot, kk], tb, sem_rows.at[slot])

    topw = topw_ref[...]
    groups = x1_ref.shape[1] // LANES
    moe = functools.reduce(jnp.add, [
        topw[:, kk:kk + 1] * _load_token_tiles(ybuf.at[slot, kk], tb, groups) for kk in range(TOP_K)])
    x2 = x1_ref[...] + moe
    gate = jax.nn.sigmoid(jnp.dot(_rms(x2, lnp_ref[...]).astype(BF16), wpg_ref[...],
                                  preferred_element_type=F32))
    proj = jnp.dot(p_ref[...].astype(BF16), wpp_ref[...], preferred_element_type=F32)
    o_ref[...] = _rms(x2 + gate * proj, lnf_ref[...])


def _ple(pos_flat, ys, topw, x1, p2, ln_ple, wpg, wpp, ln_final):
    n, d = x1.shape
    pd = p2.shape[1]
    tb = PLE_TB
    row = lambda i: (i, 0)
    const = lambda i: (0, 0)
    return pl.pallas_call(
        _ple_kernel,
        grid=(n // tb,),
        in_specs=[
            pl.BlockSpec(memory_space=pl.ANY), pl.BlockSpec(memory_space=pl.ANY),
            pl.BlockSpec((tb, LANES), row), pl.BlockSpec((tb, d), row), pl.BlockSpec((tb, pd), row),
            pl.BlockSpec((1, d), const), pl.BlockSpec((d, d), const),
            pl.BlockSpec((pd, d), const), pl.BlockSpec((1, d), const),
        ],
        out_specs=pl.BlockSpec((tb, d), row),
        out_shape=jax.ShapeDtypeStruct((n, d), F32),
        scratch_shapes=[
            pltpu.SMEM((2 * tb * TOP_K,), I32),
            pltpu.VMEM((2, TOP_K, tb * d // LANES, LANES), F32),
            pltpu.SemaphoreType.DMA((2,)),
            pltpu.SemaphoreType.DMA((2,)),
        ],
        compiler_params=_cparams(("arbitrary",)),
        name="combine_ple_final",
    )(pos_flat, ys, topw, x1, p2, ln_ple, wpg, wpp, ln_final)


def _layer(x2, p2, bsz, seq, ln_mix, w_in, w_gk, b_gk, gla_norm, rel_bias, w_out, ln_moe,
           w_router, b_router, w1, b1, w2, b2, ln_ple, w_ple_gate, w_ple_proj, ln_out):
    n, d = x2.shape
    qk, gv, aw = GLA_HEADS * GLA_DK, GLA_HEADS * GLA_DV, ATT_HEADS * ATT_DH
    names = ("q_g", "k_g", "v_g", "gk_low", "r_g", "q_a", "k_a", "v_a", "gt_a", "gt_b")
    widths = (qk, qk, gv, GLA_RANK, gv, aw, aw, aw, d, d)
    src = dict(zip(names, np.cumsum((0,) + widths[:-1]).tolist()))
    wid = dict(zip(names, widths))
    order = [nm for nm in names if nm != "gk_low"]
    col, off = {}, 0
    for nm in order:
        col[nm] = off
        off += wid[nm]
    w_main = jnp.concatenate([w_in[:, src[nm]:src[nm] + wid[nm]] for nm in order], axis=1).astype(BF16)
    w_low = jnp.pad(w_in[:, src["gk_low"]:src["gk_low"] + GLA_RANK],
                    ((0, 0), (0, LANES - GLA_RANK))).astype(BF16)

    z, gk = _in_proj(x2, ln_mix.reshape(1, d), w_main, w_low)
    z3 = z.reshape(bsz, seq, -1)
    gk3 = gk.reshape(bsz, seq, LANES)

    wgk_pad = jnp.pad(w_gk, ((0, LANES - GLA_RANK), (0, 0)))
    ya = _gla(z3, gk3, wgk_pad, b_gk.reshape(1, qk), gla_norm.reshape(1, GLA_DV), col)
    yb = _attn(z3, _attn_bias(rel_bias), col)

    wr_pad = jnp.pad(w_router, ((0, 0), (0, LANES - N_EXPERTS)))
    br_pad = jnp.pad(b_router, (0, LANES - N_EXPERTS)).reshape(1, LANES)
    x1, xm, topi, topw, cnt = _outproj(x2, ya.reshape(n, d), yb.reshape(n, d), w_out.astype(BF16),
                                       ln_moe.reshape(1, d), wr_pad, br_pad)

    counts = cnt[0, :N_EXPERTS].astype(I32)
    ntile = (counts + EXP_TM - 1) // EXP_TM
    tile_end = jnp.cumsum(ntile)
    tile_start = tile_end - ntile
    rows_total = n * TOP_K + N_EXPERTS * EXP_TM
    tiles = jnp.arange(rows_total // EXP_TM, dtype=I32)
    onehot = (tiles[:, None] >= tile_start[None, :]) & (tiles[:, None] < tile_end[None, :])
    te = jnp.sum(jnp.where(onehot, jnp.arange(N_EXPERTS, dtype=I32)[None, :], 0), axis=1)
    tv = jnp.sum(jnp.where(onehot, counts[None, :] - (tiles[:, None] - tile_start[None, :]) * EXP_TM, 0),
                 axis=1)
    te = jnp.where(tiles < tile_end[-1], te, N_EXPERTS - 1).astype(I32)
    tv = jnp.clip(tv, 0, EXP_TM).astype(I32)
    start_rows = jnp.pad((tile_start * EXP_TM).astype(F32), (0, LANES - N_EXPERTS)).reshape(1, LANES)

    pos_t = _positions(topi, start_rows)[:TOP_K]

    def pos_blocks(tokens):
        return pos_t.reshape(TOP_K, n // tokens, tokens).transpose(1, 0, 2).reshape(-1)

    xs = _dispatch(tv, pos_blocks(DISPATCH_TOKENS), xm, rows_total)

    ys = _experts(te, tv, xs, w1,
                  b1[:, 0::2].reshape(N_EXPERTS, 1, -1), b1[:, 1::2].reshape(N_EXPERTS, 1, -1),
                  w2, b2.reshape(N_EXPERTS, 1, d))

    return _ple(pos_blocks(PLE_TB), ys, topw, x1, p2, ln_ple.reshape(1, d), w_ple_gate.astype(BF16),
                w_ple_proj.astype(BF16), ln_out.reshape(1, d))


def kernel(x, p, ln_mix, w_in, w_gk, b_gk, gla_norm, rel_bias, w_out, ln_moe, w_router, b_router,
           w1, b1, w2, b2, ln_ple, w_ple_gate, w_ple_proj, ln_final):
    bsz, seq, d = x.shape
    depth = p.shape[0]
    assert depth == 1, "the final RMSNorm is fused into the last layer's kernel"
    assert d == SUBLANES * LANES, "token-tile layout needs one vreg tile per token"
    assert seq % max(ATT_TQ, GLA_TT) == 0
    assert (bsz * seq) % max(IN_TM, OUT_TM, RANK_TB, DISPATCH_TOKENS, PLE_TB) == 0
    x2 = x.reshape(bsz * seq, d)
    out = _layer(x2, p[0].reshape(bsz * seq, -1), bsz, seq, ln_mix[0], w_in[0], w_gk[0], b_gk[0],
                 gla_norm[0], rel_bias[0], w_out[0], ln_moe[0], w_router[0], b_router[0],
                 w1[0], b1[0], w2[0], b2[0], ln_ple[0], w_ple_gate[0], w_ple_proj[0], ln_final)
    return out.reshape(bsz, seq, d)
```

```python
import functools

import numpy as np
import jax
import jax.numpy as jnp
from jax import lax
from jax.experimental import pallas as pl
from jax.experimental.pallas import tpu as pltpu

F32 = jnp.float32
BF16 = jnp.bfloat16
I32 = jnp.int32

LANES = 128
SUBLANES = 8
CHUNK = 64
GLA_HEADS = 4
GLA_DK = 128
GLA_DV = 256
GLA_RANK = 16
GLA_TAU = 16.0
ATT_HEADS = 16
ATT_DH = 64
ATT_PAST = 8
REL_CLIP = 256
N_EXPERTS = 32
TOP_K = 4
SWIGLU_ALPHA = 1.702
SWIGLU_LIMIT = 7.0
EPS = 1e-6
LOG2E = 1.4426950408889634

VMEM_LIMIT = 48 * 1024 * 1024

IN_TM, IN_TN = 512, 1024
GLA_TT = 128
ATT_TQ = 256
ATT_STEP_HEADS = 16
OUT_TM = 512
RANK_TB = 512
EXP_TM = 512
DISPATCH_TOKENS = 512
PLE_TB = 256
DMA_UNROLL = 8


def _cparams(sem):
    return pltpu.CompilerParams(dimension_semantics=sem, vmem_limit_bytes=VMEM_LIMIT)


def _split_bf16(a):
    hi = a.astype(BF16)
    lo = (a - hi.astype(F32)).astype(BF16)
    return hi, lo


def _dot_split(a, b):
    a_hi, a_lo = _split_bf16(a)
    b_hi, b_lo = _split_bf16(b)
    d = functools.partial(jnp.dot, preferred_element_type=F32)
    return d(a_hi, b_hi) + (d(a_hi, b_lo) + d(a_lo, b_hi))


def _rms(x, gain):
    ms = jnp.mean(x * x, axis=-1, keepdims=True)
    return x * lax.rsqrt(ms + EPS) * gain


def _store_token_tiles(ref, val):
    groups = val.shape[1] // LANES
    for c in range(groups):
        ref[pl.ds(c, val.shape[0], stride=groups), :] = val[:, c * LANES:(c + 1) * LANES]


def _load_token_tiles(ref, rows, groups):
    return jnp.concatenate([ref[pl.ds(c, rows, stride=groups), :] for c in range(groups)], axis=1)


_NT = (((1,), (1,)), ((), ()))
_TN = (((0,), (0,)), ((), ()))


def _in_proj_kernel(x_ref, g_ref, w_ref, wlow_ref, z_ref, gk_ref):
    xn = _rms(x_ref[...], g_ref[...]).astype(BF16)
    gk_ref[...] = jnp.dot(xn, wlow_ref[...], preferred_element_type=F32)
    for j in range(w_ref.shape[1] // IN_TN):
        cols = slice(j * IN_TN, (j + 1) * IN_TN)
        z_ref[:, cols] = jnp.dot(xn, w_ref[:, cols], preferred_element_type=F32).astype(BF16)


def _in_proj(x2, ln, w_main, w_low):
    n, d = x2.shape
    ncol = w_main.shape[1]
    once = pl.Buffered(1)
    return pl.pallas_call(
        _in_proj_kernel,
        grid=(n // IN_TM,),
        in_specs=[
            pl.BlockSpec((IN_TM, d), lambda i: (i, 0)),
            pl.BlockSpec((1, d), lambda i: (0, 0)),
            pl.BlockSpec((d, ncol), lambda i: (0, 0), pipeline_mode=once),
            pl.BlockSpec((d, LANES), lambda i: (0, 0), pipeline_mode=once),
        ],
        out_specs=[
            pl.BlockSpec((IN_TM, ncol), lambda i: (i, 0)),
            pl.BlockSpec((IN_TM, LANES), lambda i: (i, 0)),
        ],
        out_shape=[
            jax.ShapeDtypeStruct((n, ncol), BF16),
            jax.ShapeDtypeStruct((n, LANES), F32),
        ],
        compiler_params=_cparams(("parallel",)),
        name="in_proj",
    )(x2, ln, w_main, w_low)


_GLA_LEVELS = (8, 16, 32)
_GLA_BOT = 8


def _gla_consts(tt):
    t = np.arange(tt)
    same_chunk = (t[:, None] // CHUNK) == (t[None, :] // CHUNK)
    tri = (same_chunk & (t[None, :] <= t[:, None])).astype(np.float32)
    mlev = []
    for h in _GLA_LEVELS:
        blk = (t[:, None] // (2 * h)) == (t[None, :] // (2 * h))
        m = blk & ((t[:, None] % (2 * h)) >= h) & ((t[None, :] % (2 * h)) < h)
        mlev.append(m.astype(np.float32))
    mbot = (((t[None, :] // _GLA_BOT) == (t[:, None] // _GLA_BOT))
            & ((t[None, :] % _GLA_BOT) <= (t[:, None] % _GLA_BOT)))
    spread = (np.arange(_GLA_BOT * GLA_DK)[:, None] // GLA_DK) == (t[None, :] % _GLA_BOT)
    return (jnp.asarray(tri, BF16), jnp.asarray(np.stack(mlev), F32),
            jnp.asarray(mbot.astype(np.float32), F32), jnp.asarray(spread.astype(np.float32), BF16))


def _gla_kernel(q_ref, k_ref, v_ref, r_ref, ga_ref, gk_ref, wgk_ref, bgk_ref, gn_ref,
                tri_ref, mlev_ref, mbot_ref, ones_ref, o_ref, st_ref):
    @pl.when(pl.program_id(1) == 0)
    def _():
        st_ref[...] = jnp.zeros_like(st_ref)

    xg = _dot_split(gk_ref[...], wgk_ref[...]) + bgk_ref[...]
    g = -(jnp.maximum(-xg, 0.0) + jnp.log(1.0 + jnp.exp(-jnp.abs(xg)))) * (1.0 / GLA_TAU)
    g_hi, g_lo = _split_bf16(g)
    tri = tri_ref[...]
    b_all = (jnp.dot(tri, g_hi, preferred_element_type=F32)
             + jnp.dot(tri, g_lo, preferred_element_type=F32))

    for hh in range(GLA_HEADS):
        kc = slice(hh * GLA_DK, (hh + 1) * GLA_DK)
        vc = slice(hh * GLA_DV, (hh + 1) * GLA_DV)
        y, st = _gla_head(q_ref[:, kc], k_ref[:, kc], v_ref[:, vc], b_all[:, kc], st_ref[hh],
                          mlev_ref, mbot_ref, ones_ref[...])
        st_ref[hh] = st
        r = r_ref[:, vc].astype(F32)
        ya = _rms(y, gn_ref[...]) * (r * jax.nn.sigmoid(r))
        o_ref[:, vc] = (jax.nn.sigmoid(ga_ref[:, vc].astype(F32)) * ya).astype(BF16)


def _gla_head(q, k, v, b, st, mlev_ref, mbot_ref, ones):
    tt = q.shape[0]
    q = q.astype(F32) * (GLA_DK ** -0.5)
    k = k.astype(F32)
    b = b * LOG2E

    s_intra = jnp.zeros((tt, tt), F32)
    for li, h in enumerate(_GLA_LEVELS):
        b3 = b.reshape(tt // (2 * h), 2 * h, GLA_DK)
        e = jnp.exp2(-jnp.abs(b3 - b3[:, h - 1:h, :])).reshape(tt, GLA_DK)
        sc = lax.dot_general((q * e).astype(BF16), (k * e).astype(BF16), _NT,
                             preferred_element_type=F32)
        s_intra = s_intra + sc * mlev_ref[li]
    nb = tt // _GLA_BOT
    b3 = b.reshape(nb, _GLA_BOT, GLA_DK)
    q3 = q.reshape(nb, _GLA_BOT, GLA_DK)
    k3 = k.reshape(nb, _GLA_BOT, GLA_DK)
    prods = []
    for s in range(_GLA_BOT):
        e = jnp.exp2(jnp.minimum(b3 - b3[:, s:s + 1, :], 0.0))
        prods.append((q3 * k3[:, s:s + 1, :] * e).reshape(tt, GLA_DK).astype(BF16))
    s_intra = s_intra + jnp.dot(jnp.concatenate(prods, axis=1), ones,
                                preferred_element_type=F32) * mbot_ref[...]
    o_intra = jnp.dot(s_intra.astype(BF16), v, preferred_element_type=F32)

    outs = []
    for c in range(tt // CHUNK):
        lo = c * CHUNK
        bc = b[lo:lo + CHUNK]
        bl = b[lo + CHUNK - 1:lo + CHUNK]
        qe = (q[lo:lo + CHUNK] * jnp.exp2(bc)).astype(BF16)
        outs.append(lax.dot_general(qe, st.astype(BF16), _NT, preferred_element_type=F32))
        kd = (k[lo:lo + CHUNK] * jnp.exp2(bl - bc)).astype(BF16)
        upd = lax.dot_general(v[lo:lo + CHUNK], kd, _TN, preferred_element_type=F32)
        st = st * jnp.exp2(bl) + upd
    return o_intra + jnp.concatenate(outs, axis=0), st


def _gla(z3, gk3, wgk_pad, bgk, gnorm, col):
    b, t, _ = z3.shape
    tt = GLA_TT
    tri, mlev, mbot, ones = _gla_consts(tt)
    qk, gv = GLA_HEADS * GLA_DK, GLA_HEADS * GLA_DV

    def zspec(width, off):
        return pl.BlockSpec((None, tt, width), lambda bi, ti, o=off // width: (bi, ti, o))

    const2 = lambda bi, ti: (0, 0)
    const3 = lambda bi, ti: (0, 0, 0)
    return pl.pallas_call(
        _gla_kernel,
        grid=(b, t // tt),
        in_specs=[
            zspec(qk, col["q_g"]), zspec(qk, col["k_g"]), zspec(gv, col["v_g"]),
            zspec(gv, col["r_g"]), zspec(gv, col["gt_a"]),
            pl.BlockSpec((None, tt, LANES), lambda bi, ti: (bi, ti, 0)),
            pl.BlockSpec((LANES, qk), const2),
            pl.BlockSpec((1, qk), const2),
            pl.BlockSpec((1, GLA_DV), const2),
            pl.BlockSpec((tt, tt), const2),
            pl.BlockSpec((len(_GLA_LEVELS), tt, tt), const3),
            pl.BlockSpec((tt, tt), const2),
            pl.BlockSpec((_GLA_BOT * GLA_DK, tt), const2),
        ],
        out_specs=pl.BlockSpec((None, tt, gv), lambda bi, ti: (bi, ti, 0)),
        out_shape=jax.ShapeDtypeStruct((b, t, gv), BF16),
        scratch_shapes=[pltpu.VMEM((GLA_HEADS, GLA_DV, GLA_DK), F32)],
        compiler_params=_cparams(("parallel", "arbitrary")),
        name="gla",
    )(z3, z3, z3, z3, z3, gk3, wgk_pad, bgk, gnorm, tri, mlev, mbot, ones)


_ATT_NKB = 3


def _attn_bias(rel_bias):
    tq = ATT_TQ
    nk = _ATT_NKB * tq
    back = nk - tq
    nheads = rel_bias.shape[0]
    span = nk + tq - 1
    dist = np.clip(np.arange(span) - (tq - 1), -REL_CLIP, REL_CLIP) + REL_CLIP
    g = rel_bias.astype(F32)[:, dist]
    x = jnp.pad(g[:, ::-1], ((0, 0), (0, 1)))
    tab = jnp.tile(x, (1, tq))[:, :tq * span].reshape(nheads, tq, span)[:, :, tq - 1:tq - 1 + nk]
    t = np.arange(tq)[:, None]
    w = np.arange(nk)[None, :]
    dc = t // CHUNK - np.floor_divide(w - back, CHUNK)
    band = (dc >= 0) & (dc <= ATT_PAST)
    valid = np.stack([band & (w // tq >= _ATT_NKB - 1 - e) for e in range(_ATT_NKB)])
    return jnp.where(jnp.asarray(valid)[:, None], tab[None], -jnp.inf)


def _attn_kernel(q_ref, k0_ref, k1_ref, k2_ref, v0_ref, v1_ref, v2_ref, gb_ref, bias_ref, o_ref):
    tq = q_ref.shape[0]
    first = lax.broadcasted_iota(I32, (1, LANES), 1) < ATT_DH
    scale = jnp.asarray(ATT_DH ** -0.5, BF16)
    krefs = (k0_ref, k1_ref, k2_ref)
    vrefs = (v0_ref, v1_ref, v2_ref)
    for lb in range(q_ref.shape[1] // LANES):
        cols = slice(lb * LANES, (lb + 1) * LANES)
        q = q_ref[:, cols] * scale
        zero = jnp.zeros_like(q)
        qs = jnp.concatenate([jnp.where(first, q, zero), jnp.where(first, zero, q)], axis=0)
        s = jnp.concatenate(
            [lax.dot_general(qs, r[:, cols], _NT, preferred_element_type=F32) for r in krefs], axis=1)
        s = s + jnp.concatenate([bias_ref[2 * lb], bias_ref[2 * lb + 1]], axis=0)
        p = jnp.exp(s - jnp.max(s, axis=-1, keepdims=True))
        l = jnp.sum(p, axis=-1, keepdims=True)
        pb = p.astype(BF16)
        pv = functools.reduce(jnp.add, [
            jnp.dot(pb[:, i * tq:(i + 1) * tq], vrefs[i][:, cols], preferred_element_type=F32)
            for i in range(_ATT_NKB)]) / l
        o = jnp.where(first, pv[:tq], pv[tq:])
        o_ref[:, cols] = (jax.nn.sigmoid(gb_ref[:, cols].astype(F32)) * o).astype(BF16)


def _attn(z3, bias, col):
    b, t, _ = z3.shape
    tq = ATT_TQ
    width = ATT_STEP_HEADS * ATT_DH
    steps = ATT_HEADS // ATT_STEP_HEADS

    def cur(off):
        return pl.BlockSpec((None, tq, width), lambda h, j, bi, o=off // width: (bi, j, o + h))

    def past(off, back):
        return pl.BlockSpec((None, tq, width),
                            lambda h, j, bi, o=off // width: (bi, jnp.maximum(j - back, 0), o + h))

    return pl.pallas_call(
        _attn_kernel,
        grid=(steps, t // tq, b),
        in_specs=[
            cur(col["q_a"]),
            past(col["k_a"], 2), past(col["k_a"], 1), cur(col["k_a"]),
            past(col["v_a"], 2), past(col["v_a"], 1), cur(col["v_a"]),
            cur(col["gt_b"]),
            pl.BlockSpec((None, ATT_STEP_HEADS, tq, _ATT_NKB * tq),
                         lambda h, j, bi: (jnp.minimum(j, _ATT_NKB - 1), h, 0, 0)),
        ],
        out_specs=pl.BlockSpec((None, tq, width), lambda h, j, bi: (bi, j, h)),
        out_shape=jax.ShapeDtypeStruct((b, t, ATT_HEADS * ATT_DH), BF16),
        compiler_params=_cparams(("parallel", "parallel", "parallel")),
        name="band_attn",
    )(z3, z3, z3, z3, z3, z3, z3, z3, bias)


def _outproj_kernel(x_ref, ya_ref, yb_ref, wo_ref, lnm_ref, wr_ref, br_ref,
                    x1_ref, xm_ref, topi_ref, topw_ref, cnt_ref):
    h = (ya_ref[...].astype(F32) + yb_ref[...].astype(F32)).astype(BF16)
    x1 = x_ref[...] + jnp.dot(h, wo_ref[...], preferred_element_type=F32)
    x1_ref[...] = x1
    xm = _rms(x1, lnm_ref[...])
    _store_token_tiles(xm_ref, xm)

    tm = xm.shape[0]
    lane = lax.broadcasted_iota(I32, (tm, LANES), 1)
    lanef = lane.astype(F32)
    logits = _dot_split(xm, wr_ref[...]) + br_ref[...]
    l = jnp.where(lane < N_EXPERTS, logits, -jnp.inf)
    vals, idxs = [], []
    picked = jnp.zeros((tm, LANES), F32)
    for _ in range(TOP_K):
        m = jnp.max(l, axis=-1, keepdims=True)
        idx = jnp.min(jnp.where(l == m, lanef, float(LANES)), axis=-1, keepdims=True)
        vals.append(m)
        idxs.append(idx)
        hit = lanef == idx
        picked = jnp.where(hit, 1.0, picked)
        l = jnp.where(hit, -jnp.inf, l)
    es = [jnp.exp(vv - vals[0]) for vv in vals]
    tot = functools.reduce(jnp.add, es)
    topw = jnp.zeros((tm, LANES), F32)
    topi = jnp.zeros((tm, LANES), F32)
    for kk in range(TOP_K):
        topw = jnp.where(lane == kk, es[kk] / tot, topw)
        topi = jnp.where(lane == kk, idxs[kk], topi)
    topw_ref[...] = topw
    topi_ref[...] = topi.astype(I32)

    @pl.when(pl.program_id(0) == 0)
    def _():
        cnt_ref[...] = jnp.zeros_like(cnt_ref)

    cnt_ref[...] = cnt_ref[...] + jnp.sum(picked, axis=0, keepdims=True)


def _outproj(x2, ya, yb, w_out, ln_moe, wr_pad, br_pad):
    n, d = x2.shape
    tm = OUT_TM
    row = lambda i: (i, 0)
    const = lambda i: (0, 0)
    return pl.pallas_call(
        _outproj_kernel,
        grid=(n // tm,),
        in_specs=[
            pl.BlockSpec((tm, d), row), pl.BlockSpec((tm, d), row), pl.BlockSpec((tm, d), row),
            pl.BlockSpec((d, d), const), pl.BlockSpec((1, d), const),
            pl.BlockSpec((d, LANES), const), pl.BlockSpec((1, LANES), const),
        ],
        out_specs=[
            pl.BlockSpec((tm, d), row), pl.BlockSpec((tm * d // LANES, LANES), row),
            pl.BlockSpec((tm, LANES), row), pl.BlockSpec((tm, LANES), row),
            pl.BlockSpec((1, LANES), const),
        ],
        out_shape=[
            jax.ShapeDtypeStruct((n, d), F32), jax.ShapeDtypeStruct((n * d // LANES, LANES), F32),
            jax.ShapeDtypeStruct((n, LANES), I32), jax.ShapeDtypeStruct((n, LANES), F32),
            jax.ShapeDtypeStruct((1, LANES), F32),
        ],
        compiler_params=_cparams(("arbitrary",)),
        name="out_proj_router",
    )(x2, ya, yb, w_out, ln_moe, wr_pad, br_pad)


def _pos_kernel(topi_ref, start_ref, tri_ref, pos_ref, carry_ref):
    @pl.when(pl.program_id(0) == 0)
    def _():
        carry_ref[...] = start_ref[...]

    ti = topi_ref[...]
    tb = ti.shape[0]
    lane = lax.broadcasted_iota(I32, (tb, LANES), 1)
    sel = [lane == ti[:, kk:kk + 1] for kk in range(TOP_K)]
    oh = functools.reduce(jnp.add, [jnp.where(s, 1.0, 0.0) for s in sel])
    row = carry_ref[...] + jnp.dot(tri_ref[...], oh.astype(BF16), preferred_element_type=F32)
    out = jnp.zeros((tb, LANES), F32)
    for kk in range(TOP_K):
        rk = jnp.sum(jnp.where(sel[kk], row, 0.0), axis=-1, keepdims=True)
        out = jnp.where(lane == kk, rk, out)
    pos_ref[...] = jnp.transpose(out)[:pos_ref.shape[0]].astype(I32)
    carry_ref[...] = carry_ref[...] + jnp.sum(oh, axis=0, keepdims=True)


def _positions(topi, start_rows):
    n = topi.shape[0]
    tb = RANK_TB
    t = np.arange(tb)
    tri = jnp.asarray((t[None, :] < t[:, None]).astype(np.float32), BF16)
    return pl.pallas_call(
        _pos_kernel,
        grid=(n // tb,),
        in_specs=[pl.BlockSpec((tb, LANES), lambda i: (i, 0)),
                  pl.BlockSpec((1, LANES), lambda i: (0, 0)),
                  pl.BlockSpec((tb, tb), lambda i: (0, 0))],
        out_specs=pl.BlockSpec((SUBLANES, tb), lambda i: (0, i)),
        out_shape=jax.ShapeDtypeStruct((SUBLANES, n), I32),
        scratch_shapes=[pltpu.VMEM((1, LANES), F32)],
        compiler_params=_cparams(("arbitrary",)),
        name="route_positions",
    )(topi, start_rows, tri)


def _token_copy(src, src_tok, dst, dst_tok, sem):
    s0 = pl.multiple_of(src_tok * SUBLANES, SUBLANES)
    d0 = pl.multiple_of(dst_tok * SUBLANES, SUBLANES)
    return pltpu.make_async_copy(src.at[pl.ds(s0, SUBLANES)], dst.at[pl.ds(d0, SUBLANES)], sem)


def _tokens_wait(ref, ntok, sem):
    pltpu.make_async_copy(ref.at[pl.ds(0, ntok * SUBLANES)], ref.at[pl.ds(0, ntok * SUBLANES)], sem).wait()


def _dispatch_kernel(tv_ref, pos_hbm, xm_ref, xs_hbm, pos_smem, zeros_ref, sem_pos, sem_rows, sem_fill):
    i = pl.program_id(0)
    nsteps = pl.num_programs(0)
    tokens = xm_ref.shape[0] // SUBLANES
    ch = tokens * TOP_K
    slot = i % 2

    def pos_copy(step, sl):
        return pltpu.make_async_copy(pos_hbm.at[pl.ds(step * ch, ch)], pos_smem.at[pl.ds(sl * ch, ch)],
                                     sem_pos.at[sl])

    @pl.when(i == 0)
    def _():
        pos_copy(0, 0).start()
        tm = zeros_ref.shape[0]
        zeros_ref[...] = jnp.zeros_like(zeros_ref)

        def fill(t):
            return pltpu.make_async_copy(zeros_ref, xs_hbm.at[pl.ds(t * tm, tm)], sem_fill)

        def start(t, carry):
            @pl.when(tv_ref[t] < EXP_TM)
            def _():
                fill(t).start()
            return carry

        def wait(t, carry):
            @pl.when(tv_ref[t] < EXP_TM)
            def _():
                fill(t).wait()
            return carry

        ntiles = xs_hbm.shape[0] // tm
        lax.fori_loop(0, ntiles, start, 0)
        lax.fori_loop(0, ntiles, wait, 0)

    @pl.when(i + 1 < nsteps)
    def _():
        pos_copy(i + 1, 1 - slot).start()

    pos_copy(i, slot).wait()

    def body(t, carry):
        for kk in range(TOP_K):
            _token_copy(xm_ref, t, xs_hbm, pos_smem[slot * ch + kk * tokens + t],
                        sem_rows).start(priority=kk % 2)
        return carry

    lax.fori_loop(0, tokens, body, 0, unroll=DMA_UNROLL)
    _tokens_wait(xs_hbm, ch, sem_rows)


def _dispatch(tile_valid, pos_flat, xm, rows_total):
    tokens = DISPATCH_TOKENS
    n = xm.shape[0] // SUBLANES
    return pl.pallas_call(
        _dispatch_kernel,
        grid_spec=pltpu.PrefetchScalarGridSpec(
            num_scalar_prefetch=1,
            grid=(n // tokens,),
            in_specs=[pl.BlockSpec(memory_space=pl.ANY),
                      pl.BlockSpec((tokens * SUBLANES, LANES), lambda i, tv: (i, 0))],
            out_specs=pl.BlockSpec(memory_space=pl.ANY),
            scratch_shapes=[pltpu.SMEM((2 * tokens * TOP_K,), I32),
                            pltpu.VMEM((EXP_TM * SUBLANES, LANES), F32),
                            pltpu.SemaphoreType.DMA((2,)), pltpu.SemaphoreType.DMA,
                            pltpu.SemaphoreType.DMA],
        ),
        out_shape=jax.ShapeDtypeStruct((rows_total * SUBLANES, LANES), F32),
        compiler_params=_cparams(("arbitrary",)),
        name="dispatch",
    )(tile_valid, pos_flat, xm)


def _deinterleave_matrix():
    j = np.arange(2 * LANES)[:, None]
    c = np.arange(2 * LANES)[None, :]
    sel = np.where(c < LANES, j == 2 * c, j == 2 * (c - LANES) + 1)
    return jnp.asarray(sel.astype(np.float32), BF16)


def _expert_kernel(te_ref, tv_ref, xs_ref, w1_ref, b1g_ref, b1l_ref, w2_ref, b2_ref, sel_ref, ys_ref,
                   w1g_ref, w1l_ref, w2p_ref):
    t = pl.program_id(0)
    nvalid = tv_ref[t]
    new_expert = (t == 0) | (te_ref[t] != te_ref[jnp.maximum(t - 1, 0)])

    @pl.when((nvalid > 0) & new_expert)
    def _():
        sel = sel_ref[...]
        for m in range(w1g_ref.shape[1] // LANES):
            pair = w1_ref[:, 2 * m * LANES:(2 * m + 2) * LANES].astype(BF16)
            split = jnp.dot(pair, sel, preferred_element_type=F32).astype(BF16)
            w1g_ref[:, m * LANES:(m + 1) * LANES] = split[:, :LANES]
            w1l_ref[:, m * LANES:(m + 1) * LANES] = split[:, LANES:]
        w2p_ref[...] = w2_ref[...].astype(BF16)

    @pl.when(nvalid > 0)
    def _():
        tm = xs_ref.shape[0] // SUBLANES
        x = _load_token_tiles(xs_ref, tm, SUBLANES).astype(BF16)
        hg = jnp.dot(x, w1g_ref[...], preferred_element_type=F32) + b1g_ref[...]
        hl = jnp.dot(x, w1l_ref[...], preferred_element_type=F32) + b1l_ref[...]
        glu = jnp.minimum(hg, SWIGLU_LIMIT)
        lin = jnp.clip(hl, -SWIGLU_LIMIT, SWIGLU_LIMIT)
        act = glu * jax.nn.sigmoid(SWIGLU_ALPHA * glu) * (lin + 1.0)
        y = jnp.dot(act.astype(BF16), w2p_ref[...], preferred_element_type=F32) + b2_ref[...]
        _store_token_tiles(ys_ref, y)

    @pl.when(nvalid <= 0)
    def _():
        ys_ref[...] = jnp.zeros_like(ys_ref)


def _experts(tile_expert, tile_valid, xs, w1, b1g, b1l, w2, b2):
    f, d = w2.shape[1:]
    p = xs.shape[0] // SUBLANES
    tm = EXP_TM
    wmap = lambda t, te, tv: (te[t], 0, 0)
    return pl.pallas_call(
        _expert_kernel,
        grid_spec=pltpu.PrefetchScalarGridSpec(
            num_scalar_prefetch=2,
            grid=(p // tm,),
            in_specs=[
                pl.BlockSpec((tm * SUBLANES, LANES), lambda t, te, tv: (t, 0)),
                pl.BlockSpec((None, d, 2 * f), wmap),
                pl.BlockSpec((None, 1, f), wmap), pl.BlockSpec((None, 1, f), wmap),
                pl.BlockSpec((None, f, d), wmap), pl.BlockSpec((None, 1, d), wmap),
                pl.BlockSpec((2 * LANES, 2 * LANES), lambda t, te, tv: (0, 0)),
            ],
            out_specs=pl.BlockSpec((tm * SUBLANES, LANES), lambda t, te, tv: (t, 0)),
            scratch_shapes=[pltpu.VMEM((d, f), BF16), pltpu.VMEM((d, f), BF16),
                            pltpu.VMEM((f, d), BF16)],
        ),
        out_shape=jax.ShapeDtypeStruct((p * SUBLANES, LANES), F32),
        compiler_params=_cparams(("arbitrary",)),
        name="experts",
    )(tile_expert, tile_valid, xs, w1, b1g, b1l, w2, b2, _deinterleave_matrix())


def _ple_kernel(pos_hbm, ys_hbm, topw_ref, x1_ref, p_ref, lnp_ref, wpg_ref, wpp_ref, lnf_ref, o_ref,
                pos0, pos1, ybuf0, ybuf1, sem_pos, sem_rows):
    i = pl.program_id(0)
    nsteps = pl.num_programs(0)
    tb = x1_ref.shape[0] // 2
    ch = tb * TOP_K
    nblocks = 2 * nsteps
    pos_bufs = (pos0, pos1)
    ybufs = (ybuf0, ybuf1)

    def pos_copy(block, par):
        blk = jnp.minimum(block, nblocks - 1)
        return pltpu.make_async_copy(pos_hbm.at[pl.ds(blk * ch, ch)], pos_bufs[par], sem_pos.at[par])

    def issue_gathers(par):
        def body(t, carry):
            for kk in range(TOP_K):
                _token_copy(ys_hbm, pos_bufs[par][kk * tb + t], ybufs[par].at[kk], t,
                            sem_rows.at[par]).start(priority=kk % 2)
            return carry
        lax.fori_loop(0, tb, body, 0, unroll=True)

    def wait_gathers(par):
        for kk in range(TOP_K):
            _tokens_wait(ybufs[par].at[kk], tb, sem_rows.at[par])

    def compute(par):
        rows = slice(par * tb, (par + 1) * tb)
        topw = topw_ref[rows, :]
        groups = x1_ref.shape[1] // LANES
        moe = functools.reduce(jnp.add, [
            topw[:, kk:kk + 1] * _load_token_tiles(ybufs[par].at[kk], tb, groups)
            for kk in range(TOP_K)])
        x2 = x1_ref[rows, :] + moe
        gate = jax.nn.sigmoid(jnp.dot(_rms(x2, lnp_ref[...]).astype(BF16), wpg_ref[...],
                                      preferred_element_type=F32))
        proj = jnp.dot(p_ref[rows, :].astype(BF16), wpp_ref[...], preferred_element_type=F32)
        o_ref[rows, :] = _rms(x2 + gate * proj, lnf_ref[...])

    @pl.when(i == 0)
    def _():
        pos_copy(0, 0).start()
        pos_copy(0, 0).wait()
        issue_gathers(0)
        pos_copy(1, 1).start()

    pos_copy(2 * i + 1, 1).wait()
    wait_gathers(0)
    pos_copy(2 * i + 2, 0).start()
    issue_gathers(1)
    compute(0)
    pos_copy(2 * i + 2, 0).wait()
    wait_gathers(1)
    pos_copy(2 * i + 3, 1).start()
    issue_gathers(0)
    compute(1)

    @pl.when(i == nsteps - 1)
    def _():
        wait_gathers(0)
        pos_copy(0, 1).wait()


def _ple(pos_flat, ys, topw, x1, p2, ln_ple, wpg, wpp, ln_final):
    n, d = x1.shape
    pd = p2.shape[1]
    tb = PLE_TB
    row = lambda i: (i, 0)
    const = lambda i: (0, 0)
    return pl.pallas_call(
        _ple_kernel,
        grid=(n // (2 * tb),),
        in_specs=[
            pl.BlockSpec(memory_space=pl.ANY), pl.BlockSpec(memory_space=pl.ANY),
            pl.BlockSpec((2 * tb, LANES), row), pl.BlockSpec((2 * tb, d), row),
            pl.BlockSpec((2 * tb, pd), row),
            pl.BlockSpec((1, d), const), pl.BlockSpec((d, d), const),
            pl.BlockSpec((pd, d), const), pl.BlockSpec((1, d), const),
        ],
        out_specs=pl.BlockSpec((2 * tb, d), row),
        out_shape=jax.ShapeDtypeStruct((n, d), F32),
        scratch_shapes=[
            pltpu.SMEM((tb * TOP_K,), I32), pltpu.SMEM((tb * TOP_K,), I32),
            pltpu.VMEM((TOP_K, tb * d // LANES, LANES), F32),
            pltpu.VMEM((TOP_K, tb * d // LANES, LANES), F32),
            pltpu.SemaphoreType.DMA((2,)),
            pltpu.SemaphoreType.DMA((2,)),
        ],
        compiler_params=_cparams(("arbitrary",)),
        name="combine_ple_final",
    )(pos_flat, ys, topw, x1, p2, ln_ple, wpg, wpp, ln_final)


def _layer(x2, p2, bsz, seq, ln_mix, w_in, w_gk, b_gk, gla_norm, rel_bias, w_out, ln_moe,
           w_router, b_router, w1, b1, w2, b2, ln_ple, w_ple_gate, w_ple_proj, ln_out):
    n, d = x2.shape
    qk, gv, aw = GLA_HEADS * GLA_DK, GLA_HEADS * GLA_DV, ATT_HEADS * ATT_DH
    names = ("q_g", "k_g", "v_g", "gk_low", "r_g", "q_a", "k_a", "v_a", "gt_a", "gt_b")
    widths = (qk, qk, gv, GLA_RANK, gv, aw, aw, aw, d, d)
    src = dict(zip(names, np.cumsum((0,) + widths[:-1]).tolist()))
    wid = dict(zip(names, widths))
    order = [nm for nm in names if nm != "gk_low"]
    col, off = {}, 0
    for nm in order:
        col[nm] = off
        off += wid[nm]
    w_main = jnp.concatenate([w_in[:, src[nm]:src[nm] + wid[nm]] for nm in order], axis=1).astype(BF16)
    w_low = jnp.pad(w_in[:, src["gk_low"]:src["gk_low"] + GLA_RANK],
                    ((0, 0), (0, LANES - GLA_RANK))).astype(BF16)

    z, gk = _in_proj(x2, ln_mix.reshape(1, d), w_main, w_low)
    z3 = z.reshape(bsz, seq, -1)
    gk3 = gk.reshape(bsz, seq, LANES)

    wgk_pad = jnp.pad(w_gk, ((0, LANES - GLA_RANK), (0, 0)))
    ya = _gla(z3, gk3, wgk_pad, b_gk.reshape(1, qk), gla_norm.reshape(1, GLA_DV), col)
    yb = _attn(z3, _attn_bias(rel_bias), col)

    wr_pad = jnp.pad(w_router, ((0, 0), (0, LANES - N_EXPERTS)))
    br_pad = jnp.pad(b_router, (0, LANES - N_EXPERTS)).reshape(1, LANES)
    x1, xm, topi, topw, cnt = _outproj(x2, ya.reshape(n, d), yb.reshape(n, d), w_out.astype(BF16),
                                       ln_moe.reshape(1, d), wr_pad, br_pad)

    counts = cnt[0, :N_EXPERTS].astype(I32)
    ntile = (counts + EXP_TM - 1) // EXP_TM
    tile_end = jnp.cumsum(ntile)
    tile_start = tile_end - ntile
    rows_total = n * TOP_K + N_EXPERTS * EXP_TM
    tiles = jnp.arange(rows_total // EXP_TM, dtype=I32)
    onehot = (tiles[:, None] >= tile_start[None, :]) & (tiles[:, None] < tile_end[None, :])
    te = jnp.sum(jnp.where(onehot, jnp.arange(N_EXPERTS, dtype=I32)[None, :], 0), axis=1)
    tv = jnp.sum(jnp.where(onehot, counts[None, :] - (tiles[:, None] - tile_start[None, :]) * EXP_TM, 0),
                 axis=1)
    te = jnp.where(tiles < tile_end[-1], te, N_EXPERTS - 1).astype(I32)
    tv = jnp.clip(tv, 0, EXP_TM).astype(I32)
    start_rows = jnp.pad((tile_start * EXP_TM).astype(F32), (0, LANES - N_EXPERTS)).reshape(1, LANES)

    pos_t = _positions(topi, start_rows)[:TOP_K]

    def pos_blocks(tokens):
        return pos_t.reshape(TOP_K, n // tokens, tokens).transpose(1, 0, 2).reshape(-1)

    xs = _dispatch(tv, pos_blocks(DISPATCH_TOKENS), xm, rows_total)

    ys = _experts(te, tv, xs, w1,
                  b1[:, 0::2].reshape(N_EXPERTS, 1, -1), b1[:, 1::2].reshape(N_EXPERTS, 1, -1),
                  w2, b2.reshape(N_EXPERTS, 1, d))

    return _ple(pos_blocks(PLE_TB), ys, topw, x1, p2, ln_ple.reshape(1, d), w_ple_gate.astype(BF16),
                w_ple_proj.astype(BF16), ln_out.reshape(1, d))


def kernel(x, p, ln_mix, w_in, w_gk, b_gk, gla_norm, rel_bias, w_out, ln_moe, w_router, b_router,
           w1, b1, w2, b2, ln_ple, w_ple_gate, w_ple_proj, ln_final):
    bsz, seq, d = x.shape
    depth = p.shape[0]
    assert depth == 1, "the final RMSNorm is fused into the last layer's kernel"
    assert d == SUBLANES * LANES, "token-tile layout needs one vreg tile per token"
    assert seq % max(ATT_TQ, GLA_TT) == 0
    assert (bsz * seq) % max(IN_TM, OUT_TM, RANK_TB, DISPATCH_TOKENS, 2 * PLE_TB) == 0
    x2 = x.reshape(bsz * seq, d)
    out = _layer(x2, p[0].reshape(bsz * seq, -1), bsz, seq, ln_mix[0], w_in[0], w_gk[0], b_gk[0],
                 gla_norm[0], rel_bias[0], w_out[0], ln_moe[0], w_router[0], b_router[0],
                 w1[0], b1[0], w2[0], b2[0], ln_ple[0], w_ple_gate[0], w_ple_proj[0], ln_final)
    return out.reshape(bsz, seq, d)
```

```python
import functools

import numpy as np
import jax
import jax.numpy as jnp
from jax import lax
from jax.experimental import pallas as pl
from jax.experimental.pallas import tpu as pltpu

F32 = jnp.float32
BF16 = jnp.bfloat16
I32 = jnp.int32

LANES = 128
SUBLANES = 8
CHUNK = 64
GLA_HEADS = 4
GLA_DK = 128
GLA_DV = 256
GLA_RANK = 16
GLA_TAU = 16.0
ATT_HEADS = 16
ATT_DH = 64
ATT_PAST = 8
REL_CLIP = 256
N_EXPERTS = 32
TOP_K = 4
SWIGLU_ALPHA = 1.702
SWIGLU_LIMIT = 7.0
EPS = 1e-6
LOG2E = 1.4426950408889634

VMEM_LIMIT = 48 * 1024 * 1024

IN_TM, IN_TN = 512, 1024
GLA_TT = 256
ATT_TQ = 256
ATT_STEP_HEADS = 16
OUT_TM = 512
RANK_TB = 512
EXP_TM = 512
DISPATCH_TOKENS = 512
PLE_TB = 256
DMA_UNROLL = 8


def _cparams(sem):
    return pltpu.CompilerParams(dimension_semantics=sem, vmem_limit_bytes=VMEM_LIMIT)


def _split_bf16(a):
    hi = a.astype(BF16)
    lo = (a - hi.astype(F32)).astype(BF16)
    return hi, lo


def _dot_split(a, b):
    a_hi, a_lo = _split_bf16(a)
    b_hi, b_lo = _split_bf16(b)
    d = functools.partial(jnp.dot, preferred_element_type=F32)
    return d(a_hi, b_hi) + (d(a_hi, b_lo) + d(a_lo, b_hi))


def _rms(x, gain):
    ms = jnp.mean(x * x, axis=-1, keepdims=True)
    return x * lax.rsqrt(ms + EPS) * gain


def _store_token_tiles(ref, val):
    groups = val.shape[1] // LANES
    for c in range(groups):
        ref[pl.ds(c, val.shape[0], stride=groups), :] = val[:, c * LANES:(c + 1) * LANES]


def _load_token_tiles(ref, rows, groups):
    return jnp.concatenate([ref[pl.ds(c, rows, stride=groups), :] for c in range(groups)], axis=1)


_NT = (((1,), (1,)), ((), ()))
_TN = (((0,), (0,)), ((), ()))


def _in_proj_kernel(x_ref, g_ref, w_ref, wlow_ref, z_ref, gk_ref):
    xn = _rms(x_ref[...], g_ref[...]).astype(BF16)
    gk_ref[...] = jnp.dot(xn, wlow_ref[...], preferred_element_type=F32)
    for j in range(w_ref.shape[1] // IN_TN):
        cols = slice(j * IN_TN, (j + 1) * IN_TN)
        z_ref[:, cols] = jnp.dot(xn, w_ref[:, cols], preferred_element_type=F32).astype(BF16)


def _in_proj(x2, ln, w_main, w_low):
    n, d = x2.shape
    ncol = w_main.shape[1]
    once = pl.Buffered(1)
    return pl.pallas_call(
        _in_proj_kernel,
        grid=(n // IN_TM,),
        in_specs=[
            pl.BlockSpec((IN_TM, d), lambda i: (i, 0)),
            pl.BlockSpec((1, d), lambda i: (0, 0)),
            pl.BlockSpec((d, ncol), lambda i: (0, 0), pipeline_mode=once),
            pl.BlockSpec((d, LANES), lambda i: (0, 0), pipeline_mode=once),
        ],
        out_specs=[
            pl.BlockSpec((IN_TM, ncol), lambda i: (i, 0)),
            pl.BlockSpec((IN_TM, LANES), lambda i: (i, 0)),
        ],
        out_shape=[
            jax.ShapeDtypeStruct((n, ncol), BF16),
            jax.ShapeDtypeStruct((n, LANES), F32),
        ],
        compiler_params=_cparams(("parallel",)),
        name="in_proj",
    )(x2, ln, w_main, w_low)


_GLA_LEVELS = (8, 16, 32)
_GLA_BOT = 8


def _gla_consts(tt):
    t = np.arange(tt)
    same_chunk = (t[:, None] // CHUNK) == (t[None, :] // CHUNK)
    tri = (same_chunk & (t[None, :] <= t[:, None])).astype(np.float32)
    mlev = []
    for h in _GLA_LEVELS:
        blk = (t[:, None] // (2 * h)) == (t[None, :] // (2 * h))
        m = blk & ((t[:, None] % (2 * h)) >= h) & ((t[None, :] % (2 * h)) < h)
        mlev.append(m.astype(np.float32))
    mbot = (((t[None, :] // _GLA_BOT) == (t[:, None] // _GLA_BOT))
            & ((t[None, :] % _GLA_BOT) <= (t[:, None] % _GLA_BOT)))
    spread = (np.arange(_GLA_BOT * GLA_DK)[:, None] // GLA_DK) == (t[None, :] % _GLA_BOT)
    return (jnp.asarray(tri, BF16), jnp.asarray(np.stack(mlev), F32),
            jnp.asarray(mbot.astype(np.float32), F32), jnp.asarray(spread.astype(np.float32), BF16))


def _gla_kernel(q_ref, k_ref, v_ref, r_ref, ga_ref, gk_ref, wgk_ref, bgk_ref, gn_ref,
                tri_ref, mlev_ref, mbot_ref, ones_ref, o_ref, st_ref):
    @pl.when(pl.program_id(1) == 0)
    def _():
        st_ref[...] = jnp.zeros_like(st_ref)

    xg = _dot_split(gk_ref[...], wgk_ref[...]) + bgk_ref[...]
    g = -(jnp.maximum(-xg, 0.0) + jnp.log(1.0 + jnp.exp(-jnp.abs(xg)))) * (1.0 / GLA_TAU)
    g_hi, g_lo = _split_bf16(g)
    tri = tri_ref[...]
    b_all = (jnp.dot(tri, g_hi, preferred_element_type=F32)
             + jnp.dot(tri, g_lo, preferred_element_type=F32))

    for hh in range(GLA_HEADS):
        kc = slice(hh * GLA_DK, (hh + 1) * GLA_DK)
        vc = slice(hh * GLA_DV, (hh + 1) * GLA_DV)
        y, st = _gla_head(q_ref[:, kc], k_ref[:, kc], v_ref[:, vc], b_all[:, kc], st_ref[hh],
                          mlev_ref, mbot_ref, ones_ref[...])
        st_ref[hh] = st
        r = r_ref[:, vc].astype(F32)
        ya = _rms(y, gn_ref[...]) * (r * jax.nn.sigmoid(r))
        o_ref[:, vc] = (jax.nn.sigmoid(ga_ref[:, vc].astype(F32)) * ya).astype(BF16)


def _gla_head(q, k, v, b, st, mlev_ref, mbot_ref, ones):
    tt = q.shape[0]
    q = q.astype(F32) * (GLA_DK ** -0.5)
    k = k.astype(F32)
    b = b * LOG2E

    s_intra = jnp.zeros((tt, tt), F32)
    for li, h in enumerate(_GLA_LEVELS):
        b3 = b.reshape(tt // (2 * h), 2 * h, GLA_DK)
        e = jnp.exp2(-jnp.abs(b3 - b3[:, h - 1:h, :])).reshape(tt, GLA_DK)
        sc = lax.dot_general((q * e).astype(BF16), (k * e).astype(BF16), _NT,
                             preferred_element_type=F32)
        s_intra = s_intra + sc * mlev_ref[li]
    nb = tt // _GLA_BOT
    b3 = b.reshape(nb, _GLA_BOT, GLA_DK)
    q3 = q.reshape(nb, _GLA_BOT, GLA_DK)
    k3 = k.reshape(nb, _GLA_BOT, GLA_DK)
    prods = []
    for s in range(_GLA_BOT):
        e = jnp.exp2(jnp.minimum(b3 - b3[:, s:s + 1, :], 0.0))
        prods.append((q3 * k3[:, s:s + 1, :] * e).reshape(tt, GLA_DK).astype(BF16))
    s_intra = s_intra + jnp.dot(jnp.concatenate(prods, axis=1), ones,
                                preferred_element_type=F32) * mbot_ref[...]
    o_intra = jnp.dot(s_intra.astype(BF16), v, preferred_element_type=F32)

    outs = []
    for c in range(tt // CHUNK):
        lo = c * CHUNK
        bc = b[lo:lo + CHUNK]
        bl = b[lo + CHUNK - 1:lo + CHUNK]
        qe = (q[lo:lo + CHUNK] * jnp.exp2(bc)).astype(BF16)
        outs.append(lax.dot_general(qe, st.astype(BF16), _NT, preferred_element_type=F32))
        kd = (k[lo:lo + CHUNK] * jnp.exp2(bl - bc)).astype(BF16)
        upd = lax.dot_general(v[lo:lo + CHUNK], kd, _TN, preferred_element_type=F32)
        st = st * jnp.exp2(bl) + upd
    return o_intra + jnp.concatenate(outs, axis=0), st


def _gla(z3, gk3, wgk_pad, bgk, gnorm, col):
    b, t, _ = z3.shape
    tt = GLA_TT
    tri, mlev, mbot, ones = _gla_consts(tt)
    qk, gv = GLA_HEADS * GLA_DK, GLA_HEADS * GLA_DV

    def zspec(width, off):
        return pl.BlockSpec((None, tt, width), lambda bi, ti, o=off // width: (bi, ti, o))

    const2 = lambda bi, ti: (0, 0)
    const3 = lambda bi, ti: (0, 0, 0)
    return pl.pallas_call(
        _gla_kernel,
        grid=(b, t // tt),
        in_specs=[
            zspec(qk, col["q_g"]), zspec(qk, col["k_g"]), zspec(gv, col["v_g"]),
            zspec(gv, col["r_g"]), zspec(gv, col["gt_a"]),
            pl.BlockSpec((None, tt, LANES), lambda bi, ti: (bi, ti, 0)),
            pl.BlockSpec((LANES, qk), const2),
            pl.BlockSpec((1, qk), const2),
            pl.BlockSpec((1, GLA_DV), const2),
            pl.BlockSpec((tt, tt), const2),
            pl.BlockSpec((len(_GLA_LEVELS), tt, tt), const3),
            pl.BlockSpec((tt, tt), const2),
            pl.BlockSpec((_GLA_BOT * GLA_DK, tt), const2),
        ],
        out_specs=pl.BlockSpec((None, tt, gv), lambda bi, ti: (bi, ti, 0)),
        out_shape=jax.ShapeDtypeStruct((b, t, gv), BF16),
        scratch_shapes=[pltpu.VMEM((GLA_HEADS, GLA_DV, GLA_DK), F32)],
        compiler_params=_cparams(("parallel", "arbitrary")),
        name="gla",
    )(z3, z3, z3, z3, z3, gk3, wgk_pad, bgk, gnorm, tri, mlev, mbot, ones)


_ATT_NKB = 3


def _attn_bias(rel_bias):
    tq = ATT_TQ
    nk = _ATT_NKB * tq
    back = nk - tq
    nheads = rel_bias.shape[0]
    span = nk + tq - 1
    dist = np.clip(np.arange(span) - (tq - 1), -REL_CLIP, REL_CLIP) + REL_CLIP
    g = rel_bias.astype(F32)[:, dist]
    x = jnp.pad(g[:, ::-1], ((0, 0), (0, 1)))
    tab = jnp.tile(x, (1, tq))[:, :tq * span].reshape(nheads, tq, span)[:, :, tq - 1:tq - 1 + nk]
    t = np.arange(tq)[:, None]
    w = np.arange(nk)[None, :]
    dc = t // CHUNK - np.floor_divide(w - back, CHUNK)
    band = (dc >= 0) & (dc <= ATT_PAST)
    valid = np.stack([band & (w // tq >= _ATT_NKB - 1 - e) for e in range(_ATT_NKB)])
    return jnp.where(jnp.asarray(valid)[:, None], tab[None], -jnp.inf)


def _attn_kernel(q_ref, k0_ref, k1_ref, k2_ref, v0_ref, v1_ref, v2_ref, gb_ref, bias_ref, o_ref):
    tq = q_ref.shape[0]
    first = lax.broadcasted_iota(I32, (1, LANES), 1) < ATT_DH
    scale = jnp.asarray(ATT_DH ** -0.5, BF16)
    krefs = (k0_ref, k1_ref, k2_ref)
    vrefs = (v0_ref, v1_ref, v2_ref)
    for lb in range(q_ref.shape[1] // LANES):
        cols = slice(lb * LANES, (lb + 1) * LANES)
        q = q_ref[:, cols] * scale
        zero = jnp.zeros_like(q)
        qs = jnp.concatenate([jnp.where(first, q, zero), jnp.where(first, zero, q)], axis=0)
        s = jnp.concatenate(
            [lax.dot_general(qs, r[:, cols], _NT, preferred_element_type=F32) for r in krefs], axis=1)
        s = s + jnp.concatenate([bias_ref[2 * lb], bias_ref[2 * lb + 1]], axis=0)
        p = jnp.exp(s - jnp.max(s, axis=-1, keepdims=True))
        l = jnp.sum(p, axis=-1, keepdims=True)
        pb = p.astype(BF16)
        pv = functools.reduce(jnp.add, [
            jnp.dot(pb[:, i * tq:(i + 1) * tq], vrefs[i][:, cols], preferred_element_type=F32)
            for i in range(_ATT_NKB)]) / l
        o = jnp.where(first, pv[:tq], pv[tq:])
        o_ref[:, cols] = (jax.nn.sigmoid(gb_ref[:, cols].astype(F32)) * o).astype(BF16)


def _attn(z3, bias, col):
    b, t, _ = z3.shape
    tq = ATT_TQ
    width = ATT_STEP_HEADS * ATT_DH
    steps = ATT_HEADS // ATT_STEP_HEADS

    def cur(off):
        return pl.BlockSpec((None, tq, width), lambda h, j, bi, o=off // width: (bi, j, o + h))

    def past(off, back):
        return pl.BlockSpec((None, tq, width),
                            lambda h, j, bi, o=off // width: (bi, jnp.maximum(j - back, 0), o + h))

    return pl.pallas_call(
        _attn_kernel,
        grid=(steps, t // tq, b),
        in_specs=[
            cur(col["q_a"]),
            past(col["k_a"], 2), past(col["k_a"], 1), cur(col["k_a"]),
            past(col["v_a"], 2), past(col["v_a"], 1), cur(col["v_a"]),
            cur(col["gt_b"]),
            pl.BlockSpec((None, ATT_STEP_HEADS, tq, _ATT_NKB * tq),
                         lambda h, j, bi: (jnp.minimum(j, _ATT_NKB - 1), h, 0, 0)),
        ],
        out_specs=pl.BlockSpec((None, tq, width), lambda h, j, bi: (bi, j, h)),
        out_shape=jax.ShapeDtypeStruct((b, t, ATT_HEADS * ATT_DH), BF16),
        compiler_params=_cparams(("parallel", "parallel", "parallel")),
        name="band_attn",
    )(z3, z3, z3, z3, z3, z3, z3, z3, bias)


def _outproj_kernel(x_ref, ya_ref, yb_ref, wo_ref, lnm_ref, wr_ref, br_ref,
                    x1_ref, xm_ref, topi_ref, topw_ref, cnt_ref):
    h = (ya_ref[...].astype(F32) + yb_ref[...].astype(F32)).astype(BF16)
    x1 = x_ref[...] + jnp.dot(h, wo_ref[...], preferred_element_type=F32)
    x1_ref[...] = x1
    xm = _rms(x1, lnm_ref[...])
    _store_token_tiles(xm_ref, xm)

    tm = xm.shape[0]
    lane = lax.broadcasted_iota(I32, (tm, LANES), 1)
    lanef = lane.astype(F32)
    logits = _dot_split(xm, wr_ref[...]) + br_ref[...]
    l = jnp.where(lane < N_EXPERTS, logits, -jnp.inf)
    vals, idxs = [], []
    picked = jnp.zeros((tm, LANES), F32)
    for _ in range(TOP_K):
        m = jnp.max(l, axis=-1, keepdims=True)
        idx = jnp.min(jnp.where(l == m, lanef, float(LANES)), axis=-1, keepdims=True)
        vals.append(m)
        idxs.append(idx)
        hit = lanef == idx
        picked = jnp.where(hit, 1.0, picked)
        l = jnp.where(hit, -jnp.inf, l)
    es = [jnp.exp(vv - vals[0]) for vv in vals]
    tot = functools.reduce(jnp.add, es)
    topw = jnp.zeros((tm, LANES), F32)
    topi = jnp.zeros((tm, LANES), F32)
    for kk in range(TOP_K):
        topw = jnp.where(lane == kk, es[kk] / tot, topw)
        topi = jnp.where(lane == kk, idxs[kk], topi)
    topw_ref[...] = topw
    topi_ref[...] = topi.astype(I32)

    @pl.when(pl.program_id(0) == 0)
    def _():
        cnt_ref[...] = jnp.zeros_like(cnt_ref)

    cnt_ref[...] = cnt_ref[...] + jnp.sum(picked, axis=0, keepdims=True)


def _outproj(x2, ya, yb, w_out, ln_moe, wr_pad, br_pad):
    n, d = x2.shape
    tm = OUT_TM
    row = lambda i: (i, 0)
    const = lambda i: (0, 0)
    return pl.pallas_call(
        _outproj_kernel,
        grid=(n // tm,),
        in_specs=[
            pl.BlockSpec((tm, d), row), pl.BlockSpec((tm, d), row), pl.BlockSpec((tm, d), row),
            pl.BlockSpec((d, d), const), pl.BlockSpec((1, d), const),
            pl.BlockSpec((d, LANES), const), pl.BlockSpec((1, LANES), const),
        ],
        out_specs=[
            pl.BlockSpec((tm, d), row), pl.BlockSpec((tm * d // LANES, LANES), row),
            pl.BlockSpec((tm, LANES), row), pl.BlockSpec((tm, LANES), row),
            pl.BlockSpec((1, LANES), const),
        ],
        out_shape=[
            jax.ShapeDtypeStruct((n, d), F32), jax.ShapeDtypeStruct((n * d // LANES, LANES), F32),
            jax.ShapeDtypeStruct((n, LANES), I32), jax.ShapeDtypeStruct((n, LANES), F32),
            jax.ShapeDtypeStruct((1, LANES), F32),
        ],
        compiler_params=_cparams(("arbitrary",)),
        name="out_proj_router",
    )(x2, ya, yb, w_out, ln_moe, wr_pad, br_pad)


def _pos_kernel(topi_ref, start_ref, tri_ref, pos_ref, carry_ref):
    @pl.when(pl.program_id(0) == 0)
    def _():
        carry_ref[...] = start_ref[...]

    ti = topi_ref[...]
    tb = ti.shape[0]
    lane = lax.broadcasted_iota(I32, (tb, LANES), 1)
    sel = [lane == ti[:, kk:kk + 1] for kk in range(TOP_K)]
    oh = functools.reduce(jnp.add, [jnp.where(s, 1.0, 0.0) for s in sel])
    row = carry_ref[...] + jnp.dot(tri_ref[...], oh.astype(BF16), preferred_element_type=F32)
    out = jnp.zeros((tb, LANES), F32)
    for kk in range(TOP_K):
        rk = jnp.sum(jnp.where(sel[kk], row, 0.0), axis=-1, keepdims=True)
        out = jnp.where(lane == kk, rk, out)
    pos_ref[...] = jnp.transpose(out)[:pos_ref.shape[0]].astype(I32)
    carry_ref[...] = carry_ref[...] + jnp.sum(oh, axis=0, keepdims=True)


def _positions(topi, start_rows):
    n = topi.shape[0]
    tb = RANK_TB
    t = np.arange(tb)
    tri = jnp.asarray((t[None, :] < t[:, None]).astype(np.float32), BF16)
    return pl.pallas_call(
        _pos_kernel,
        grid=(n // tb,),
        in_specs=[pl.BlockSpec((tb, LANES), lambda i: (i, 0)),
                  pl.BlockSpec((1, LANES), lambda i: (0, 0)),
                  pl.BlockSpec((tb, tb), lambda i: (0, 0))],
        out_specs=pl.BlockSpec((SUBLANES, tb), lambda i: (0, i)),
        out_shape=jax.ShapeDtypeStruct((SUBLANES, n), I32),
        scratch_shapes=[pltpu.VMEM((1, LANES), F32)],
        compiler_params=_cparams(("arbitrary",)),
        name="route_positions",
    )(topi, start_rows, tri)


def _token_copy(src, src_tok, dst, dst_tok, sem):
    s0 = pl.multiple_of(src_tok * SUBLANES, SUBLANES)
    d0 = pl.multiple_of(dst_tok * SUBLANES, SUBLANES)
    return pltpu.make_async_copy(src.at[pl.ds(s0, SUBLANES)], dst.at[pl.ds(d0, SUBLANES)], sem)


def _tokens_wait(ref, ntok, sem):
    pltpu.make_async_copy(ref.at[pl.ds(0, ntok * SUBLANES)], ref.at[pl.ds(0, ntok * SUBLANES)], sem).wait()


def _dispatch_kernel(tv_ref, pos_hbm, xm_ref, xs_hbm, pos_smem, zeros_ref, sem_pos, sem_rows, sem_fill):
    i = pl.program_id(0)
    nsteps = pl.num_programs(0)
    tokens = xm_ref.shape[0] // SUBLANES
    ch = tokens * TOP_K
    slot = i % 2

    def pos_copy(step, sl):
        return pltpu.make_async_copy(pos_hbm.at[pl.ds(step * ch, ch)], pos_smem.at[pl.ds(sl * ch, ch)],
                                     sem_pos.at[sl])

    @pl.when(i == 0)
    def _():
        pos_copy(0, 0).start()
        tm = zeros_ref.shape[0]
        zeros_ref[...] = jnp.zeros_like(zeros_ref)

        def fill(t):
            return pltpu.make_async_copy(zeros_ref, xs_hbm.at[pl.ds(t * tm, tm)], sem_fill)

        def start(t, carry):
            @pl.when(tv_ref[t] < EXP_TM)
            def _():
                fill(t).start()
            return carry

        def wait(t, carry):
            @pl.when(tv_ref[t] < EXP_TM)
            def _():
                fill(t).wait()
            return carry

        ntiles = xs_hbm.shape[0] // tm
        lax.fori_loop(0, ntiles, start, 0)
        lax.fori_loop(0, ntiles, wait, 0)

    @pl.when(i + 1 < nsteps)
    def _():
        pos_copy(i + 1, 1 - slot).start()

    pos_copy(i, slot).wait()

    def body(t, carry):
        for kk in range(TOP_K):
            _token_copy(xm_ref, t, xs_hbm, pos_smem[slot * ch + kk * tokens + t],
                        sem_rows).start(priority=kk % 2)
        return carry

    lax.fori_loop(0, tokens, body, 0, unroll=DMA_UNROLL)
    _tokens_wait(xs_hbm, ch, sem_rows)


def _dispatch(tile_valid, pos_flat, xm, rows_total):
    tokens = DISPATCH_TOKENS
    n = xm.shape[0] // SUBLANES
    return pl.pallas_call(
        _dispatch_kernel,
        grid_spec=pltpu.PrefetchScalarGridSpec(
            num_scalar_prefetch=1,
            grid=(n // tokens,),
            in_specs=[pl.BlockSpec(memory_space=pl.ANY),
                      pl.BlockSpec((tokens * SUBLANES, LANES), lambda i, tv: (i, 0))],
            out_specs=pl.BlockSpec(memory_space=pl.ANY),
            scratch_shapes=[pltpu.SMEM((2 * tokens * TOP_K,), I32),
                            pltpu.VMEM((EXP_TM * SUBLANES, LANES), F32),
                            pltpu.SemaphoreType.DMA((2,)), pltpu.SemaphoreType.DMA,
                            pltpu.SemaphoreType.DMA],
        ),
        out_shape=jax.ShapeDtypeStruct((rows_total * SUBLANES, LANES), F32),
        compiler_params=_cparams(("arbitrary",)),
        name="dispatch",
    )(tile_valid, pos_flat, xm)


def _deinterleave_matrix():
    j = np.arange(2 * LANES)[:, None]
    c = np.arange(2 * LANES)[None, :]
    sel = np.where(c < LANES, j == 2 * c, j == 2 * (c - LANES) + 1)
    return jnp.asarray(sel.astype(np.float32), BF16)


def _expert_kernel(te_ref, tv_ref, slot_ref, next_ref, xs_ref, w1_hbm, b1g_ref, b1l_ref, w2_hbm, b2_ref,
                   sel_ref, ys_ref, w1g_ref, w1l_ref, w2p_ref, w1buf, w2buf, sem_w1, sem_w2):
    t = pl.program_id(0)
    nvalid = tv_ref[t]
    new_expert = (t == 0) | (te_ref[t] != te_ref[jnp.maximum(t - 1, 0)])

    def weight_copies(expert, sl):
        return (pltpu.make_async_copy(w1_hbm.at[expert], w1buf.at[sl], sem_w1.at[sl]),
                pltpu.make_async_copy(w2_hbm.at[expert], w2buf.at[sl], sem_w2.at[sl]))

    @pl.when((nvalid > 0) & new_expert)
    def _():
        sl = slot_ref[t]

        @pl.when(t == 0)
        def _():
            for cp in weight_copies(te_ref[t], sl):
                cp.start()

        for cp in weight_copies(te_ref[t], sl):
            cp.wait()

        @pl.when(next_ref[t] >= 0)
        def _():
            for cp in weight_copies(next_ref[t], 1 - sl):
                cp.start()

        sel = sel_ref[...]
        for m in range(w1g_ref.shape[1] // LANES):
            pair = w1buf[sl, :, 2 * m * LANES:(2 * m + 2) * LANES].astype(BF16)
            split = jnp.dot(pair, sel, preferred_element_type=F32).astype(BF16)
            w1g_ref[:, m * LANES:(m + 1) * LANES] = split[:, :LANES]
            w1l_ref[:, m * LANES:(m + 1) * LANES] = split[:, LANES:]
        w2p_ref[...] = w2buf[sl].astype(BF16)

    @pl.when(nvalid > 0)
    def _():
        tm = xs_ref.shape[0] // SUBLANES
        x = _load_token_tiles(xs_ref, tm, SUBLANES).astype(BF16)
        hg = jnp.dot(x, w1g_ref[...], preferred_element_type=F32) + b1g_ref[...]
        hl = jnp.dot(x, w1l_ref[...], preferred_element_type=F32) + b1l_ref[...]
        glu = jnp.minimum(hg, SWIGLU_LIMIT)
        lin = jnp.clip(hl, -SWIGLU_LIMIT, SWIGLU_LIMIT)
        act = glu * jax.nn.sigmoid(SWIGLU_ALPHA * glu) * (lin + 1.0)
        y = jnp.dot(act.astype(BF16), w2p_ref[...], preferred_element_type=F32) + b2_ref[...]
        _store_token_tiles(ys_ref, y)

    @pl.when(nvalid <= 0)
    def _():
        ys_ref[...] = jnp.zeros_like(ys_ref)


def _experts(tile_expert, tile_valid, tile_slot, tile_next, xs, w1, b1g, b1l, w2, b2):
    f, d = w2.shape[1:]
    p = xs.shape[0] // SUBLANES
    tm = EXP_TM
    wmap = lambda t, te, tv, ts, tn: (te[t], 0, 0)
    tile = lambda t, te, tv, ts, tn: (t, 0)
    return pl.pallas_call(
        _expert_kernel,
        grid_spec=pltpu.PrefetchScalarGridSpec(
            num_scalar_prefetch=4,
            grid=(p // tm,),
            in_specs=[
                pl.BlockSpec((tm * SUBLANES, LANES), tile),
                pl.BlockSpec(memory_space=pl.ANY),
                pl.BlockSpec((None, 1, f), wmap), pl.BlockSpec((None, 1, f), wmap),
                pl.BlockSpec(memory_space=pl.ANY), pl.BlockSpec((None, 1, d), wmap),
                pl.BlockSpec((2 * LANES, 2 * LANES), lambda t, te, tv, ts, tn: (0, 0)),
            ],
            out_specs=pl.BlockSpec((tm * SUBLANES, LANES), tile),
            scratch_shapes=[pltpu.VMEM((d, f), BF16), pltpu.VMEM((d, f), BF16),
                            pltpu.VMEM((f, d), BF16),
                            pltpu.VMEM((2, d, 2 * f), F32), pltpu.VMEM((2, f, d), F32),
                            pltpu.SemaphoreType.DMA((2,)), pltpu.SemaphoreType.DMA((2,))],
        ),
        out_shape=jax.ShapeDtypeStruct((p * SUBLANES, LANES), F32),
        compiler_params=_cparams(("arbitrary",)),
        name="experts",
    )(tile_expert, tile_valid, tile_slot, tile_next, xs, w1, b1g, b1l, w2, b2, _deinterleave_matrix())


def _ple_kernel(pos_hbm, ys_hbm, topw_ref, x1_ref, p_ref, lnp_ref, wpg_ref, wpp_ref, lnf_ref, o_ref,
                pos0, pos1, ybuf0, ybuf1, sem_pos, sem_rows):
    i = pl.program_id(0)
    nsteps = pl.num_programs(0)
    tb = x1_ref.shape[0] // 2
    ch = tb * TOP_K
    nblocks = 2 * nsteps
    pos_bufs = (pos0, pos1)
    ybufs = (ybuf0, ybuf1)

    def pos_copy(block, par):
        blk = jnp.minimum(block, nblocks - 1)
        return pltpu.make_async_copy(pos_hbm.at[pl.ds(blk * ch, ch)], pos_bufs[par], sem_pos.at[par])

    def issue_gathers(par):
        def body(t, carry):
            for kk in range(TOP_K):
                _token_copy(ys_hbm, pos_bufs[par][kk * tb + t], ybufs[par].at[kk], t,
                            sem_rows.at[par]).start(priority=kk % 2)
            return carry
        lax.fori_loop(0, tb, body, 0, unroll=True)

    def wait_gathers(par):
        for kk in range(TOP_K):
            _tokens_wait(ybufs[par].at[kk], tb, sem_rows.at[par])

    def compute(par):
        rows = slice(par * tb, (par + 1) * tb)
        topw = topw_ref[rows, :]
        groups = x1_ref.shape[1] // LANES
        moe = functools.reduce(jnp.add, [
            topw[:, kk:kk + 1] * _load_token_tiles(ybufs[par].at[kk], tb, groups)
            for kk in range(TOP_K)])
        x2 = x1_ref[rows, :] + moe
        gate = jax.nn.sigmoid(jnp.dot(_rms(x2, lnp_ref[...]).astype(BF16), wpg_ref[...],
                                      preferred_element_type=F32))
        proj = jnp.dot(p_ref[rows, :].astype(BF16), wpp_ref[...], preferred_element_type=F32)
        o_ref[rows, :] = _rms(x2 + gate * proj, lnf_ref[...])

    @pl.when(i == 0)
    def _():
        pos_copy(0, 0).start()
        pos_copy(0, 0).wait()
        issue_gathers(0)
        pos_copy(1, 1).start()

    pos_copy(2 * i + 1, 1).wait()
    wait_gathers(0)
    pos_copy(2 * i + 2, 0).start()
    issue_gathers(1)
    compute(0)
    pos_copy(2 * i + 2, 0).wait()
    wait_gathers(1)
    pos_copy(2 * i + 3, 1).start()
    issue_gathers(0)
    compute(1)

    @pl.when(i == nsteps - 1)
    def _():
        wait_gathers(0)
        pos_copy(0, 1).wait()


def _ple(pos_flat, ys, topw, x1, p2, ln_ple, wpg, wpp, ln_final):
    n, d = x1.shape
    pd = p2.shape[1]
    tb = PLE_TB
    row = lambda i: (i, 0)
    const = lambda i: (0, 0)
    return pl.pallas_call(
        _ple_kernel,
        grid=(n // (2 * tb),),
        in_specs=[
            pl.BlockSpec(memory_space=pl.ANY), pl.BlockSpec(memory_space=pl.ANY),
            pl.BlockSpec((2 * tb, LANES), row), pl.BlockSpec((2 * tb, d), row),
            pl.BlockSpec((2 * tb, pd), row),
            pl.BlockSpec((1, d), const), pl.BlockSpec((d, d), const),
            pl.BlockSpec((pd, d), const), pl.BlockSpec((1, d), const),
        ],
        out_specs=pl.BlockSpec((2 * tb, d), row),
        out_shape=jax.ShapeDtypeStruct((n, d), F32),
        scratch_shapes=[
            pltpu.SMEM((tb * TOP_K,), I32), pltpu.SMEM((tb * TOP_K,), I32),
            pltpu.VMEM((TOP_K, tb * d // LANES, LANES), F32),
            pltpu.VMEM((TOP_K, tb * d // LANES, LANES), F32),
            pltpu.SemaphoreType.DMA((2,)),
            pltpu.SemaphoreType.DMA((2,)),
        ],
        compiler_params=_cparams(("arbitrary",)),
        name="combine_ple_final",
    )(pos_flat, ys, topw, x1, p2, ln_ple, wpg, wpp, ln_final)


def _layer(x2, p2, bsz, seq, ln_mix, w_in, w_gk, b_gk, gla_norm, rel_bias, w_out, ln_moe,
           w_router, b_router, w1, b1, w2, b2, ln_ple, w_ple_gate, w_ple_proj, ln_out):
    n, d = x2.shape
    qk, gv, aw = GLA_HEADS * GLA_DK, GLA_HEADS * GLA_DV, ATT_HEADS * ATT_DH
    names = ("q_g", "k_g", "v_g", "gk_low", "r_g", "q_a", "k_a", "v_a", "gt_a", "gt_b")
    widths = (qk, qk, gv, GLA_RANK, gv, aw, aw, aw, d, d)
    src = dict(zip(names, np.cumsum((0,) + widths[:-1]).tolist()))
    wid = dict(zip(names, widths))
    order = [nm for nm in names if nm != "gk_low"]
    col, off = {}, 0
    for nm in order:
        col[nm] = off
        off += wid[nm]
    w_main = jnp.concatenate([w_in[:, src[nm]:src[nm] + wid[nm]] for nm in order], axis=1).astype(BF16)
    w_low = jnp.pad(w_in[:, src["gk_low"]:src["gk_low"] + GLA_RANK],
                    ((0, 0), (0, LANES - GLA_RANK))).astype(BF16)

    z, gk = _in_proj(x2, ln_mix.reshape(1, d), w_main, w_low)
    z3 = z.reshape(bsz, seq, -1)
    gk3 = gk.reshape(bsz, seq, LANES)

    wgk_pad = jnp.pad(w_gk, ((0, LANES - GLA_RANK), (0, 0)))
    ya = _gla(z3, gk3, wgk_pad, b_gk.reshape(1, qk), gla_norm.reshape(1, GLA_DV), col)
    yb = _attn(z3, _attn_bias(rel_bias), col)

    wr_pad = jnp.pad(w_router, ((0, 0), (0, LANES - N_EXPERTS)))
    br_pad = jnp.pad(b_router, (0, LANES - N_EXPERTS)).reshape(1, LANES)
    x1, xm, topi, topw, cnt = _outproj(x2, ya.reshape(n, d), yb.reshape(n, d), w_out.astype(BF16),
                                       ln_moe.reshape(1, d), wr_pad, br_pad)

    counts = cnt[0, :N_EXPERTS].astype(I32)
    ntile = (counts + EXP_TM - 1) // EXP_TM
    tile_end = jnp.cumsum(ntile)
    tile_start = tile_end - ntile
    rows_total = n * TOP_K + N_EXPERTS * EXP_TM
    tiles = jnp.arange(rows_total // EXP_TM, dtype=I32)
    onehot = (tiles[:, None] >= tile_start[None, :]) & (tiles[:, None] < tile_end[None, :])
    te = jnp.sum(jnp.where(onehot, jnp.arange(N_EXPERTS, dtype=I32)[None, :], 0), axis=1)
    tv = jnp.sum(jnp.where(onehot, counts[None, :] - (tiles[:, None] - tile_start[None, :]) * EXP_TM, 0),
                 axis=1)
    te = jnp.where(tiles < tile_end[-1], te, N_EXPERTS - 1).astype(I32)
    tv = jnp.clip(tv, 0, EXP_TM).astype(I32)
    start_rows = jnp.pad((tile_start * EXP_TM).astype(F32), (0, LANES - N_EXPERTS)).reshape(1, LANES)
    active = ntile > 0
    eid = jnp.arange(N_EXPERTS, dtype=I32)
    slot_e = (jnp.cumsum(active.astype(I32)) - 1) % 2
    later = (eid[None, :] > eid[:, None]) & active[None, :]
    next_e = jnp.min(jnp.where(later, eid[None, :], N_EXPERTS), axis=1)
    next_e = jnp.where(next_e < N_EXPERTS, next_e, -1)
    tslot = slot_e[te].astype(I32)
    tnext = next_e[te].astype(I32)

    pos_t = _positions(topi, start_rows)[:TOP_K]

    def pos_blocks(tokens):
        return pos_t.reshape(TOP_K, n // tokens, tokens).transpose(1, 0, 2).reshape(-1)

    xs = _dispatch(tv, pos_blocks(DISPATCH_TOKENS), xm, rows_total)

    ys = _experts(te, tv, tslot, tnext, xs, w1,
                  b1[:, 0::2].reshape(N_EXPERTS, 1, -1), b1[:, 1::2].reshape(N_EXPERTS, 1, -1),
                  w2, b2.reshape(N_EXPERTS, 1, d))

    return _ple(pos_blocks(PLE_TB), ys, topw, x1, p2, ln_ple.reshape(1, d), w_ple_gate.astype(BF16),
                w_ple_proj.astype(BF16), ln_out.reshape(1, d))


def kernel(x, p, ln_mix, w_in, w_gk, b_gk, gla_norm, rel_bias, w_out, ln_moe, w_router, b_router,
           w1, b1, w2, b2, ln_ple, w_ple_gate, w_ple_proj, ln_final):
    bsz, seq, d = x.shape
    depth = p.shape[0]
    assert depth == 1, "the final RMSNorm is fused into the last layer's kernel"
    assert d == SUBLANES * LANES, "token-tile layout needs one vreg tile per token"
    assert seq % max(ATT_TQ, GLA_TT) == 0
    assert (bsz * seq) % max(IN_TM, OUT_TM, RANK_TB, DISPATCH_TOKENS, 2 * PLE_TB) == 0
    x2 = x.reshape(bsz * seq, d)
    out = _layer(x2, p[0].reshape(bsz * seq, -1), bsz, seq, ln_mix[0], w_in[0], w_gk[0], b_gk[0],
                 gla_norm[0], rel_bias[0], w_out[0], ln_moe[0], w_router[0], b_router[0],
                 w1[0], b1[0], w2[0], b2[0], ln_ple[0], w_ple_gate[0], w_ple_proj[0], ln_final)
    return out.reshape(bsz, seq, d)
```

```python
import functools

import numpy as np
import jax
import jax.numpy as jnp
from jax import lax
from jax.experimental import pallas as pl
from jax.experimental.pallas import tpu as pltpu

F32 = jnp.float32
BF16 = jnp.bfloat16
I32 = jnp.int32

LANES = 128
SUBLANES = 8
CHUNK = 64
GLA_HEADS = 4
GLA_DK = 128
GLA_DV = 256
GLA_RANK = 16
GLA_TAU = 16.0
ATT_HEADS = 16
ATT_DH = 64
ATT_PAST = 8
REL_CLIP = 256
N_EXPERTS = 32
TOP_K = 4
SWIGLU_ALPHA = 1.702
SWIGLU_LIMIT = 7.0
EPS = 1e-6
LOG2E = 1.4426950408889634

VMEM_LIMIT = 48 * 1024 * 1024

IN_TM, IN_TN = 512, 1024
GLA_TT = 256
ATT_TQ = 256
ATT_STEP_HEADS = 16
OUT_TM = 512
RANK_TB = 512
EXP_TM = 512
DISPATCH_TOKENS = 512
PLE_TB = 256
DMA_UNROLL = 8


def _cparams(sem):
    return pltpu.CompilerParams(dimension_semantics=sem, vmem_limit_bytes=VMEM_LIMIT)


def _split_bf16(a):
    hi = a.astype(BF16)
    lo = (a - hi.astype(F32)).astype(BF16)
    return hi, lo


def _dot_split(a, b):
    a_hi, a_lo = _split_bf16(a)
    b_hi, b_lo = _split_bf16(b)
    d = functools.partial(jnp.dot, preferred_element_type=F32)
    return d(a_hi, b_hi) + (d(a_hi, b_lo) + d(a_lo, b_hi))


def _rms(x, gain):
    ms = jnp.mean(x * x, axis=-1, keepdims=True)
    return x * lax.rsqrt(ms + EPS) * gain


def _store_token_tiles(ref, val):
    groups = val.shape[1] // LANES
    for c in range(groups):
        ref[pl.ds(c, val.shape[0], stride=groups), :] = val[:, c * LANES:(c + 1) * LANES]


def _load_token_tiles(ref, rows, groups):
    return jnp.concatenate([ref[pl.ds(c, rows, stride=groups), :] for c in range(groups)], axis=1)


_NT = (((1,), (1,)), ((), ()))
_TN = (((0,), (0,)), ((), ()))


def _in_proj_kernel(x_ref, g_ref, wa_ref, wb_ref, wlow_ref, z_ref, gk_ref):
    xn = _rms(x_ref[...], g_ref[...]).astype(BF16)
    gk_ref[...] = jnp.dot(xn, wlow_ref[...], preferred_element_type=F32)
    off = 0
    for w_ref in (wa_ref, wb_ref):
        for j in range(w_ref.shape[1] // IN_TN):
            z_ref[:, off:off + IN_TN] = jnp.dot(xn, w_ref[:, j * IN_TN:(j + 1) * IN_TN],
                                                preferred_element_type=F32).astype(BF16)
            off += IN_TN


def _in_proj(x2, ln, w_a, w_b, w_low):
    n, d = x2.shape
    ncol = w_a.shape[1] + w_b.shape[1]
    once = pl.Buffered(1)
    return pl.pallas_call(
        _in_proj_kernel,
        grid=(n // IN_TM,),
        in_specs=[
            pl.BlockSpec((IN_TM, d), lambda i: (i, 0)),
            pl.BlockSpec((1, d), lambda i: (0, 0)),
            pl.BlockSpec(w_a.shape, lambda i: (0, 0), pipeline_mode=once),
            pl.BlockSpec(w_b.shape, lambda i: (0, 0), pipeline_mode=once),
            pl.BlockSpec((d, LANES), lambda i: (0, 0), pipeline_mode=once),
        ],
        out_specs=[
            pl.BlockSpec((IN_TM, ncol), lambda i: (i, 0)),
            pl.BlockSpec((IN_TM, LANES), lambda i: (i, 0)),
        ],
        out_shape=[
            jax.ShapeDtypeStruct((n, ncol), BF16),
            jax.ShapeDtypeStruct((n, LANES), F32),
        ],
        compiler_params=_cparams(("parallel",)),
        name="in_proj",
    )(x2, ln, w_a, w_b, w_low)


_GLA_LEVELS = (8, 16, 32)
_GLA_BOT = 8


def _gla_consts(tt):
    t = np.arange(tt)
    same_chunk = (t[:, None] // CHUNK) == (t[None, :] // CHUNK)
    tri = (same_chunk & (t[None, :] <= t[:, None])).astype(np.float32)
    mlev = []
    for h in _GLA_LEVELS:
        blk = (t[:, None] // (2 * h)) == (t[None, :] // (2 * h))
        m = blk & ((t[:, None] % (2 * h)) >= h) & ((t[None, :] % (2 * h)) < h)
        mlev.append(m.astype(np.float32))
    mbot = (((t[None, :] // _GLA_BOT) == (t[:, None] // _GLA_BOT))
            & ((t[None, :] % _GLA_BOT) <= (t[:, None] % _GLA_BOT)))
    spread = (np.arange(_GLA_BOT * GLA_DK)[:, None] // GLA_DK) == (t[None, :] % _GLA_BOT)
    return (jnp.asarray(tri, BF16), jnp.asarray(np.stack(mlev), F32),
            jnp.asarray(mbot.astype(np.float32), F32), jnp.asarray(spread.astype(np.float32), BF16))


def _gla_kernel(q_ref, k_ref, v_ref, r_ref, ga_ref, gk_ref, wgk_ref, bgk_ref, gn_ref,
                tri_ref, mlev_ref, mbot_ref, ones_ref, o_ref, st_ref):
    @pl.when(pl.program_id(1) == 0)
    def _():
        st_ref[...] = jnp.zeros_like(st_ref)

    xg = _dot_split(gk_ref[...], wgk_ref[...]) + bgk_ref[...]
    g = -(jnp.maximum(-xg, 0.0) + jnp.log(1.0 + jnp.exp(-jnp.abs(xg)))) * (1.0 / GLA_TAU)
    g_hi, g_lo = _split_bf16(g)
    tri = tri_ref[...]
    b_all = (jnp.dot(tri, g_hi, preferred_element_type=F32)
             + jnp.dot(tri, g_lo, preferred_element_type=F32))

    for hh in range(GLA_HEADS):
        kc = slice(hh * GLA_DK, (hh + 1) * GLA_DK)
        vc = slice(hh * GLA_DV, (hh + 1) * GLA_DV)
        y, st = _gla_head(q_ref[:, kc], k_ref[:, kc], v_ref[:, vc], b_all[:, kc], st_ref[hh],
                          mlev_ref, mbot_ref, ones_ref[...])
        st_ref[hh] = st
        r = r_ref[:, vc].astype(F32)
        ya = _rms(y, gn_ref[...]) * (r * jax.nn.sigmoid(r))
        o_ref[:, vc] = (jax.nn.sigmoid(ga_ref[:, vc].astype(F32)) * ya).astype(BF16)


def _gla_head(q, k, v, b, st, mlev_ref, mbot_ref, ones):
    tt = q.shape[0]
    q = q.astype(F32) * (GLA_DK ** -0.5)
    k = k.astype(F32)
    b = b * LOG2E

    s_intra = jnp.zeros((tt, tt), F32)
    for li, h in enumerate(_GLA_LEVELS):
        b3 = b.reshape(tt // (2 * h), 2 * h, GLA_DK)
        e = jnp.exp2(-jnp.abs(b3 - b3[:, h - 1:h, :])).reshape(tt, GLA_DK)
        sc = lax.dot_general((q * e).astype(BF16), (k * e).astype(BF16), _NT,
                             preferred_element_type=F32)
        s_intra = s_intra + sc * mlev_ref[li]
    nb = tt // _GLA_BOT
    b3 = b.reshape(nb, _GLA_BOT, GLA_DK)
    q3 = q.reshape(nb, _GLA_BOT, GLA_DK)
    k3 = k.reshape(nb, _GLA_BOT, GLA_DK)
    prods = []
    for s in range(_GLA_BOT):
        e = jnp.exp2(jnp.minimum(b3 - b3[:, s:s + 1, :], 0.0))
        prods.append((q3 * k3[:, s:s + 1, :] * e).reshape(tt, GLA_DK).astype(BF16))
    s_intra = s_intra + jnp.dot(jnp.concatenate(prods, axis=1), ones,
                                preferred_element_type=F32) * mbot_ref[...]
    o_intra = jnp.dot(s_intra.astype(BF16), v, preferred_element_type=F32)

    outs = []
    for c in range(tt // CHUNK):
        lo = c * CHUNK
        bc = b[lo:lo + CHUNK]
        bl = b[lo + CHUNK - 1:lo + CHUNK]
        qe = (q[lo:lo + CHUNK] * jnp.exp2(bc)).astype(BF16)
        outs.append(lax.dot_general(qe, st.astype(BF16), _NT, preferred_element_type=F32))
        kd = (k[lo:lo + CHUNK] * jnp.exp2(bl - bc)).astype(BF16)
        upd = lax.dot_general(v[lo:lo + CHUNK], kd, _TN, preferred_element_type=F32)
        st = st * jnp.exp2(bl) + upd
    return o_intra + jnp.concatenate(outs, axis=0), st


def _gla(z3, gk3, wgk_pad, bgk, gnorm, col):
    b, t, _ = z3.shape
    tt = GLA_TT
    tri, mlev, mbot, ones = _gla_consts(tt)
    qk, gv = GLA_HEADS * GLA_DK, GLA_HEADS * GLA_DV

    def zspec(width, off):
        return pl.BlockSpec((None, tt, width), lambda bi, ti, o=off // width: (bi, ti, o))

    const2 = lambda bi, ti: (0, 0)
    const3 = lambda bi, ti: (0, 0, 0)
    return pl.pallas_call(
        _gla_kernel,
        grid=(b, t // tt),
        in_specs=[
            zspec(qk, col["q_g"]), zspec(qk, col["k_g"]), zspec(gv, col["v_g"]),
            zspec(gv, col["r_g"]), zspec(gv, col["gt_a"]),
            pl.BlockSpec((None, tt, LANES), lambda bi, ti: (bi, ti, 0)),
            pl.BlockSpec((LANES, qk), const2),
            pl.BlockSpec((1, qk), const2),
            pl.BlockSpec((1, GLA_DV), const2),
            pl.BlockSpec((tt, tt), const2),
            pl.BlockSpec((len(_GLA_LEVELS), tt, tt), const3),
            pl.BlockSpec((tt, tt), const2),
            pl.BlockSpec((_GLA_BOT * GLA_DK, tt), const2),
        ],
        out_specs=pl.BlockSpec((None, tt, gv), lambda bi, ti: (bi, ti, 0)),
        out_shape=jax.ShapeDtypeStruct((b, t, gv), BF16),
        scratch_shapes=[pltpu.VMEM((GLA_HEADS, GLA_DV, GLA_DK), F32)],
        compiler_params=_cparams(("parallel", "arbitrary")),
        name="gla",
    )(z3, z3, z3, z3, z3, gk3, wgk_pad, bgk, gnorm, tri, mlev, mbot, ones)


_ATT_NKB = 3


def _attn_bias(rel_bias):
    tq = ATT_TQ
    nk = _ATT_NKB * tq
    nheads = rel_bias.shape[0]
    band = (ATT_PAST + 1) * CHUNK
    assert nk - tq == ATT_PAST * CHUNK and tq % CHUNK == 0
    span = band + CHUNK - 1
    dist = np.clip(np.arange(span) - (CHUNK - 1), -REL_CLIP, REL_CLIP) + REL_CLIP
    g = rel_bias.astype(F32)[:, dist]
    x = jnp.pad(g[:, ::-1], ((0, 0), (0, 1)))
    base = jnp.tile(x, (1, CHUNK))[:, :CHUNK * span].reshape(nheads, CHUNK, span)[:, :, CHUNK - 1:CHUNK - 1 + band]
    tab = jnp.concatenate(
        [jnp.pad(base, ((0, 0), (0, 0), (ci * CHUNK, nk - band - ci * CHUNK)), constant_values=-jnp.inf)
         for ci in range(tq // CHUNK)], axis=1)
    present = np.stack([np.arange(nk) // tq >= _ATT_NKB - 1 - e for e in range(_ATT_NKB)])
    return jnp.where(jnp.asarray(present)[:, None, None, :], tab[None], -jnp.inf)


def _attn_kernel(q_ref, k0_ref, k1_ref, k2_ref, v0_ref, v1_ref, v2_ref, gb_ref, bias_ref, o_ref):
    tq = q_ref.shape[0]
    first = lax.broadcasted_iota(I32, (1, LANES), 1) < ATT_DH
    scale = jnp.asarray(ATT_DH ** -0.5, BF16)
    krefs = (k0_ref, k1_ref, k2_ref)
    vrefs = (v0_ref, v1_ref, v2_ref)
    for lb in range(q_ref.shape[1] // LANES):
        cols = slice(lb * LANES, (lb + 1) * LANES)
        q = q_ref[:, cols] * scale
        zero = jnp.zeros_like(q)
        qs = jnp.concatenate([jnp.where(first, q, zero), jnp.where(first, zero, q)], axis=0)
        s = jnp.concatenate(
            [lax.dot_general(qs, r[:, cols], _NT, preferred_element_type=F32) for r in krefs], axis=1)
        s = s + jnp.concatenate([bias_ref[2 * lb], bias_ref[2 * lb + 1]], axis=0)
        p = jnp.exp(s - jnp.max(s, axis=-1, keepdims=True))
        l = jnp.sum(p, axis=-1, keepdims=True)
        pb = p.astype(BF16)
        pv = functools.reduce(jnp.add, [
            jnp.dot(pb[:, i * tq:(i + 1) * tq], vrefs[i][:, cols], preferred_element_type=F32)
            for i in range(_ATT_NKB)]) / l
        o = jnp.where(first, pv[:tq], pv[tq:])
        o_ref[:, cols] = (jax.nn.sigmoid(gb_ref[:, cols].astype(F32)) * o).astype(BF16)


def _attn(z3, bias, col):
    b, t, _ = z3.shape
    tq = ATT_TQ
    width = ATT_STEP_HEADS * ATT_DH
    steps = ATT_HEADS // ATT_STEP_HEADS

    def cur(off):
        return pl.BlockSpec((None, tq, width), lambda h, j, bi, o=off // width: (bi, j, o + h))

    def past(off, back):
        return pl.BlockSpec((None, tq, width),
                            lambda h, j, bi, o=off // width: (bi, jnp.maximum(j - back, 0), o + h))

    return pl.pallas_call(
        _attn_kernel,
        grid=(steps, t // tq, b),
        in_specs=[
            cur(col["q_a"]),
            past(col["k_a"], 2), past(col["k_a"], 1), cur(col["k_a"]),
            past(col["v_a"], 2), past(col["v_a"], 1), cur(col["v_a"]),
            cur(col["gt_b"]),
            pl.BlockSpec((None, ATT_STEP_HEADS, tq, _ATT_NKB * tq),
                         lambda h, j, bi: (jnp.minimum(j, _ATT_NKB - 1), h, 0, 0)),
        ],
        out_specs=pl.BlockSpec((None, tq, width), lambda h, j, bi: (bi, j, h)),
        out_shape=jax.ShapeDtypeStruct((b, t, ATT_HEADS * ATT_DH), BF16),
        compiler_params=_cparams(("parallel", "parallel", "parallel")),
        name="band_attn",
    )(z3, z3, z3, z3, z3, z3, z3, z3, bias)


def _outproj_kernel(x_ref, ya_ref, yb_ref, wo_ref, lnm_ref, wr_ref, br_ref,
                    x1_ref, xm_ref, topi_ref, topw_ref, cnt_ref):
    h = (ya_ref[...].astype(F32) + yb_ref[...].astype(F32)).astype(BF16)
    x1 = x_ref[...] + jnp.dot(h, wo_ref[...], preferred_element_type=F32)
    x1_ref[...] = x1
    xm = _rms(x1, lnm_ref[...])
    _store_token_tiles(xm_ref, xm)

    tm = xm.shape[0]
    lane = lax.broadcasted_iota(I32, (tm, LANES), 1)
    lanef = lane.astype(F32)
    logits = _dot_split(xm, wr_ref[...]) + br_ref[...]
    l = jnp.where(lane < N_EXPERTS, logits, -jnp.inf)
    vals, idxs = [], []
    picked = jnp.zeros((tm, LANES), F32)
    for _ in range(TOP_K):
        m = jnp.max(l, axis=-1, keepdims=True)
        idx = jnp.min(jnp.where(l == m, lanef, float(LANES)), axis=-1, keepdims=True)
        vals.append(m)
        idxs.append(idx)
        hit = lanef == idx
        picked = jnp.where(hit, 1.0, picked)
        l = jnp.where(hit, -jnp.inf, l)
    es = [jnp.exp(vv - vals[0]) for vv in vals]
    tot = functools.reduce(jnp.add, es)
    topw = jnp.zeros((tm, LANES), F32)
    topi = jnp.zeros((tm, LANES), F32)
    for kk in range(TOP_K):
        topw = jnp.where(lane == kk, es[kk] / tot, topw)
        topi = jnp.where(lane == kk, idxs[kk], topi)
    topw_ref[...] = topw
    topi_ref[...] = topi.astype(I32)

    @pl.when(pl.program_id(0) == 0)
    def _():
        cnt_ref[...] = jnp.zeros_like(cnt_ref)

    cnt_ref[...] = cnt_ref[...] + jnp.sum(picked, axis=0, keepdims=True)


def _outproj(x2, ya, yb, w_out, ln_moe, wr_pad, br_pad):
    n, d = x2.shape
    tm = OUT_TM
    row = lambda i: (i, 0)
    const = lambda i: (0, 0)
    return pl.pallas_call(
        _outproj_kernel,
        grid=(n // tm,),
        in_specs=[
            pl.BlockSpec((tm, d), row), pl.BlockSpec((tm, d), row), pl.BlockSpec((tm, d), row),
            pl.BlockSpec((d, d), const), pl.BlockSpec((1, d), const),
            pl.BlockSpec((d, LANES), const), pl.BlockSpec((1, LANES), const),
        ],
        out_specs=[
            pl.BlockSpec((tm, d), row), pl.BlockSpec((tm * d // LANES, LANES), row),
            pl.BlockSpec((tm, LANES), row), pl.BlockSpec((tm, LANES), row),
            pl.BlockSpec((1, LANES), const),
        ],
        out_shape=[
            jax.ShapeDtypeStruct((n, d), F32), jax.ShapeDtypeStruct((n * d // LANES, LANES), F32),
            jax.ShapeDtypeStruct((n, LANES), I32), jax.ShapeDtypeStruct((n, LANES), F32),
            jax.ShapeDtypeStruct((1, LANES), F32),
        ],
        compiler_params=_cparams(("arbitrary",)),
        name="out_proj_router",
    )(x2, ya, yb, w_out, ln_moe, wr_pad, br_pad)


def _pos_kernel(topi_ref, start_ref, tri_ref, pos_ref, carry_ref):
    @pl.when(pl.program_id(0) == 0)
    def _():
        carry_ref[...] = start_ref[...]

    ti = topi_ref[...]
    tb = ti.shape[0]
    lane = lax.broadcasted_iota(I32, (tb, LANES), 1)
    sel = [lane == ti[:, kk:kk + 1] for kk in range(TOP_K)]
    oh = functools.reduce(jnp.add, [jnp.where(s, 1.0, 0.0) for s in sel])
    row = carry_ref[...] + jnp.dot(tri_ref[...], oh.astype(BF16), preferred_element_type=F32)
    out = jnp.zeros((tb, LANES), F32)
    for kk in range(TOP_K):
        rk = jnp.sum(jnp.where(sel[kk], row, 0.0), axis=-1, keepdims=True)
        out = jnp.where(lane == kk, rk, out)
    pos_ref[...] = jnp.transpose(out)[:pos_ref.shape[0]].astype(I32)
    carry_ref[...] = carry_ref[...] + jnp.sum(oh, axis=0, keepdims=True)


def _positions(topi, start_rows):
    n = topi.shape[0]
    tb = RANK_TB
    t = np.arange(tb)
    tri = jnp.asarray((t[None, :] < t[:, None]).astype(np.float32), BF16)
    return pl.pallas_call(
        _pos_kernel,
        grid=(n // tb,),
        in_specs=[pl.BlockSpec((tb, LANES), lambda i: (i, 0)),
                  pl.BlockSpec((1, LANES), lambda i: (0, 0)),
                  pl.BlockSpec((tb, tb), lambda i: (0, 0))],
        out_specs=pl.BlockSpec((SUBLANES, tb), lambda i: (0, i)),
        out_shape=jax.ShapeDtypeStruct((SUBLANES, n), I32),
        scratch_shapes=[pltpu.VMEM((1, LANES), F32)],
        compiler_params=_cparams(("arbitrary",)),
        name="route_positions",
    )(topi, start_rows, tri)


def _token_copy(src, src_tok, dst, dst_tok, sem):
    s0 = pl.multiple_of(src_tok * SUBLANES, SUBLANES)
    d0 = pl.multiple_of(dst_tok * SUBLANES, SUBLANES)
    return pltpu.make_async_copy(src.at[pl.ds(s0, SUBLANES)], dst.at[pl.ds(d0, SUBLANES)], sem)


def _tokens_wait(ref, ntok, sem):
    pltpu.make_async_copy(ref.at[pl.ds(0, ntok * SUBLANES)], ref.at[pl.ds(0, ntok * SUBLANES)], sem).wait()


def _dispatch_kernel(tv_ref, pos_hbm, xm_ref, xs_hbm, pos_smem, zeros_ref, sem_pos, sem_rows, sem_fill):
    i = pl.program_id(0)
    nsteps = pl.num_programs(0)
    tokens = xm_ref.shape[0] // SUBLANES
    ch = tokens * TOP_K
    slot = i % 2

    def pos_copy(step, sl):
        return pltpu.make_async_copy(pos_hbm.at[pl.ds(step * ch, ch)], pos_smem.at[pl.ds(sl * ch, ch)],
                                     sem_pos.at[sl])

    @pl.when(i == 0)
    def _():
        pos_copy(0, 0).start()
        tm = zeros_ref.shape[0]
        zeros_ref[...] = jnp.zeros_like(zeros_ref)

        def fill(t):
            return pltpu.make_async_copy(zeros_ref, xs_hbm.at[pl.ds(t * tm, tm)], sem_fill)

        def start(t, carry):
            @pl.when(tv_ref[t] < EXP_TM)
            def _():
                fill(t).start()
            return carry

        def wait(t, carry):
            @pl.when(tv_ref[t] < EXP_TM)
            def _():
                fill(t).wait()
            return carry

        ntiles = xs_hbm.shape[0] // tm
        lax.fori_loop(0, ntiles, start, 0)
        lax.fori_loop(0, ntiles, wait, 0)

    @pl.when(i + 1 < nsteps)
    def _():
        pos_copy(i + 1, 1 - slot).start()

    pos_copy(i, slot).wait()

    def body(t, carry):
        for kk in range(TOP_K):
            _token_copy(xm_ref, t, xs_hbm, pos_smem[slot * ch + kk * tokens + t],
                        sem_rows).start(priority=kk % 2)
        return carry

    lax.fori_loop(0, tokens, body, 0, unroll=DMA_UNROLL)
    _tokens_wait(xs_hbm, ch, sem_rows)


def _dispatch(tile_valid, pos_flat, xm, rows_total):
    tokens = DISPATCH_TOKENS
    n = xm.shape[0] // SUBLANES
    return pl.pallas_call(
        _dispatch_kernel,
        grid_spec=pltpu.PrefetchScalarGridSpec(
            num_scalar_prefetch=1,
            grid=(n // tokens,),
            in_specs=[pl.BlockSpec(memory_space=pl.ANY),
                      pl.BlockSpec((tokens * SUBLANES, LANES), lambda i, tv: (i, 0))],
            out_specs=pl.BlockSpec(memory_space=pl.ANY),
            scratch_shapes=[pltpu.SMEM((2 * tokens * TOP_K,), I32),
                            pltpu.VMEM((EXP_TM * SUBLANES, LANES), F32),
                            pltpu.SemaphoreType.DMA((2,)), pltpu.SemaphoreType.DMA,
                            pltpu.SemaphoreType.DMA],
        ),
        out_shape=jax.ShapeDtypeStruct((rows_total * SUBLANES, LANES), F32),
        compiler_params=_cparams(("arbitrary",)),
        name="dispatch",
    )(tile_valid, pos_flat, xm)


def _deinterleave_matrix():
    j = np.arange(2 * LANES)[:, None]
    c = np.arange(2 * LANES)[None, :]
    sel = np.where(c < LANES, j == 2 * c, j == 2 * (c - LANES) + 1)
    return jnp.asarray(sel.astype(np.float32), BF16)


def _expert_kernel(te_ref, tv_ref, slot_ref, next_ref, xs_ref, w1_hbm, b1g_ref, b1l_ref, w2_hbm, b2_ref,
                   sel_ref, ys_ref, w1g_ref, w1l_ref, w2p_ref, w1buf, w2buf, sem_w1, sem_w2):
    t = pl.program_id(0)
    nvalid = tv_ref[t]
    new_expert = (t == 0) | (te_ref[t] != te_ref[jnp.maximum(t - 1, 0)])

    def weight_copies(expert, sl):
        return (pltpu.make_async_copy(w1_hbm.at[expert], w1buf.at[sl], sem_w1.at[sl]),
                pltpu.make_async_copy(w2_hbm.at[expert], w2buf.at[sl], sem_w2.at[sl]))

    @pl.when((nvalid > 0) & new_expert)
    def _():
        sl = slot_ref[t]

        @pl.when(t == 0)
        def _():
            for cp in weight_copies(te_ref[t], sl):
                cp.start()

        for cp in weight_copies(te_ref[t], sl):
            cp.wait()

        @pl.when(next_ref[t] >= 0)
        def _():
            for cp in weight_copies(next_ref[t], 1 - sl):
                cp.start()

        sel = sel_ref[...]
        for m in range(w1g_ref.shape[1] // LANES):
            pair = w1buf[sl, :, 2 * m * LANES:(2 * m + 2) * LANES].astype(BF16)
            split = jnp.dot(pair, sel, preferred_element_type=F32).astype(BF16)
            w1g_ref[:, m * LANES:(m + 1) * LANES] = split[:, :LANES]
            w1l_ref[:, m * LANES:(m + 1) * LANES] = split[:, LANES:]
        w2p_ref[...] = w2buf[sl].astype(BF16)

    @pl.when(nvalid > 0)
    def _():
        tm = xs_ref.shape[0] // SUBLANES
        x = _load_token_tiles(xs_ref, tm, SUBLANES).astype(BF16)
        hg = jnp.dot(x, w1g_ref[...], preferred_element_type=F32) + b1g_ref[...]
        hl = jnp.dot(x, w1l_ref[...], preferred_element_type=F32) + b1l_ref[...]
        glu = jnp.minimum(hg, SWIGLU_LIMIT)
        lin = jnp.clip(hl, -SWIGLU_LIMIT, SWIGLU_LIMIT)
        act = glu * jax.nn.sigmoid(SWIGLU_ALPHA * glu) * (lin + 1.0)
        y = jnp.dot(act.astype(BF16), w2p_ref[...], preferred_element_type=F32) + b2_ref[...]
        _store_token_tiles(ys_ref, y)

    @pl.when(nvalid <= 0)
    def _():
        ys_ref[...] = jnp.zeros_like(ys_ref)


def _experts(tile_expert, tile_valid, tile_slot, tile_next, xs, w1, b1g, b1l, w2, b2):
    f, d = w2.shape[1:]
    p = xs.shape[0] // SUBLANES
    tm = EXP_TM
    wmap = lambda t, te, tv, ts, tn: (te[t], 0, 0)
    tile = lambda t, te, tv, ts, tn: (t, 0)
    return pl.pallas_call(
        _expert_kernel,
        grid_spec=pltpu.PrefetchScalarGridSpec(
            num_scalar_prefetch=4,
            grid=(p // tm,),
            in_specs=[
                pl.BlockSpec((tm * SUBLANES, LANES), tile),
                pl.BlockSpec(memory_space=pl.ANY),
                pl.BlockSpec((None, 1, f), wmap), pl.BlockSpec((None, 1, f), wmap),
                pl.BlockSpec(memory_space=pl.ANY), pl.BlockSpec((None, 1, d), wmap),
                pl.BlockSpec((2 * LANES, 2 * LANES), lambda t, te, tv, ts, tn: (0, 0)),
            ],
            out_specs=pl.BlockSpec((tm * SUBLANES, LANES), tile),
            scratch_shapes=[pltpu.VMEM((d, f), BF16), pltpu.VMEM((d, f), BF16),
                            pltpu.VMEM((f, d), BF16),
                            pltpu.VMEM((2, d, 2 * f), F32), pltpu.VMEM((2, f, d), F32),
                            pltpu.SemaphoreType.DMA((2,)), pltpu.SemaphoreType.DMA((2,))],
        ),
        out_shape=jax.ShapeDtypeStruct((p * SUBLANES, LANES), F32),
        compiler_params=_cparams(("arbitrary",)),
        name="experts",
    )(tile_expert, tile_valid, tile_slot, tile_next, xs, w1, b1g, b1l, w2, b2, _deinterleave_matrix())


def _ple_kernel(pos_hbm, ys_hbm, topw_ref, x1_ref, p_ref, lnp_ref, wpg_ref, wpp_ref, lnf_ref, o_ref,
                pos0, pos1, ybuf0, ybuf1, sem_pos, sem_rows):
    i = pl.program_id(0)
    nsteps = pl.num_programs(0)
    tb = x1_ref.shape[0] // 2
    ch = tb * TOP_K
    nblocks = 2 * nsteps
    pos_bufs = (pos0, pos1)
    ybufs = (ybuf0, ybuf1)

    def pos_copy(block, par):
        blk = jnp.minimum(block, nblocks - 1)
        return pltpu.make_async_copy(pos_hbm.at[pl.ds(blk * ch, ch)], pos_bufs[par], sem_pos.at[par])

    def issue_gathers(par):
        def body(t, carry):
            for kk in range(TOP_K):
                _token_copy(ys_hbm, pos_bufs[par][kk * tb + t], ybufs[par].at[kk], t,
                            sem_rows.at[par]).start(priority=kk % 2)
            return carry
        lax.fori_loop(0, tb, body, 0, unroll=True)

    def wait_gathers(par):
        for kk in range(TOP_K):
            _tokens_wait(ybufs[par].at[kk], tb, sem_rows.at[par])

    def compute(par):
        rows = slice(par * tb, (par + 1) * tb)
        topw = topw_ref[rows, :]
        groups = x1_ref.shape[1] // LANES
        moe = functools.reduce(jnp.add, [
            topw[:, kk:kk + 1] * _load_token_tiles(ybufs[par].at[kk], tb, groups)
            for kk in range(TOP_K)])
        x2 = x1_ref[rows, :] + moe
        gate = jax.nn.sigmoid(jnp.dot(_rms(x2, lnp_ref[...]).astype(BF16), wpg_ref[...],
                                      preferred_element_type=F32))
        proj = jnp.dot(p_ref[rows, :].astype(BF16), wpp_ref[...], preferred_element_type=F32)
        o_ref[rows, :] = _rms(x2 + gate * proj, lnf_ref[...])

    @pl.when(i == 0)
    def _():
        pos_copy(0, 0).start()
        pos_copy(0, 0).wait()
        issue_gathers(0)
        pos_copy(1, 1).start()

    pos_copy(2 * i + 1, 1).wait()
    wait_gathers(0)
    pos_copy(2 * i + 2, 0).start()
    issue_gathers(1)
    compute(0)
    pos_copy(2 * i + 2, 0).wait()
    wait_gathers(1)
    pos_copy(2 * i + 3, 1).start()
    issue_gathers(0)
    compute(1)

    @pl.when(i == nsteps - 1)
    def _():
        wait_gathers(0)
        pos_copy(0, 1).wait()


def _ple(pos_flat, ys, topw, x1, p2, ln_ple, wpg, wpp, ln_final):
    n, d = x1.shape
    pd = p2.shape[1]
    tb = PLE_TB
    row = lambda i: (i, 0)
    const = lambda i: (0, 0)
    return pl.pallas_call(
        _ple_kernel,
        grid=(n // (2 * tb),),
        in_specs=[
            pl.BlockSpec(memory_space=pl.ANY), pl.BlockSpec(memory_space=pl.ANY),
            pl.BlockSpec((2 * tb, LANES), row), pl.BlockSpec((2 * tb, d), row),
            pl.BlockSpec((2 * tb, pd), row),
            pl.BlockSpec((1, d), const), pl.BlockSpec((d, d), const),
            pl.BlockSpec((pd, d), const), pl.BlockSpec((1, d), const),
        ],
        out_specs=pl.BlockSpec((2 * tb, d), row),
        out_shape=jax.ShapeDtypeStruct((n, d), F32),
        scratch_shapes=[
            pltpu.SMEM((tb * TOP_K,), I32), pltpu.SMEM((tb * TOP_K,), I32),
            pltpu.VMEM((TOP_K, tb * d // LANES, LANES), F32),
            pltpu.VMEM((TOP_K, tb * d // LANES, LANES), F32),
            pltpu.SemaphoreType.DMA((2,)),
            pltpu.SemaphoreType.DMA((2,)),
        ],
        compiler_params=_cparams(("arbitrary",)),
        name="combine_ple_final",
    )(pos_flat, ys, topw, x1, p2, ln_ple, wpg, wpp, ln_final)


def _layer(x2, p2, bsz, seq, ln_mix, w_in, w_gk, b_gk, gla_norm, rel_bias, w_out, ln_moe,
           w_router, b_router, w1, b1, w2, b2, ln_ple, w_ple_gate, w_ple_proj, ln_out):
    n, d = x2.shape
    qk, gv, aw = GLA_HEADS * GLA_DK, GLA_HEADS * GLA_DV, ATT_HEADS * ATT_DH
    names = ("q_g", "k_g", "v_g", "gk_low", "r_g", "q_a", "k_a", "v_a", "gt_a", "gt_b")
    widths = (qk, qk, gv, GLA_RANK, gv, aw, aw, aw, d, d)
    src = dict(zip(names, np.cumsum((0,) + widths[:-1]).tolist()))
    wid = dict(zip(names, widths))
    order = [nm for nm in names if nm != "gk_low"]
    col, off = {}, 0
    for nm in order:
        col[nm] = off
        off += wid[nm]
    lo, hi = src["gk_low"], src["gk_low"] + GLA_RANK
    assert lo % IN_TN == 0 and (w_in.shape[1] - hi) % IN_TN == 0
    w_low = jnp.pad(w_in[:, lo:hi], ((0, 0), (0, LANES - GLA_RANK))).astype(BF16)

    z, gk = _in_proj(x2, ln_mix.reshape(1, d), w_in[:, :lo].astype(BF16), w_in[:, hi:].astype(BF16), w_low)
    z3 = z.reshape(bsz, seq, -1)
    gk3 = gk.reshape(bsz, seq, LANES)

    wgk_pad = jnp.pad(w_gk, ((0, LANES - GLA_RANK), (0, 0)))
    ya = _gla(z3, gk3, wgk_pad, b_gk.reshape(1, qk), gla_norm.reshape(1, GLA_DV), col)
    yb = _attn(z3, _attn_bias(rel_bias), col)

    wr_pad = jnp.pad(w_router, ((0, 0), (0, LANES - N_EXPERTS)))
    br_pad = jnp.pad(b_router, (0, LANES - N_EXPERTS)).reshape(1, LANES)
    x1, xm, topi, topw, cnt = _outproj(x2, ya.reshape(n, d), yb.reshape(n, d), w_out.astype(BF16),
                                       ln_moe.reshape(1, d), wr_pad, br_pad)

    counts = cnt[0, :N_EXPERTS].astype(I32)
    ntile = (counts + EXP_TM - 1) // EXP_TM
    tile_end = jnp.cumsum(ntile)
    tile_start = tile_end - ntile
    rows_total = n * TOP_K + N_EXPERTS * EXP_TM
    tiles = jnp.arange(rows_total // EXP_TM, dtype=I32)
    onehot = (tiles[:, None] >= tile_start[None, :]) & (tiles[:, None] < tile_end[None, :])
    te = jnp.sum(jnp.where(onehot, jnp.arange(N_EXPERTS, dtype=I32)[None, :], 0), axis=1)
    tv = jnp.sum(jnp.where(onehot, counts[None, :] - (tiles[:, None] - tile_start[None, :]) * EXP_TM, 0),
                 axis=1)
    te = jnp.where(tiles < tile_end[-1], te, N_EXPERTS - 1).astype(I32)
    tv = jnp.clip(tv, 0, EXP_TM).astype(I32)
    start_rows = jnp.pad((tile_start * EXP_TM).astype(F32), (0, LANES - N_EXPERTS)).reshape(1, LANES)
    active = ntile > 0
    eid = jnp.arange(N_EXPERTS, dtype=I32)
    slot_e = (jnp.cumsum(active.astype(I32)) - 1) % 2
    later = (eid[None, :] > eid[:, None]) & active[None, :]
    next_e = jnp.min(jnp.where(later, eid[None, :], N_EXPERTS), axis=1)
    next_e = jnp.where(next_e < N_EXPERTS, next_e, -1)
    tslot = slot_e[te].astype(I32)
    tnext = next_e[te].astype(I32)

    pos_t = _positions(topi, start_rows)[:TOP_K]

    def pos_blocks(tokens):
        return pos_t.reshape(TOP_K, n // tokens, tokens).transpose(1, 0, 2).reshape(-1)

    xs = _dispatch(tv, pos_blocks(DISPATCH_TOKENS), xm, rows_total)

    ys = _experts(te, tv, tslot, tnext, xs, w1,
                  b1[:, 0::2].reshape(N_EXPERTS, 1, -1), b1[:, 1::2].reshape(N_EXPERTS, 1, -1),
                  w2, b2.reshape(N_EXPERTS, 1, d))

    return _ple(pos_blocks(PLE_TB), ys, topw, x1, p2, ln_ple.reshape(1, d), w_ple_gate.astype(BF16),
                w_ple_proj.astype(BF16), ln_out.reshape(1, d))


def kernel(x, p, ln_mix, w_in, w_gk, b_gk, gla_norm, rel_bias, w_out, ln_moe, w_router, b_router,
           w1, b1, w2, b2, ln_ple, w_ple_gate, w_ple_proj, ln_final):
    bsz, seq, d = x.shape
    depth = p.shape[0]
    assert depth == 1, "the final RMSNorm is fused into the last layer's kernel"
    assert d == SUBLANES * LANES, "token-tile layout needs one vreg tile per token"
    assert seq % max(ATT_TQ, GLA_TT) == 0
    assert (bsz * seq) % max(IN_TM, OUT_TM, RANK_TB, DISPATCH_TOKENS, 2 * PLE_TB) == 0
    x2 = x.reshape(bsz * seq, d)
    out = _layer(x2, p[0].reshape(bsz * seq, -1), bsz, seq, ln_mix[0], w_in[0], w_gk[0], b_gk[0],
                 gla_norm[0], rel_bias[0], w_out[0], ln_moe[0], w_router[0], b_router[0],
                 w1[0], b1[0], w2[0], b2[0], ln_ple[0], w_ple_gate[0], w_ple_proj[0], ln_final)
    return out.reshape(bsz, seq, d)
```

```python
import functools

import numpy as np
import jax
import jax.numpy as jnp
from jax import lax
from jax.experimental import pallas as pl
from jax.experimental.pallas import tpu as pltpu

F32 = jnp.float32
BF16 = jnp.bfloat16
I32 = jnp.int32

LANES = 128
SUBLANES = 8
CHUNK = 64
GLA_HEADS = 4
GLA_DK = 128
GLA_DV = 256
GLA_RANK = 16
GLA_TAU = 16.0
ATT_HEADS = 16
ATT_DH = 64
ATT_PAST = 8
REL_CLIP = 256
N_EXPERTS = 32
TOP_K = 4
SWIGLU_ALPHA = 1.702
SWIGLU_LIMIT = 7.0
EPS = 1e-6
LOG2E = 1.4426950408889634

VMEM_LIMIT = 48 * 1024 * 1024

IN_TM, IN_TN = 512, 1024
GLA_TT = 256
ATT_TQ = 256
ATT_STEP_HEADS = 16
OUT_TM = 512
RANK_TB = 1024
EXP_TM = 512
DISPATCH_TOKENS = 2048
PLE_TB = 256
DMA_UNROLL = 8


def _cparams(sem):
    return pltpu.CompilerParams(dimension_semantics=sem, vmem_limit_bytes=VMEM_LIMIT)


def _split_bf16(a):
    hi = a.astype(BF16)
    lo = (a - hi.astype(F32)).astype(BF16)
    return hi, lo


def _dot_split(a, b):
    a_hi, a_lo = _split_bf16(a)
    b_hi, b_lo = _split_bf16(b)
    d = functools.partial(jnp.dot, preferred_element_type=F32)
    return d(a_hi, b_hi) + (d(a_hi, b_lo) + d(a_lo, b_hi))


def _rms(x, gain):
    ms = jnp.mean(x * x, axis=-1, keepdims=True)
    return x * lax.rsqrt(ms + EPS) * gain


def _store_token_tiles(ref, val):
    groups = val.shape[1] // LANES
    for c in range(groups):
        ref[pl.ds(c, val.shape[0], stride=groups), :] = val[:, c * LANES:(c + 1) * LANES]


def _load_token_tiles(ref, rows, groups):
    return jnp.concatenate([ref[pl.ds(c, rows, stride=groups), :] for c in range(groups)], axis=1)


_NT = (((1,), (1,)), ((), ()))
_TN = (((0,), (0,)), ((), ()))


def _in_proj_kernel(x_ref, g_ref, wa_ref, wb_ref, wlow_ref, z_ref, gk_ref):
    xn = _rms(x_ref[...], g_ref[...]).astype(BF16)
    gk_ref[...] = jnp.dot(xn, wlow_ref[...], preferred_element_type=F32)
    off = 0
    for w_ref in (wa_ref, wb_ref):
        for j in range(w_ref.shape[1] // IN_TN):
            z_ref[:, off:off + IN_TN] = jnp.dot(xn, w_ref[:, j * IN_TN:(j + 1) * IN_TN],
                                                preferred_element_type=F32).astype(BF16)
            off += IN_TN


def _in_proj(x2, ln, w_a, w_b, w_low):
    n, d = x2.shape
    ncol = w_a.shape[1] + w_b.shape[1]
    once = pl.Buffered(1)
    return pl.pallas_call(
        _in_proj_kernel,
        grid=(n // IN_TM,),
        in_specs=[
            pl.BlockSpec((IN_TM, d), lambda i: (i, 0)),
            pl.BlockSpec((1, d), lambda i: (0, 0)),
            pl.BlockSpec(w_a.shape, lambda i: (0, 0), pipeline_mode=once),
            pl.BlockSpec(w_b.shape, lambda i: (0, 0), pipeline_mode=once),
            pl.BlockSpec((d, LANES), lambda i: (0, 0), pipeline_mode=once),
        ],
        out_specs=[
            pl.BlockSpec((IN_TM, ncol), lambda i: (i, 0)),
            pl.BlockSpec((IN_TM, LANES), lambda i: (i, 0)),
        ],
        out_shape=[
            jax.ShapeDtypeStruct((n, ncol), BF16),
            jax.ShapeDtypeStruct((n, LANES), F32),
        ],
        compiler_params=_cparams(("parallel",)),
        name="in_proj",
    )(x2, ln, w_a, w_b, w_low)


_GLA_LEVELS = (8, 16, 32)
_GLA_BOT = 8


def _gla_consts(tt):
    t = np.arange(tt)
    same_chunk = (t[:, None] // CHUNK) == (t[None, :] // CHUNK)
    tri = (same_chunk & (t[None, :] <= t[:, None])).astype(np.float32)
    mlev = []
    for h in _GLA_LEVELS:
        blk = (t[:, None] // (2 * h)) == (t[None, :] // (2 * h))
        m = blk & ((t[:, None] % (2 * h)) >= h) & ((t[None, :] % (2 * h)) < h)
        mlev.append(m.astype(np.float32))
    mbot = (((t[None, :] // _GLA_BOT) == (t[:, None] // _GLA_BOT))
            & ((t[None, :] % _GLA_BOT) <= (t[:, None] % _GLA_BOT)))
    spread = (np.arange(_GLA_BOT * GLA_DK)[:, None] // GLA_DK) == (t[None, :] % _GLA_BOT)
    return (jnp.asarray(tri, BF16), jnp.asarray(np.stack(mlev), F32),
            jnp.asarray(mbot.astype(np.float32), F32), jnp.asarray(spread.astype(np.float32), BF16))


def _gla_kernel(q_ref, k_ref, v_ref, r_ref, ga_ref, gk_ref, wgk_ref, bgk_ref, gn_ref,
                tri_ref, mlev_ref, mbot_ref, ones_ref, o_ref, st_ref):
    @pl.when(pl.program_id(1) == 0)
    def _():
        st_ref[...] = jnp.zeros_like(st_ref)

    xg = _dot_split(gk_ref[...], wgk_ref[...]) + bgk_ref[...]
    g = -(jnp.maximum(-xg, 0.0) + jnp.log(1.0 + jnp.exp(-jnp.abs(xg)))) * (1.0 / GLA_TAU)
    g_hi, g_lo = _split_bf16(g)
    tri = tri_ref[...]
    b_all = (jnp.dot(tri, g_hi, preferred_element_type=F32)
             + jnp.dot(tri, g_lo, preferred_element_type=F32))

    for hh in range(GLA_HEADS):
        kc = slice(hh * GLA_DK, (hh + 1) * GLA_DK)
        vc = slice(hh * GLA_DV, (hh + 1) * GLA_DV)
        y, st = _gla_head(q_ref[:, kc], k_ref[:, kc], v_ref[:, vc], b_all[:, kc], st_ref[hh],
                          mlev_ref, mbot_ref, ones_ref[...])
        st_ref[hh] = st
        r = r_ref[:, vc].astype(F32)
        ya = _rms(y, gn_ref[...]) * (r * jax.nn.sigmoid(r))
        o_ref[:, vc] = (jax.nn.sigmoid(ga_ref[:, vc].astype(F32)) * ya).astype(BF16)


def _gla_head(q, k, v, b, st, mlev_ref, mbot_ref, ones):
    tt = q.shape[0]
    q = q.astype(F32) * (GLA_DK ** -0.5)
    k = k.astype(F32)
    b = b * LOG2E

    s_intra = jnp.zeros((tt, tt), F32)
    for li, h in enumerate(_GLA_LEVELS):
        b3 = b.reshape(tt // (2 * h), 2 * h, GLA_DK)
        e = jnp.exp2(-jnp.abs(b3 - b3[:, h - 1:h, :])).reshape(tt, GLA_DK)
        sc = lax.dot_general((q * e).astype(BF16), (k * e).astype(BF16), _NT,
                             preferred_element_type=F32)
        s_intra = s_intra + sc * mlev_ref[li]
    nb = tt // _GLA_BOT
    b3 = b.reshape(nb, _GLA_BOT, GLA_DK)
    q3 = q.reshape(nb, _GLA_BOT, GLA_DK)
    k3 = k.reshape(nb, _GLA_BOT, GLA_DK)
    prods = []
    for s in range(_GLA_BOT):
        e = jnp.exp2(jnp.minimum(b3 - b3[:, s:s + 1, :], 0.0))
        prods.append((q3 * k3[:, s:s + 1, :] * e).reshape(tt, GLA_DK).astype(BF16))
    s_intra = s_intra + jnp.dot(jnp.concatenate(prods, axis=1), ones,
                                preferred_element_type=F32) * mbot_ref[...]
    o_intra = jnp.dot(s_intra.astype(BF16), v, preferred_element_type=F32)

    outs = []
    for c in range(tt // CHUNK):
        lo = c * CHUNK
        bc = b[lo:lo + CHUNK]
        bl = b[lo + CHUNK - 1:lo + CHUNK]
        qe = (q[lo:lo + CHUNK] * jnp.exp2(bc)).astype(BF16)
        outs.append(lax.dot_general(qe, st.astype(BF16), _NT, preferred_element_type=F32))
        kd = (k[lo:lo + CHUNK] * jnp.exp2(bl - bc)).astype(BF16)
        upd = lax.dot_general(v[lo:lo + CHUNK], kd, _TN, preferred_element_type=F32)
        st = st * jnp.exp2(bl) + upd
    return o_intra + jnp.concatenate(outs, axis=0), st


def _gla(z3, gk3, wgk_pad, bgk, gnorm, col):
    b, t, _ = z3.shape
    tt = GLA_TT
    tri, mlev, mbot, ones = _gla_consts(tt)
    qk, gv = GLA_HEADS * GLA_DK, GLA_HEADS * GLA_DV

    def zspec(width, off):
        return pl.BlockSpec((None, tt, width), lambda bi, ti, o=off // width: (bi, ti, o))

    const2 = lambda bi, ti: (0, 0)
    const3 = lambda bi, ti: (0, 0, 0)
    return pl.pallas_call(
        _gla_kernel,
        grid=(b, t // tt),
        in_specs=[
            zspec(qk, col["q_g"]), zspec(qk, col["k_g"]), zspec(gv, col["v_g"]),
            zspec(gv, col["r_g"]), zspec(gv, col["gt_a"]),
            pl.BlockSpec((None, tt, LANES), lambda bi, ti: (bi, ti, 0)),
            pl.BlockSpec((LANES, qk), const2),
            pl.BlockSpec((1, qk), const2),
            pl.BlockSpec((1, GLA_DV), const2),
            pl.BlockSpec((tt, tt), const2),
            pl.BlockSpec((len(_GLA_LEVELS), tt, tt), const3),
            pl.BlockSpec((tt, tt), const2),
            pl.BlockSpec((_GLA_BOT * GLA_DK, tt), const2),
        ],
        out_specs=pl.BlockSpec((None, tt, gv), lambda bi, ti: (bi, ti, 0)),
        out_shape=jax.ShapeDtypeStruct((b, t, gv), BF16),
        scratch_shapes=[pltpu.VMEM((GLA_HEADS, GLA_DV, GLA_DK), F32)],
        compiler_params=_cparams(("parallel", "arbitrary")),
        name="gla",
    )(z3, z3, z3, z3, z3, gk3, wgk_pad, bgk, gnorm, tri, mlev, mbot, ones)


_ATT_NKB = 3


def _attn_bias(rel_bias):
    tq = ATT_TQ
    nk = _ATT_NKB * tq
    nheads = rel_bias.shape[0]
    band = (ATT_PAST + 1) * CHUNK
    assert nk - tq == ATT_PAST * CHUNK and tq % CHUNK == 0
    span = band + CHUNK - 1
    dist = np.clip(np.arange(span) - (CHUNK - 1), -REL_CLIP, REL_CLIP) + REL_CLIP
    g = rel_bias.astype(F32)[:, dist]
    x = jnp.pad(g[:, ::-1], ((0, 0), (0, 1)))
    base = jnp.tile(x, (1, CHUNK))[:, :CHUNK * span].reshape(nheads, CHUNK, span)[:, :, CHUNK - 1:CHUNK - 1 + band]
    tab = jnp.concatenate(
        [jnp.pad(base, ((0, 0), (0, 0), (ci * CHUNK, nk - band - ci * CHUNK)), constant_values=-jnp.inf)
         for ci in range(tq // CHUNK)], axis=1)
    present = np.stack([np.arange(nk) // tq >= _ATT_NKB - 1 - e for e in range(_ATT_NKB)])
    return jnp.where(jnp.asarray(present)[:, None, None, :], tab[None], -jnp.inf)


def _attn_kernel(q_ref, k0_ref, k1_ref, k2_ref, v0_ref, v1_ref, v2_ref, gb_ref, bias_ref, o_ref):
    tq = q_ref.shape[0]
    first = lax.broadcasted_iota(I32, (1, LANES), 1) < ATT_DH
    scale = jnp.asarray(ATT_DH ** -0.5, BF16)
    krefs = (k0_ref, k1_ref, k2_ref)
    vrefs = (v0_ref, v1_ref, v2_ref)
    for lb in range(q_ref.shape[1] // LANES):
        cols = slice(lb * LANES, (lb + 1) * LANES)
        q = q_ref[:, cols] * scale
        zero = jnp.zeros_like(q)
        qs = jnp.concatenate([jnp.where(first, q, zero), jnp.where(first, zero, q)], axis=0)
        s = jnp.concatenate(
            [lax.dot_general(qs, r[:, cols], _NT, preferred_element_type=F32) for r in krefs], axis=1)
        s = s + jnp.concatenate([bias_ref[2 * lb], bias_ref[2 * lb + 1]], axis=0)
        p = jnp.exp(s - jnp.max(s, axis=-1, keepdims=True))
        l = jnp.sum(p, axis=-1, keepdims=True)
        pb = p.astype(BF16)
        pv = functools.reduce(jnp.add, [
            jnp.dot(pb[:, i * tq:(i + 1) * tq], vrefs[i][:, cols], preferred_element_type=F32)
            for i in range(_ATT_NKB)]) / l
        o = jnp.where(first, pv[:tq], pv[tq:])
        o_ref[:, cols] = (jax.nn.sigmoid(gb_ref[:, cols].astype(F32)) * o).astype(BF16)


def _attn(z3, bias, col):
    b, t, _ = z3.shape
    tq = ATT_TQ
    width = ATT_STEP_HEADS * ATT_DH
    steps = ATT_HEADS // ATT_STEP_HEADS

    def cur(off):
        return pl.BlockSpec((None, tq, width), lambda h, j, bi, o=off // width: (bi, j, o + h))

    def past(off, back):
        return pl.BlockSpec((None, tq, width),
                            lambda h, j, bi, o=off // width: (bi, jnp.maximum(j - back, 0), o + h))

    return pl.pallas_call(
        _attn_kernel,
        grid=(steps, t // tq, b),
        in_specs=[
            cur(col["q_a"]),
            past(col["k_a"], 2), past(col["k_a"], 1), cur(col["k_a"]),
            past(col["v_a"], 2), past(col["v_a"], 1), cur(col["v_a"]),
            cur(col["gt_b"]),
            pl.BlockSpec((None, ATT_STEP_HEADS, tq, _ATT_NKB * tq),
                         lambda h, j, bi: (jnp.minimum(j, _ATT_NKB - 1), h, 0, 0)),
        ],
        out_specs=pl.BlockSpec((None, tq, width), lambda h, j, bi: (bi, j, h)),
        out_shape=jax.ShapeDtypeStruct((b, t, ATT_HEADS * ATT_DH), BF16),
        compiler_params=_cparams(("parallel", "parallel", "parallel")),
        name="band_attn",
    )(z3, z3, z3, z3, z3, z3, z3, z3, bias)


def _outproj_kernel(x_ref, ya_ref, yb_ref, wo_ref, lnm_ref, wr_ref, br_ref,
                    x1_ref, xm_ref, topi_ref, topw_ref, cnt_ref):
    h = (ya_ref[...].astype(F32) + yb_ref[...].astype(F32)).astype(BF16)
    x1 = x_ref[...] + jnp.dot(h, wo_ref[...], preferred_element_type=F32)
    x1_ref[...] = x1
    xm = _rms(x1, lnm_ref[...])
    _store_token_tiles(xm_ref, xm)

    tm = xm.shape[0]
    lane = lax.broadcasted_iota(I32, (tm, LANES), 1)
    lanef = lane.astype(F32)
    logits = _dot_split(xm, wr_ref[...]) + br_ref[...]
    l = jnp.where(lane < N_EXPERTS, logits, -jnp.inf)
    vals, idxs = [], []
    picked = jnp.zeros((tm, LANES), F32)
    for _ in range(TOP_K):
        m = jnp.max(l, axis=-1, keepdims=True)
        idx = jnp.min(jnp.where(l == m, lanef, float(LANES)), axis=-1, keepdims=True)
        vals.append(m)
        idxs.append(idx)
        hit = lanef == idx
        picked = jnp.where(hit, 1.0, picked)
        l = jnp.where(hit, -jnp.inf, l)
    es = [jnp.exp(vv - vals[0]) for vv in vals]
    tot = functools.reduce(jnp.add, es)
    topw = jnp.zeros((tm, LANES), F32)
    topi = jnp.zeros((tm, LANES), F32)
    for kk in range(TOP_K):
        topw = jnp.where(lane == kk, es[kk] / tot, topw)
        topi = jnp.where(lane == kk, idxs[kk], topi)
    topw_ref[...] = topw
    topi_ref[...] = topi.astype(I32)

    @pl.when(pl.program_id(0) == 0)
    def _():
        cnt_ref[...] = jnp.zeros_like(cnt_ref)

    cnt_ref[...] = cnt_ref[...] + jnp.sum(picked, axis=0, keepdims=True)


def _outproj(x2, ya, yb, w_out, ln_moe, wr_pad, br_pad):
    n, d = x2.shape
    tm = OUT_TM
    row = lambda i: (i, 0)
    const = lambda i: (0, 0)
    return pl.pallas_call(
        _outproj_kernel,
        grid=(n // tm,),
        in_specs=[
            pl.BlockSpec((tm, d), row), pl.BlockSpec((tm, d), row), pl.BlockSpec((tm, d), row),
            pl.BlockSpec((d, d), const), pl.BlockSpec((1, d), const),
            pl.BlockSpec((d, LANES), const), pl.BlockSpec((1, LANES), const),
        ],
        out_specs=[
            pl.BlockSpec((tm, d), row), pl.BlockSpec((tm * d // LANES, LANES), row),
            pl.BlockSpec((tm, LANES), row), pl.BlockSpec((tm, LANES), row),
            pl.BlockSpec((1, LANES), const),
        ],
        out_shape=[
            jax.ShapeDtypeStruct((n, d), F32), jax.ShapeDtypeStruct((n * d // LANES, LANES), F32),
            jax.ShapeDtypeStruct((n, LANES), I32), jax.ShapeDtypeStruct((n, LANES), F32),
            jax.ShapeDtypeStruct((1, LANES), F32),
        ],
        compiler_params=_cparams(("arbitrary",)),
        name="out_proj_router",
    )(x2, ya, yb, w_out, ln_moe, wr_pad, br_pad)


def _pos_kernel(topi_ref, start_ref, tri_ref, pos_ref, carry_ref):
    @pl.when(pl.program_id(0) == 0)
    def _():
        carry_ref[...] = start_ref[...]

    ti = topi_ref[...]
    tb = ti.shape[0]
    lane = lax.broadcasted_iota(I32, (tb, LANES), 1)
    sel = [lane == ti[:, kk:kk + 1] for kk in range(TOP_K)]
    oh = functools.reduce(jnp.add, [jnp.where(s, 1.0, 0.0) for s in sel])
    row = carry_ref[...] + jnp.dot(tri_ref[...], oh.astype(BF16), preferred_element_type=F32)
    out = jnp.zeros((tb, LANES), F32)
    for kk in range(TOP_K):
        rk = jnp.sum(jnp.where(sel[kk], row, 0.0), axis=-1, keepdims=True)
        out = jnp.where(lane == kk, rk, out)
    pos_ref[...] = jnp.transpose(out)[:pos_ref.shape[0]].astype(I32)
    carry_ref[...] = carry_ref[...] + jnp.sum(oh, axis=0, keepdims=True)


def _positions(topi, start_rows):
    n = topi.shape[0]
    tb = RANK_TB
    t = np.arange(tb)
    tri = jnp.asarray((t[None, :] < t[:, None]).astype(np.float32), BF16)
    return pl.pallas_call(
        _pos_kernel,
        grid=(n // tb,),
        in_specs=[pl.BlockSpec((tb, LANES), lambda i: (i, 0)),
                  pl.BlockSpec((1, LANES), lambda i: (0, 0)),
                  pl.BlockSpec((tb, tb), lambda i: (0, 0))],
        out_specs=pl.BlockSpec((SUBLANES, tb), lambda i: (0, i)),
        out_shape=jax.ShapeDtypeStruct((SUBLANES, n), I32),
        scratch_shapes=[pltpu.VMEM((1, LANES), F32)],
        compiler_params=_cparams(("arbitrary",)),
        name="route_positions",
    )(topi, start_rows, tri)


def _token_copy(src, src_tok, dst, dst_tok, sem):
    s0 = pl.multiple_of(src_tok * SUBLANES, SUBLANES)
    d0 = pl.multiple_of(dst_tok * SUBLANES, SUBLANES)
    return pltpu.make_async_copy(src.at[pl.ds(s0, SUBLANES)], dst.at[pl.ds(d0, SUBLANES)], sem)


def _tokens_wait(ref, ntok, sem):
    pltpu.make_async_copy(ref.at[pl.ds(0, ntok * SUBLANES)], ref.at[pl.ds(0, ntok * SUBLANES)], sem).wait()


def _dispatch_kernel(tv_ref, pos_hbm, xm_ref, xs_hbm, pos_smem, zeros_ref, sem_pos, sem_rows, sem_fill):
    i = pl.program_id(0)
    nsteps = pl.num_programs(0)
    tokens = xm_ref.shape[0] // SUBLANES
    ch = tokens * TOP_K
    slot = i % 2

    def pos_copy(step, sl):
        return pltpu.make_async_copy(pos_hbm.at[pl.ds(step * ch, ch)], pos_smem.at[pl.ds(sl * ch, ch)],
                                     sem_pos.at[sl])

    @pl.when(i == 0)
    def _():
        pos_copy(0, 0).start()
        tm = zeros_ref.shape[0]
        zeros_ref[...] = jnp.zeros_like(zeros_ref)

        def fill(t):
            return pltpu.make_async_copy(zeros_ref, xs_hbm.at[pl.ds(t * tm, tm)], sem_fill)

        def start(t, carry):
            @pl.when(tv_ref[t] < EXP_TM)
            def _():
                fill(t).start()
            return carry

        def wait(t, carry):
            @pl.when(tv_ref[t] < EXP_TM)
            def _():
                fill(t).wait()
            return carry

        ntiles = xs_hbm.shape[0] // tm
        lax.fori_loop(0, ntiles, start, 0)
        lax.fori_loop(0, ntiles, wait, 0)

    @pl.when(i + 1 < nsteps)
    def _():
        pos_copy(i + 1, 1 - slot).start()

    pos_copy(i, slot).wait()

    def body(t, carry):
        for kk in range(TOP_K):
            _token_copy(xm_ref, t, xs_hbm, pos_smem[slot * ch + kk * tokens + t],
                        sem_rows).start(priority=kk % 2)
        return carry

    lax.fori_loop(0, tokens, body, 0, unroll=DMA_UNROLL)
    _tokens_wait(xs_hbm, ch, sem_rows)


def _dispatch(tile_valid, pos_flat, xm, rows_total):
    tokens = DISPATCH_TOKENS
    n = xm.shape[0] // SUBLANES
    return pl.pallas_call(
        _dispatch_kernel,
        grid_spec=pltpu.PrefetchScalarGridSpec(
            num_scalar_prefetch=1,
            grid=(n // tokens,),
            in_specs=[pl.BlockSpec(memory_space=pl.ANY),
                      pl.BlockSpec((tokens * SUBLANES, LANES), lambda i, tv: (i, 0))],
            out_specs=pl.BlockSpec(memory_space=pl.ANY),
            scratch_shapes=[pltpu.SMEM((2 * tokens * TOP_K,), I32),
                            pltpu.VMEM((EXP_TM * SUBLANES, LANES), F32),
                            pltpu.SemaphoreType.DMA((2,)), pltpu.SemaphoreType.DMA,
                            pltpu.SemaphoreType.DMA],
        ),
        out_shape=jax.ShapeDtypeStruct((rows_total * SUBLANES, LANES), F32),
        compiler_params=_cparams(("arbitrary",)),
        name="dispatch",
    )(tile_valid, pos_flat, xm)


def _deinterleave_matrix():
    j = np.arange(2 * LANES)[:, None]
    c = np.arange(2 * LANES)[None, :]
    sel = np.where(c < LANES, j == 2 * c, j == 2 * (c - LANES) + 1)
    return jnp.asarray(sel.astype(np.float32), BF16)


def _expert_kernel(te_ref, tv_ref, slot_ref, next_ref, xs_ref, w1_hbm, b1g_ref, b1l_ref, w2_hbm, b2_ref,
                   sel_ref, ys_ref, w1g_ref, w1l_ref, w2p_ref, w1buf, w2buf, sem_w1, sem_w2):
    t = pl.program_id(0)
    nvalid = tv_ref[t]
    new_expert = (t == 0) | (te_ref[t] != te_ref[jnp.maximum(t - 1, 0)])

    def weight_copies(expert, sl):
        return (pltpu.make_async_copy(w1_hbm.at[expert], w1buf.at[sl], sem_w1.at[sl]),
                pltpu.make_async_copy(w2_hbm.at[expert], w2buf.at[sl], sem_w2.at[sl]))

    @pl.when((nvalid > 0) & new_expert)
    def _():
        sl = slot_ref[t]

        @pl.when(t == 0)
        def _():
            for cp in weight_copies(te_ref[t], sl):
                cp.start()

        for cp in weight_copies(te_ref[t], sl):
            cp.wait()

        @pl.when(next_ref[t] >= 0)
        def _():
            for cp in weight_copies(next_ref[t], 1 - sl):
                cp.start()

        sel = sel_ref[...]
        for m in range(w1g_ref.shape[1] // LANES):
            pair = w1buf[sl, :, 2 * m * LANES:(2 * m + 2) * LANES].astype(BF16)
            split = jnp.dot(pair, sel, preferred_element_type=F32).astype(BF16)
            w1g_ref[:, m * LANES:(m + 1) * LANES] = split[:, :LANES]
            w1l_ref[:, m * LANES:(m + 1) * LANES] = split[:, LANES:]
        w2p_ref[...] = w2buf[sl].astype(BF16)

    @pl.when(nvalid > 0)
    def _():
        tm = xs_ref.shape[0] // SUBLANES
        x = _load_token_tiles(xs_ref, tm, SUBLANES).astype(BF16)
        hg = jnp.dot(x, w1g_ref[...], preferred_element_type=F32) + b1g_ref[...]
        hl = jnp.dot(x, w1l_ref[...], preferred_element_type=F32) + b1l_ref[...]
        glu = jnp.minimum(hg, SWIGLU_LIMIT)
        lin = jnp.clip(hl, -SWIGLU_LIMIT, SWIGLU_LIMIT)
        act = glu * jax.nn.sigmoid(SWIGLU_ALPHA * glu) * (lin + 1.0)
        y = jnp.dot(act.astype(BF16), w2p_ref[...], preferred_element_type=F32) + b2_ref[...]
        _store_token_tiles(ys_ref, y)

    @pl.when(nvalid <= 0)
    def _():
        ys_ref[...] = jnp.zeros_like(ys_ref)


def _experts(tile_expert, tile_valid, tile_slot, tile_next, xs, w1, b1g, b1l, w2, b2):
    f, d = w2.shape[1:]
    p = xs.shape[0] // SUBLANES
    tm = EXP_TM
    wmap = lambda t, te, tv, ts, tn: (te[t], 0, 0)
    tile = lambda t, te, tv, ts, tn: (t, 0)
    return pl.pallas_call(
        _expert_kernel,
        grid_spec=pltpu.PrefetchScalarGridSpec(
            num_scalar_prefetch=4,
            grid=(p // tm,),
            in_specs=[
                pl.BlockSpec((tm * SUBLANES, LANES), tile),
                pl.BlockSpec(memory_space=pl.ANY),
                pl.BlockSpec((None, 1, f), wmap), pl.BlockSpec((None, 1, f), wmap),
                pl.BlockSpec(memory_space=pl.ANY), pl.BlockSpec((None, 1, d), wmap),
                pl.BlockSpec((2 * LANES, 2 * LANES), lambda t, te, tv, ts, tn: (0, 0)),
            ],
            out_specs=pl.BlockSpec((tm * SUBLANES, LANES), tile),
            scratch_shapes=[pltpu.VMEM((d, f), BF16), pltpu.VMEM((d, f), BF16),
                            pltpu.VMEM((f, d), BF16),
                            pltpu.VMEM((2, d, 2 * f), F32), pltpu.VMEM((2, f, d), F32),
                            pltpu.SemaphoreType.DMA((2,)), pltpu.SemaphoreType.DMA((2,))],
        ),
        out_shape=jax.ShapeDtypeStruct((p * SUBLANES, LANES), F32),
        compiler_params=_cparams(("arbitrary",)),
        name="experts",
    )(tile_expert, tile_valid, tile_slot, tile_next, xs, w1, b1g, b1l, w2, b2, _deinterleave_matrix())


def _ple_kernel(pos_hbm, ys_hbm, topw_ref, x1_ref, p_ref, lnp_ref, wpg_ref, wpp_ref, lnf_ref, o_ref,
                pos0, pos1, ybuf0, ybuf1, sem_pos, sem_rows):
    i = pl.program_id(0)
    nsteps = pl.num_programs(0)
    tb = x1_ref.shape[0] // 2
    ch = tb * TOP_K
    nblocks = 2 * nsteps
    pos_bufs = (pos0, pos1)
    ybufs = (ybuf0, ybuf1)

    def pos_copy(block, par):
        blk = jnp.minimum(block, nblocks - 1)
        return pltpu.make_async_copy(pos_hbm.at[pl.ds(blk * ch, ch)], pos_bufs[par], sem_pos.at[par])

    def issue_gathers(par):
        def body(t, carry):
            for kk in range(TOP_K):
                _token_copy(ys_hbm, pos_bufs[par][kk * tb + t], ybufs[par].at[kk], t,
                            sem_rows.at[par]).start(priority=kk % 2)
            return carry
        lax.fori_loop(0, tb, body, 0, unroll=True)

    def wait_gathers(par):
        for kk in range(TOP_K):
            _tokens_wait(ybufs[par].at[kk], tb, sem_rows.at[par])

    def compute(par):
        rows = slice(par * tb, (par + 1) * tb)
        topw = topw_ref[rows, :]
        groups = x1_ref.shape[1] // LANES
        moe = functools.reduce(jnp.add, [
            topw[:, kk:kk + 1] * _load_token_tiles(ybufs[par].at[kk], tb, groups)
            for kk in range(TOP_K)])
        x2 = x1_ref[rows, :] + moe
        gate = jax.nn.sigmoid(jnp.dot(_rms(x2, lnp_ref[...]).astype(BF16), wpg_ref[...],
                                      preferred_element_type=F32))
        proj = jnp.dot(p_ref[rows, :].astype(BF16), wpp_ref[...], preferred_element_type=F32)
        o_ref[rows, :] = _rms(x2 + gate * proj, lnf_ref[...])

    @pl.when(i == 0)
    def _():
        pos_copy(0, 0).start()
        pos_copy(0, 0).wait()
        issue_gathers(0)
        pos_copy(1, 1).start()

    pos_copy(2 * i + 1, 1).wait()
    wait_gathers(0)
    pos_copy(2 * i + 2, 0).start()
    issue_gathers(1)
    compute(0)
    pos_copy(2 * i + 2, 0).wait()
    wait_gathers(1)
    pos_copy(2 * i + 3, 1).start()
    issue_gathers(0)
    compute(1)

    @pl.when(i == nsteps - 1)
    def _():
        wait_gathers(0)
        pos_copy(0, 1).wait()


def _ple(pos_flat, ys, topw, x1, p2, ln_ple, wpg, wpp, ln_final):
    n, d = x1.shape
    pd = p2.shape[1]
    tb = PLE_TB
    row = lambda i: (i, 0)
    const = lambda i: (0, 0)
    return pl.pallas_call(
        _ple_kernel,
        grid=(n // (2 * tb),),
        in_specs=[
            pl.BlockSpec(memory_space=pl.ANY), pl.BlockSpec(memory_space=pl.ANY),
            pl.BlockSpec((2 * tb, LANES), row), pl.BlockSpec((2 * tb, d), row),
            pl.BlockSpec((2 * tb, pd), row),
            pl.BlockSpec((1, d), const), pl.BlockSpec((d, d), const),
            pl.BlockSpec((pd, d), const), pl.BlockSpec((1, d), const),
        ],
        out_specs=pl.BlockSpec((2 * tb, d), row),
        out_shape=jax.ShapeDtypeStruct((n, d), F32),
        scratch_shapes=[
            pltpu.SMEM((tb * TOP_K,), I32), pltpu.SMEM((tb * TOP_K,), I32),
            pltpu.VMEM((TOP_K, tb * d // LANES, LANES), F32),
            pltpu.VMEM((TOP_K, tb * d // LANES, LANES), F32),
            pltpu.SemaphoreType.DMA((2,)),
            pltpu.SemaphoreType.DMA((2,)),
        ],
        compiler_params=_cparams(("arbitrary",)),
        name="combine_ple_final",
    )(pos_flat, ys, topw, x1, p2, ln_ple, wpg, wpp, ln_final)


def _layer(x2, p2, bsz, seq, ln_mix, w_in, w_gk, b_gk, gla_norm, rel_bias, w_out, ln_moe,
           w_router, b_router, w1, b1, w2, b2, ln_ple, w_ple_gate, w_ple_proj, ln_out):
    n, d = x2.shape
    qk, gv, aw = GLA_HEADS * GLA_DK, GLA_HEADS * GLA_DV, ATT_HEADS * ATT_DH
    names = ("q_g", "k_g", "v_g", "gk_low", "r_g", "q_a", "k_a", "v_a", "gt_a", "gt_b")
    widths = (qk, qk, gv, GLA_RANK, gv, aw, aw, aw, d, d)
    src = dict(zip(names, np.cumsum((0,) + widths[:-1]).tolist()))
    wid = dict(zip(names, widths))
    order = [nm for nm in names if nm != "gk_low"]
    col, off = {}, 0
    for nm in order:
        col[nm] = off
        off += wid[nm]
    lo, hi = src["gk_low"], src["gk_low"] + GLA_RANK
    assert lo % IN_TN == 0 and (w_in.shape[1] - hi) % IN_TN == 0
    w_low = jnp.pad(w_in[:, lo:hi], ((0, 0), (0, LANES - GLA_RANK))).astype(BF16)

    z, gk = _in_proj(x2, ln_mix.reshape(1, d), w_in[:, :lo].astype(BF16), w_in[:, hi:].astype(BF16), w_low)
    z3 = z.reshape(bsz, seq, -1)
    gk3 = gk.reshape(bsz, seq, LANES)

    wgk_pad = jnp.pad(w_gk, ((0, LANES - GLA_RANK), (0, 0)))
    ya = _gla(z3, gk3, wgk_pad, b_gk.reshape(1, qk), gla_norm.reshape(1, GLA_DV), col)
    yb = _attn(z3, _attn_bias(rel_bias), col)

    wr_pad = jnp.pad(w_router, ((0, 0), (0, LANES - N_EXPERTS)))
    br_pad = jnp.pad(b_router, (0, LANES - N_EXPERTS)).reshape(1, LANES)
    x1, xm, topi, topw, cnt = _outproj(x2, ya.reshape(n, d), yb.reshape(n, d), w_out.astype(BF16),
                                       ln_moe.reshape(1, d), wr_pad, br_pad)

    counts = cnt[0, :N_EXPERTS].astype(I32)
    ntile = (counts + EXP_TM - 1) // EXP_TM
    tile_end = jnp.cumsum(ntile)
    tile_start = tile_end - ntile
    rows_total = n * TOP_K + N_EXPERTS * EXP_TM
    tiles = jnp.arange(rows_total // EXP_TM, dtype=I32)
    onehot = (tiles[:, None] >= tile_start[None, :]) & (tiles[:, None] < tile_end[None, :])
    te = jnp.sum(jnp.where(onehot, jnp.arange(N_EXPERTS, dtype=I32)[None, :], 0), axis=1)
    tv = jnp.sum(jnp.where(onehot, counts[None, :] - (tiles[:, None] - tile_start[None, :]) * EXP_TM, 0),
                 axis=1)
    te = jnp.where(tiles < tile_end[-1], te, N_EXPERTS - 1).astype(I32)
    tv = jnp.clip(tv, 0, EXP_TM).astype(I32)
    start_rows = jnp.pad((tile_start * EXP_TM).astype(F32), (0, LANES - N_EXPERTS)).reshape(1, LANES)
    active = ntile > 0
    eid = jnp.arange(N_EXPERTS, dtype=I32)
    slot_e = (jnp.cumsum(active.astype(I32)) - 1) % 2
    later = (eid[None, :] > eid[:, None]) & active[None, :]
    next_e = jnp.min(jnp.where(later, eid[None, :], N_EXPERTS), axis=1)
    next_e = jnp.where(next_e < N_EXPERTS, next_e, -1)
    tslot = slot_e[te].astype(I32)
    tnext = next_e[te].astype(I32)

    pos_t = _positions(topi, start_rows)[:TOP_K]

    def pos_blocks(tokens):
        return pos_t.reshape(TOP_K, n // tokens, tokens).transpose(1, 0, 2).reshape(-1)

    xs = _dispatch(tv, pos_blocks(DISPATCH_TOKENS), xm, rows_total)

    ys = _experts(te, tv, tslot, tnext, xs, w1,
                  b1[:, 0::2].reshape(N_EXPERTS, 1, -1), b1[:, 1::2].reshape(N_EXPERTS, 1, -1),
                  w2, b2.reshape(N_EXPERTS, 1, d))

    return _ple(pos_blocks(PLE_TB), ys, topw, x1, p2, ln_ple.reshape(1, d), w_ple_gate.astype(BF16),
                w_ple_proj.astype(BF16), ln_out.reshape(1, d))


def kernel(x, p, ln_mix, w_in, w_gk, b_gk, gla_norm, rel_bias, w_out, ln_moe, w_router, b_router,
           w1, b1, w2, b2, ln_ple, w_ple_gate, w_ple_proj, ln_final):
    bsz, seq, d = x.shape
    depth = p.shape[0]
    assert depth == 1, "the final RMSNorm is fused into the last layer's kernel"
    assert d == SUBLANES * LANES, "token-tile layout needs one vreg tile per token"
    assert seq % max(ATT_TQ, GLA_TT) == 0
    assert (bsz * seq) % max(IN_TM, OUT_TM, RANK_TB, DISPATCH_TOKENS, 2 * PLE_TB) == 0
    x2 = x.reshape(bsz * seq, d)
    out = _layer(x2, p[0].reshape(bsz * seq, -1), bsz, seq, ln_mix[0], w_in[0], w_gk[0], b_gk[0],
                 gla_norm[0], rel_bias[0], w_out[0], ln_moe[0], w_router[0], b_router[0],
                 w1[0], b1[0], w2[0], b2[0], ln_ple[0], w_ple_gate[0], w_ple_proj[0], ln_final)
    return out.reshape(bsz, seq, d)
```

```python
import functools

import numpy as np
import jax
import jax.numpy as jnp
from jax import lax
from jax.experimental import pallas as pl
from jax.experimental.pallas import tpu as pltpu

F32 = jnp.float32
BF16 = jnp.bfloat16
I32 = jnp.int32

LANES = 128
SUBLANES = 8
CHUNK = 64
GLA_HEADS = 4
GLA_DK = 128
GLA_DV = 256
GLA_RANK = 16
GLA_TAU = 16.0
ATT_HEADS = 16
ATT_DH = 64
ATT_PAST = 8
REL_CLIP = 256
N_EXPERTS = 32
TOP_K = 4
SWIGLU_ALPHA = 1.702
SWIGLU_LIMIT = 7.0
EPS = 1e-6
LOG2E = 1.4426950408889634

VMEM_LIMIT = 48 * 1024 * 1024

IN_TM, IN_TN = 512, 1024
GLA_TT = 256
ATT_TQ = 256
ATT_STEP_HEADS = 16
OUT_TM = 1024
RANK_TB = 1024
EXP_TM = 512
DISPATCH_TOKENS = 2048
PLE_TB = 256
DMA_UNROLL = 8


def _cparams(sem):
    return pltpu.CompilerParams(dimension_semantics=sem, vmem_limit_bytes=VMEM_LIMIT)


def _split_bf16(a):
    hi = a.astype(BF16)
    lo = (a - hi.astype(F32)).astype(BF16)
    return hi, lo


def _dot_split(a, b):
    a_hi, a_lo = _split_bf16(a)
    b_hi, b_lo = _split_bf16(b)
    d = functools.partial(jnp.dot, preferred_element_type=F32)
    return d(a_hi, b_hi) + (d(a_hi, b_lo) + d(a_lo, b_hi))


def _rms(x, gain):
    ms = jnp.mean(x * x, axis=-1, keepdims=True)
    return x * lax.rsqrt(ms + EPS) * gain


def _store_token_tiles(ref, val):
    groups = val.shape[1] // LANES
    for c in range(groups):
        ref[pl.ds(c, val.shape[0], stride=groups), :] = val[:, c * LANES:(c + 1) * LANES]


def _load_token_tiles(ref, rows, groups):
    return jnp.concatenate([ref[pl.ds(c, rows, stride=groups), :] for c in range(groups)], axis=1)


_NT = (((1,), (1,)), ((), ()))
_TN = (((0,), (0,)), ((), ()))


def _in_proj_kernel(x_ref, g_ref, wa_ref, wb_ref, wlow_ref, z_ref, gk_ref):
    xn = _rms(x_ref[...], g_ref[...]).astype(BF16)
    gk_ref[...] = jnp.dot(xn, wlow_ref[...], preferred_element_type=F32)
    off = 0
    for w_ref in (wa_ref, wb_ref):
        for j in range(w_ref.shape[1] // IN_TN):
            z_ref[:, off:off + IN_TN] = jnp.dot(xn, w_ref[:, j * IN_TN:(j + 1) * IN_TN],
                                                preferred_element_type=F32).astype(BF16)
            off += IN_TN


def _in_proj(x2, ln, w_a, w_b, w_low):
    n, d = x2.shape
    ncol = w_a.shape[1] + w_b.shape[1]
    once = pl.Buffered(1)
    return pl.pallas_call(
        _in_proj_kernel,
        grid=(n // IN_TM,),
        in_specs=[
            pl.BlockSpec((IN_TM, d), lambda i: (i, 0)),
            pl.BlockSpec((1, d), lambda i: (0, 0)),
            pl.BlockSpec(w_a.shape, lambda i: (0, 0), pipeline_mode=once),
            pl.BlockSpec(w_b.shape, lambda i: (0, 0), pipeline_mode=once),
            pl.BlockSpec((d, LANES), lambda i: (0, 0), pipeline_mode=once),
        ],
        out_specs=[
            pl.BlockSpec((IN_TM, ncol), lambda i: (i, 0)),
            pl.BlockSpec((IN_TM, LANES), lambda i: (i, 0)),
        ],
        out_shape=[
            jax.ShapeDtypeStruct((n, ncol), BF16),
            jax.ShapeDtypeStruct((n, LANES), F32),
        ],
        compiler_params=_cparams(("parallel",)),
        name="in_proj",
    )(x2, ln, w_a, w_b, w_low)


_GLA_LEVELS = (8, 16, 32)
_GLA_BOT = 8


def _gla_consts(tt):
    t = np.arange(tt)
    same_chunk = (t[:, None] // CHUNK) == (t[None, :] // CHUNK)
    tri = (same_chunk & (t[None, :] <= t[:, None])).astype(np.float32)
    mlev = []
    for h in _GLA_LEVELS:
        blk = (t[:, None] // (2 * h)) == (t[None, :] // (2 * h))
        m = blk & ((t[:, None] % (2 * h)) >= h) & ((t[None, :] % (2 * h)) < h)
        mlev.append(m.astype(np.float32))
    mbot = (((t[None, :] // _GLA_BOT) == (t[:, None] // _GLA_BOT))
            & ((t[None, :] % _GLA_BOT) <= (t[:, None] % _GLA_BOT)))
    spread = (np.arange(_GLA_BOT * GLA_DK)[:, None] // GLA_DK) == (t[None, :] % _GLA_BOT)
    return (jnp.asarray(tri, BF16), jnp.asarray(np.stack(mlev), F32),
            jnp.asarray(mbot.astype(np.float32), F32), jnp.asarray(spread.astype(np.float32), BF16))


def _gla_kernel(q_ref, k_ref, v_ref, r_ref, ga_ref, gk_ref, wgk_ref, bgk_ref, gn_ref,
                tri_ref, mlev_ref, mbot_ref, ones_ref, o_ref, st_ref):
    @pl.when(pl.program_id(1) == 0)
    def _():
        st_ref[...] = jnp.zeros_like(st_ref)

    xg = _dot_split(gk_ref[...], wgk_ref[...]) + bgk_ref[...]
    g = -(jnp.maximum(-xg, 0.0) + jnp.log(1.0 + jnp.exp(-jnp.abs(xg)))) * (1.0 / GLA_TAU)
    g_hi, g_lo = _split_bf16(g)
    tri = tri_ref[...]
    b_all = (jnp.dot(tri, g_hi, preferred_element_type=F32)
             + jnp.dot(tri, g_lo, preferred_element_type=F32))

    for hh in range(GLA_HEADS):
        kc = slice(hh * GLA_DK, (hh + 1) * GLA_DK)
        vc = slice(hh * GLA_DV, (hh + 1) * GLA_DV)
        y, st = _gla_head(q_ref[:, kc], k_ref[:, kc], v_ref[:, vc], b_all[:, kc], st_ref[hh],
                          mlev_ref, mbot_ref, ones_ref[...])
        st_ref[hh] = st
        r = r_ref[:, vc].astype(F32)
        ya = _rms(y, gn_ref[...]) * (r * jax.nn.sigmoid(r))
        o_ref[:, vc] = (jax.nn.sigmoid(ga_ref[:, vc].astype(F32)) * ya).astype(BF16)


def _gla_head(q, k, v, b, st, mlev_ref, mbot_ref, ones):
    tt = q.shape[0]
    q = q.astype(F32) * (GLA_DK ** -0.5)
    k = k.astype(F32)
    b = b * LOG2E

    s_intra = jnp.zeros((tt, tt), F32)
    for li, h in enumerate(_GLA_LEVELS):
        b3 = b.reshape(tt // (2 * h), 2 * h, GLA_DK)
        e = jnp.exp2(-jnp.abs(b3 - b3[:, h - 1:h, :])).reshape(tt, GLA_DK)
        sc = lax.dot_general((q * e).astype(BF16), (k * e).astype(BF16), _NT,
                             preferred_element_type=F32)
        s_intra = s_intra + sc * mlev_ref[li]
    nb = tt // _GLA_BOT
    b3 = b.reshape(nb, _GLA_BOT, GLA_DK)
    q3 = q.reshape(nb, _GLA_BOT, GLA_DK)
    k3 = k.reshape(nb, _GLA_BOT, GLA_DK)
    prods = []
    for s in range(_GLA_BOT):
        e = jnp.exp2(jnp.minimum(b3 - b3[:, s:s + 1, :], 0.0))
        prods.append((q3 * k3[:, s:s + 1, :] * e).reshape(tt, GLA_DK).astype(BF16))
    s_intra = s_intra + jnp.dot(jnp.concatenate(prods, axis=1), ones,
                                preferred_element_type=F32) * mbot_ref[...]
    o_intra = jnp.dot(s_intra.astype(BF16), v, preferred_element_type=F32)

    outs = []
    for c in range(tt // CHUNK):
        lo = c * CHUNK
        bc = b[lo:lo + CHUNK]
        bl = b[lo + CHUNK - 1:lo + CHUNK]
        qe = (q[lo:lo + CHUNK] * jnp.exp2(bc)).astype(BF16)
        outs.append(lax.dot_general(qe, st.astype(BF16), _NT, preferred_element_type=F32))
        kd = (k[lo:lo + CHUNK] * jnp.exp2(bl - bc)).astype(BF16)
        upd = lax.dot_general(v[lo:lo + CHUNK], kd, _TN, preferred_element_type=F32)
        st = st * jnp.exp2(bl) + upd
    return o_intra + jnp.concatenate(outs, axis=0), st


def _gla(z3, gk3, wgk_pad, bgk, gnorm, col):
    b, t, _ = z3.shape
    tt = GLA_TT
    tri, mlev, mbot, ones = _gla_consts(tt)
    qk, gv = GLA_HEADS * GLA_DK, GLA_HEADS * GLA_DV

    def zspec(width, off):
        return pl.BlockSpec((None, tt, width), lambda bi, ti, o=off // width: (bi, ti, o))

    const2 = lambda bi, ti: (0, 0)
    const3 = lambda bi, ti: (0, 0, 0)
    return pl.pallas_call(
        _gla_kernel,
        grid=(b, t // tt),
        in_specs=[
            zspec(qk, col["q_g"]), zspec(qk, col["k_g"]), zspec(gv, col["v_g"]),
            zspec(gv, col["r_g"]), zspec(gv, col["gt_a"]),
            pl.BlockSpec((None, tt, LANES), lambda bi, ti: (bi, ti, 0)),
            pl.BlockSpec((LANES, qk), const2),
            pl.BlockSpec((1, qk), const2),
            pl.BlockSpec((1, GLA_DV), const2),
            pl.BlockSpec((tt, tt), const2),
            pl.BlockSpec((len(_GLA_LEVELS), tt, tt), const3),
            pl.BlockSpec((tt, tt), const2),
            pl.BlockSpec((_GLA_BOT * GLA_DK, tt), const2),
        ],
        out_specs=pl.BlockSpec((None, tt, gv), lambda bi, ti: (bi, ti, 0)),
        out_shape=jax.ShapeDtypeStruct((b, t, gv), BF16),
        scratch_shapes=[pltpu.VMEM((GLA_HEADS, GLA_DV, GLA_DK), F32)],
        compiler_params=_cparams(("parallel", "arbitrary")),
        name="gla",
    )(z3, z3, z3, z3, z3, gk3, wgk_pad, bgk, gnorm, tri, mlev, mbot, ones)


_ATT_NKB = 3


def _attn_bias(rel_bias):
    tq = ATT_TQ
    nk = _ATT_NKB * tq
    nheads = rel_bias.shape[0]
    band = (ATT_PAST + 1) * CHUNK
    assert nk - tq == ATT_PAST * CHUNK and tq % CHUNK == 0
    span = band + CHUNK - 1
    dist = np.clip(np.arange(span) - (CHUNK - 1), -REL_CLIP, REL_CLIP) + REL_CLIP
    g = rel_bias.astype(F32)[:, dist]
    x = jnp.pad(g[:, ::-1], ((0, 0), (0, 1)))
    base = jnp.tile(x, (1, CHUNK))[:, :CHUNK * span].reshape(nheads, CHUNK, span)[:, :, CHUNK - 1:CHUNK - 1 + band]
    tab = jnp.concatenate(
        [jnp.pad(base, ((0, 0), (0, 0), (ci * CHUNK, nk - band - ci * CHUNK)), constant_values=-jnp.inf)
         for ci in range(tq // CHUNK)], axis=1)
    present = np.stack([np.arange(nk) // tq >= _ATT_NKB - 1 - e for e in range(_ATT_NKB)])
    return jnp.where(jnp.asarray(present)[:, None, None, :], tab[None], -jnp.inf)


def _attn_kernel(q_ref, k0_ref, k1_ref, k2_ref, v0_ref, v1_ref, v2_ref, gb_ref, bias_ref, o_ref):
    tq = q_ref.shape[0]
    first = lax.broadcasted_iota(I32, (1, LANES), 1) < ATT_DH
    scale = jnp.asarray(ATT_DH ** -0.5, BF16)
    krefs = (k0_ref, k1_ref, k2_ref)
    vrefs = (v0_ref, v1_ref, v2_ref)
    for lb in range(q_ref.shape[1] // LANES):
        cols = slice(lb * LANES, (lb + 1) * LANES)
        q = q_ref[:, cols] * scale
        zero = jnp.zeros_like(q)
        qs = jnp.concatenate([jnp.where(first, q, zero), jnp.where(first, zero, q)], axis=0)
        s = jnp.concatenate(
            [lax.dot_general(qs, r[:, cols], _NT, preferred_element_type=F32) for r in krefs], axis=1)
        s = s + jnp.concatenate([bias_ref[2 * lb], bias_ref[2 * lb + 1]], axis=0)
        p = jnp.exp(s - jnp.max(s, axis=-1, keepdims=True))
        l = jnp.sum(p, axis=-1, keepdims=True)
        pb = p.astype(BF16)
        pv = functools.reduce(jnp.add, [
            jnp.dot(pb[:, i * tq:(i + 1) * tq], vrefs[i][:, cols], preferred_element_type=F32)
            for i in range(_ATT_NKB)]) / l
        o = jnp.where(first, pv[:tq], pv[tq:])
        o_ref[:, cols] = (jax.nn.sigmoid(gb_ref[:, cols].astype(F32)) * o).astype(BF16)


def _attn(z3, bias, col):
    b, t, _ = z3.shape
    tq = ATT_TQ
    width = ATT_STEP_HEADS * ATT_DH
    steps = ATT_HEADS // ATT_STEP_HEADS

    def cur(off):
        return pl.BlockSpec((None, tq, width), lambda h, j, bi, o=off // width: (bi, j, o + h))

    def past(off, back):
        return pl.BlockSpec((None, tq, width),
                            lambda h, j, bi, o=off // width: (bi, jnp.maximum(j - back, 0), o + h))

    return pl.pallas_call(
        _attn_kernel,
        grid=(steps, t // tq, b),
        in_specs=[
            cur(col["q_a"]),
            past(col["k_a"], 2), past(col["k_a"], 1), cur(col["k_a"]),
            past(col["v_a"], 2), past(col["v_a"], 1), cur(col["v_a"]),
            cur(col["gt_b"]),
            pl.BlockSpec((None, ATT_STEP_HEADS, tq, _ATT_NKB * tq),
                         lambda h, j, bi: (jnp.minimum(j, _ATT_NKB - 1), h, 0, 0)),
        ],
        out_specs=pl.BlockSpec((None, tq, width), lambda h, j, bi: (bi, j, h)),
        out_shape=jax.ShapeDtypeStruct((b, t, ATT_HEADS * ATT_DH), BF16),
        compiler_params=_cparams(("parallel", "parallel", "parallel")),
        name="band_attn",
    )(z3, z3, z3, z3, z3, z3, z3, z3, bias)


def _outproj_kernel(x_ref, ya_ref, yb_ref, wo_ref, lnm_ref, wr_ref, br_ref,
                    x1_ref, xm_ref, topi_ref, topw_ref, cnt_ref):
    h = (ya_ref[...].astype(F32) + yb_ref[...].astype(F32)).astype(BF16)
    x1 = x_ref[...] + jnp.dot(h, wo_ref[...], preferred_element_type=F32)
    x1_ref[...] = x1
    xm = _rms(x1, lnm_ref[...])
    _store_token_tiles(xm_ref, xm)

    tm = xm.shape[0]
    lane = lax.broadcasted_iota(I32, (tm, LANES), 1)
    lanef = lane.astype(F32)
    logits = _dot_split(xm, wr_ref[...]) + br_ref[...]
    l = jnp.where(lane < N_EXPERTS, logits, -jnp.inf)
    vals, idxs = [], []
    picked = jnp.zeros((tm, LANES), F32)
    for _ in range(TOP_K):
        m = jnp.max(l, axis=-1, keepdims=True)
        idx = jnp.min(jnp.where(l == m, lanef, float(LANES)), axis=-1, keepdims=True)
        vals.append(m)
        idxs.append(idx)
        hit = lanef == idx
        picked = jnp.where(hit, 1.0, picked)
        l = jnp.where(hit, -jnp.inf, l)
    es = [jnp.exp(vv - vals[0]) for vv in vals]
    tot = functools.reduce(jnp.add, es)
    topw = jnp.zeros((tm, LANES), F32)
    topi = jnp.zeros((tm, LANES), F32)
    for kk in range(TOP_K):
        topw = jnp.where(lane == kk, es[kk] / tot, topw)
        topi = jnp.where(lane == kk, idxs[kk], topi)
    topw_ref[...] = topw
    topi_ref[...] = topi.astype(I32)

    @pl.when(pl.program_id(0) == 0)
    def _():
        cnt_ref[...] = jnp.zeros_like(cnt_ref)

    cnt_ref[...] = cnt_ref[...] + jnp.sum(picked, axis=0, keepdims=True)


def _outproj(x2, ya, yb, w_out, ln_moe, wr_pad, br_pad):
    n, d = x2.shape
    tm = OUT_TM
    row = lambda i: (i, 0)
    const = lambda i: (0, 0)
    return pl.pallas_call(
        _outproj_kernel,
        grid=(n // tm,),
        in_specs=[
            pl.BlockSpec((tm, d), row), pl.BlockSpec((tm, d), row), pl.BlockSpec((tm, d), row),
            pl.BlockSpec((d, d), const), pl.BlockSpec((1, d), const),
            pl.BlockSpec((d, LANES), const), pl.BlockSpec((1, LANES), const),
        ],
        out_specs=[
            pl.BlockSpec((tm, d), row), pl.BlockSpec((tm * d // LANES, LANES), row),
            pl.BlockSpec((tm, LANES), row), pl.BlockSpec((tm, LANES), row),
            pl.BlockSpec((1, LANES), const),
        ],
        out_shape=[
            jax.ShapeDtypeStruct((n, d), F32), jax.ShapeDtypeStruct((n * d // LANES, LANES), F32),
            jax.ShapeDtypeStruct((n, LANES), I32), jax.ShapeDtypeStruct((n, LANES), F32),
            jax.ShapeDtypeStruct((1, LANES), F32),
        ],
        compiler_params=_cparams(("arbitrary",)),
        name="out_proj_router",
    )(x2, ya, yb, w_out, ln_moe, wr_pad, br_pad)


def _pos_kernel(topi_ref, start_ref, tri_ref, pos_ref, carry_ref):
    @pl.when(pl.program_id(0) == 0)
    def _():
        carry_ref[...] = start_ref[...]

    ti = topi_ref[...]
    tb = ti.shape[0]
    lane = lax.broadcasted_iota(I32, (tb, LANES), 1)
    sel = [lane == ti[:, kk:kk + 1] for kk in range(TOP_K)]
    oh = functools.reduce(jnp.add, [jnp.where(s, 1.0, 0.0) for s in sel])
    row = carry_ref[...] + jnp.dot(tri_ref[...], oh.astype(BF16), preferred_element_type=F32)
    out = jnp.zeros((tb, LANES), F32)
    for kk in range(TOP_K):
        rk = jnp.sum(jnp.where(sel[kk], row, 0.0), axis=-1, keepdims=True)
        out = jnp.where(lane == kk, rk, out)
    pos_ref[...] = jnp.transpose(out)[:pos_ref.shape[0]].astype(I32)
    carry_ref[...] = carry_ref[...] + jnp.sum(oh, axis=0, keepdims=True)


def _positions(topi, start_rows):
    n = topi.shape[0]
    tb = RANK_TB
    t = np.arange(tb)
    tri = jnp.asarray((t[None, :] < t[:, None]).astype(np.float32), BF16)
    return pl.pallas_call(
        _pos_kernel,
        grid=(n // tb,),
        in_specs=[pl.BlockSpec((tb, LANES), lambda i: (i, 0)),
                  pl.BlockSpec((1, LANES), lambda i: (0, 0)),
                  pl.BlockSpec((tb, tb), lambda i: (0, 0))],
        out_specs=pl.BlockSpec((SUBLANES, tb), lambda i: (0, i)),
        out_shape=jax.ShapeDtypeStruct((SUBLANES, n), I32),
        scratch_shapes=[pltpu.VMEM((1, LANES), F32)],
        compiler_params=_cparams(("arbitrary",)),
        name="route_positions",
    )(topi, start_rows, tri)


def _token_copy(src, src_tok, dst, dst_tok, sem):
    s0 = pl.multiple_of(src_tok * SUBLANES, SUBLANES)
    d0 = pl.multiple_of(dst_tok * SUBLANES, SUBLANES)
    return pltpu.make_async_copy(src.at[pl.ds(s0, SUBLANES)], dst.at[pl.ds(d0, SUBLANES)], sem)


def _tokens_wait(ref, ntok, sem):
    pltpu.make_async_copy(ref.at[pl.ds(0, ntok * SUBLANES)], ref.at[pl.ds(0, ntok * SUBLANES)], sem).wait()


def _dispatch_kernel(tv_ref, pos_hbm, xm_ref, xs_hbm, pos_smem, zeros_ref, sem_pos, sem_rows, sem_fill):
    i = pl.program_id(0)
    nsteps = pl.num_programs(0)
    tokens = xm_ref.shape[0] // SUBLANES
    ch = tokens * TOP_K
    slot = i % 2

    def pos_copy(step, sl):
        return pltpu.make_async_copy(pos_hbm.at[pl.ds(step * ch, ch)], pos_smem.at[pl.ds(sl * ch, ch)],
                                     sem_pos.at[sl])

    @pl.when(i == 0)
    def _():
        pos_copy(0, 0).start()
        tm = zeros_ref.shape[0]
        zeros_ref[...] = jnp.zeros_like(zeros_ref)

        def fill(t):
            return pltpu.make_async_copy(zeros_ref, xs_hbm.at[pl.ds(t * tm, tm)], sem_fill)

        def start(t, carry):
            @pl.when(tv_ref[t] < EXP_TM)
            def _():
                fill(t).start()
            return carry

        def wait(t, carry):
            @pl.when(tv_ref[t] < EXP_TM)
            def _():
                fill(t).wait()
            return carry

        ntiles = xs_hbm.shape[0] // tm
        lax.fori_loop(0, ntiles, start, 0)
        lax.fori_loop(0, ntiles, wait, 0)

    @pl.when(i + 1 < nsteps)
    def _():
        pos_copy(i + 1, 1 - slot).start()

    pos_copy(i, slot).wait()

    def body(t, carry):
        for kk in range(TOP_K):
            _token_copy(xm_ref, t, xs_hbm, pos_smem[slot * ch + kk * tokens + t],
                        sem_rows).start(priority=kk % 2)
        return carry

    lax.fori_loop(0, tokens, body, 0, unroll=DMA_UNROLL)
    _tokens_wait(xs_hbm, ch, sem_rows)


def _dispatch(tile_valid, pos_flat, xm, rows_total):
    tokens = DISPATCH_TOKENS
    n = xm.shape[0] // SUBLANES
    return pl.pallas_call(
        _dispatch_kernel,
        grid_spec=pltpu.PrefetchScalarGridSpec(
            num_scalar_prefetch=1,
            grid=(n // tokens,),
            in_specs=[pl.BlockSpec(memory_space=pl.ANY),
                      pl.BlockSpec((tokens * SUBLANES, LANES), lambda i, tv: (i, 0))],
            out_specs=pl.BlockSpec(memory_space=pl.ANY),
            scratch_shapes=[pltpu.SMEM((2 * tokens * TOP_K,), I32),
                            pltpu.VMEM((EXP_TM * SUBLANES, LANES), F32),
                            pltpu.SemaphoreType.DMA((2,)), pltpu.SemaphoreType.DMA,
                            pltpu.SemaphoreType.DMA],
        ),
        out_shape=jax.ShapeDtypeStruct((rows_total * SUBLANES, LANES), F32),
        compiler_params=_cparams(("arbitrary",)),
        name="dispatch",
    )(tile_valid, pos_flat, xm)


def _deinterleave_matrix():
    j = np.arange(2 * LANES)[:, None]
    c = np.arange(2 * LANES)[None, :]
    sel = np.where(c < LANES, j == 2 * c, j == 2 * (c - LANES) + 1)
    return jnp.asarray(sel.astype(np.float32), BF16)


def _expert_kernel(te_ref, tv_ref, slot_ref, next_ref, xs_ref, w1_hbm, b1g_ref, b1l_ref, w2_hbm, b2_ref,
                   sel_ref, ys_ref, w1g_ref, w1l_ref, w2p_ref, w1buf, w2buf, sem_w1, sem_w2):
    t = pl.program_id(0)
    nvalid = tv_ref[t]
    new_expert = (t == 0) | (te_ref[t] != te_ref[jnp.maximum(t - 1, 0)])

    def weight_copies(expert, sl):
        return (pltpu.make_async_copy(w1_hbm.at[expert], w1buf.at[sl], sem_w1.at[sl]),
                pltpu.make_async_copy(w2_hbm.at[expert], w2buf.at[sl], sem_w2.at[sl]))

    @pl.when((nvalid > 0) & new_expert)
    def _():
        sl = slot_ref[t]

        @pl.when(t == 0)
        def _():
            for cp in weight_copies(te_ref[t], sl):
                cp.start()

        for cp in weight_copies(te_ref[t], sl):
            cp.wait()

        @pl.when(next_ref[t] >= 0)
        def _():
            for cp in weight_copies(next_ref[t], 1 - sl):
                cp.start()

        sel = sel_ref[...]
        for m in range(w1g_ref.shape[1] // LANES):
            pair = w1buf[sl, :, 2 * m * LANES:(2 * m + 2) * LANES].astype(BF16)
            split = jnp.dot(pair, sel, preferred_element_type=F32).astype(BF16)
            w1g_ref[:, m * LANES:(m + 1) * LANES] = split[:, :LANES]
            w1l_ref[:, m * LANES:(m + 1) * LANES] = split[:, LANES:]
        w2p_ref[...] = w2buf[sl].astype(BF16)

    @pl.when(nvalid > 0)
    def _():
        tm = xs_ref.shape[0] // SUBLANES
        x = _load_token_tiles(xs_ref, tm, SUBLANES).astype(BF16)
        hg = jnp.dot(x, w1g_ref[...], preferred_element_type=F32) + b1g_ref[...]
        hl = jnp.dot(x, w1l_ref[...], preferred_element_type=F32) + b1l_ref[...]
        glu = jnp.minimum(hg, SWIGLU_LIMIT)
        lin = jnp.clip(hl, -SWIGLU_LIMIT, SWIGLU_LIMIT)
        act = glu * jax.nn.sigmoid(SWIGLU_ALPHA * glu) * (lin + 1.0)
        y = jnp.dot(act.astype(BF16), w2p_ref[...], preferred_element_type=F32) + b2_ref[...]
        _store_token_tiles(ys_ref, y)

    @pl.when(nvalid <= 0)
    def _():
        ys_ref[...] = jnp.zeros_like(ys_ref)


def _experts(tile_expert, tile_valid, tile_slot, tile_next, xs, w1, b1g, b1l, w2, b2):
    f, d = w2.shape[1:]
    p = xs.shape[0] // SUBLANES
    tm = EXP_TM
    wmap = lambda t, te, tv, ts, tn: (te[t], 0, 0)
    tile = lambda t, te, tv, ts, tn: (t, 0)
    return pl.pallas_call(
        _expert_kernel,
        grid_spec=pltpu.PrefetchScalarGridSpec(
            num_scalar_prefetch=4,
            grid=(p // tm,),
            in_specs=[
                pl.BlockSpec((tm * SUBLANES, LANES), tile),
                pl.BlockSpec(memory_space=pl.ANY),
                pl.BlockSpec((None, 1, f), wmap), pl.BlockSpec((None, 1, f), wmap),
                pl.BlockSpec(memory_space=pl.ANY), pl.BlockSpec((None, 1, d), wmap),
                pl.BlockSpec((2 * LANES, 2 * LANES), lambda t, te, tv, ts, tn: (0, 0)),
            ],
            out_specs=pl.BlockSpec((tm * SUBLANES, LANES), tile),
            scratch_shapes=[pltpu.VMEM((d, f), BF16), pltpu.VMEM((d, f), BF16),
                            pltpu.VMEM((f, d), BF16),
                            pltpu.VMEM((2, d, 2 * f), F32), pltpu.VMEM((2, f, d), F32),
                            pltpu.SemaphoreType.DMA((2,)), pltpu.SemaphoreType.DMA((2,))],
        ),
        out_shape=jax.ShapeDtypeStruct((p * SUBLANES, LANES), F32),
        compiler_params=_cparams(("arbitrary",)),
        name="experts",
    )(tile_expert, tile_valid, tile_slot, tile_next, xs, w1, b1g, b1l, w2, b2, _deinterleave_matrix())


def _ple_kernel(pos_hbm, ys_hbm, topw_ref, x1_ref, p_ref, lnp_ref, wpg_ref, wpp_ref, lnf_ref, o_ref,
                pos0, pos1, ybuf0, ybuf1, sem_pos, sem_rows):
    i = pl.program_id(0)
    nsteps = pl.num_programs(0)
    tb = x1_ref.shape[0] // 2
    ch = tb * TOP_K
    nblocks = 2 * nsteps
    pos_bufs = (pos0, pos1)
    ybufs = (ybuf0, ybuf1)

    def pos_copy(block, par):
        blk = jnp.minimum(block, nblocks - 1)
        return pltpu.make_async_copy(pos_hbm.at[pl.ds(blk * ch, ch)], pos_bufs[par], sem_pos.at[par])

    def issue_gathers(par):
        def body(t, carry):
            for kk in range(TOP_K):
                _token_copy(ys_hbm, pos_bufs[par][kk * tb + t], ybufs[par].at[kk], t,
                            sem_rows.at[par]).start(priority=kk % 2)
            return carry
        lax.fori_loop(0, tb, body, 0, unroll=True)

    def wait_gathers(par):
        for kk in range(TOP_K):
            _tokens_wait(ybufs[par].at[kk], tb, sem_rows.at[par])

    def compute(par):
        rows = slice(par * tb, (par + 1) * tb)
        topw = topw_ref[rows, :]
        groups = x1_ref.shape[1] // LANES
        moe = functools.reduce(jnp.add, [
            topw[:, kk:kk + 1] * _load_token_tiles(ybufs[par].at[kk], tb, groups)
            for kk in range(TOP_K)])
        x2 = x1_ref[rows, :] + moe
        gate = jax.nn.sigmoid(jnp.dot(_rms(x2, lnp_ref[...]).astype(BF16), wpg_ref[...],
                                      preferred_element_type=F32))
        proj = jnp.dot(p_ref[rows, :].astype(BF16), wpp_ref[...], preferred_element_type=F32)
        o_ref[rows, :] = _rms(x2 + gate * proj, lnf_ref[...])

    @pl.when(i == 0)
    def _():
        pos_copy(0, 0).start()
        pos_copy(0, 0).wait()
        issue_gathers(0)
        pos_copy(1, 1).start()

    pos_copy(2 * i + 1, 1).wait()
    wait_gathers(0)
    pos_copy(2 * i + 2, 0).start()
    issue_gathers(1)
    compute(0)
    pos_copy(2 * i + 2, 0).wait()
    wait_gathers(1)
    pos_copy(2 * i + 3, 1).start()
    issue_gathers(0)
    compute(1)

    @pl.when(i == nsteps - 1)
    def _():
        wait_gathers(0)
        pos_copy(0, 1).wait()


def _ple(pos_flat, ys, topw, x1, p2, ln_ple, wpg, wpp, ln_final):
    n, d = x1.shape
    pd = p2.shape[1]
    tb = PLE_TB
    row = lambda i: (i, 0)
    const = lambda i: (0, 0)
    return pl.pallas_call(
        _ple_kernel,
        grid=(n // (2 * tb),),
        in_specs=[
            pl.BlockSpec(memory_space=pl.ANY), pl.BlockSpec(memory_space=pl.ANY),
            pl.BlockSpec((2 * tb, LANES), row), pl.BlockSpec((2 * tb, d), row),
            pl.BlockSpec((2 * tb, pd), row),
            pl.BlockSpec((1, d), const), pl.BlockSpec((d, d), const),
            pl.BlockSpec((pd, d), const), pl.BlockSpec((1, d), const),
        ],
        out_specs=pl.BlockSpec((2 * tb, d), row),
        out_shape=jax.ShapeDtypeStruct((n, d), F32),
        scratch_shapes=[
            pltpu.SMEM((tb * TOP_K,), I32), pltpu.SMEM((tb * TOP_K,), I32),
            pltpu.VMEM((TOP_K, tb * d // LANES, LANES), F32),
            pltpu.VMEM((TOP_K, tb * d // LANES, LANES), F32),
            pltpu.SemaphoreType.DMA((2,)),
            pltpu.SemaphoreType.DMA((2,)),
        ],
        compiler_params=_cparams(("arbitrary",)),
        name="combine_ple_final",
    )(pos_flat, ys, topw, x1, p2, ln_ple, wpg, wpp, ln_final)


def _layer(x2, p2, bsz, seq, ln_mix, w_in, w_gk, b_gk, gla_norm, rel_bias, w_out, ln_moe,
           w_router, b_router, w1, b1, w2, b2, ln_ple, w_ple_gate, w_ple_proj, ln_out):
    n, d = x2.shape
    qk, gv, aw = GLA_HEADS * GLA_DK, GLA_HEADS * GLA_DV, ATT_HEADS * ATT_DH
    names = ("q_g", "k_g", "v_g", "gk_low", "r_g", "q_a", "k_a", "v_a", "gt_a", "gt_b")
    widths = (qk, qk, gv, GLA_RANK, gv, aw, aw, aw, d, d)
    src = dict(zip(names, np.cumsum((0,) + widths[:-1]).tolist()))
    wid = dict(zip(names, widths))
    order = [nm for nm in names if nm != "gk_low"]
    col, off = {}, 0
    for nm in order:
        col[nm] = off
        off += wid[nm]
    lo, hi = src["gk_low"], src["gk_low"] + GLA_RANK
    assert lo % IN_TN == 0 and (w_in.shape[1] - hi) % IN_TN == 0
    w_low = jnp.pad(w_in[:, lo:hi], ((0, 0), (0, LANES - GLA_RANK))).astype(BF16)

    z, gk = _in_proj(x2, ln_mix.reshape(1, d), w_in[:, :lo].astype(BF16), w_in[:, hi:].astype(BF16), w_low)
    z3 = z.reshape(bsz, seq, -1)
    gk3 = gk.reshape(bsz, seq, LANES)

    wgk_pad = jnp.pad(w_gk, ((0, LANES - GLA_RANK), (0, 0)))
    ya = _gla(z3, gk3, wgk_pad, b_gk.reshape(1, qk), gla_norm.reshape(1, GLA_DV), col)
    yb = _attn(z3, _attn_bias(rel_bias), col)

    wr_pad = jnp.pad(w_router, ((0, 0), (0, LANES - N_EXPERTS)))
    br_pad = jnp.pad(b_router, (0, LANES - N_EXPERTS)).reshape(1, LANES)
    x1, xm, topi, topw, cnt = _outproj(x2, ya.reshape(n, d), yb.reshape(n, d), w_out.astype(BF16),
                                       ln_moe.reshape(1, d), wr_pad, br_pad)

    counts = cnt[0, :N_EXPERTS].astype(I32)
    ntile = (counts + EXP_TM - 1) // EXP_TM
    tile_end = jnp.cumsum(ntile)
    tile_start = tile_end - ntile
    rows_total = n * TOP_K + N_EXPERTS * EXP_TM
    tiles = jnp.arange(rows_total // EXP_TM, dtype=I32)
    onehot = (tiles[:, None] >= tile_start[None, :]) & (tiles[:, None] < tile_end[None, :])
    te = jnp.sum(jnp.where(onehot, jnp.arange(N_EXPERTS, dtype=I32)[None, :], 0), axis=1)
    tv = jnp.sum(jnp.where(onehot, counts[None, :] - (tiles[:, None] - tile_start[None, :]) * EXP_TM, 0),
                 axis=1)
    te = jnp.where(tiles < tile_end[-1], te, N_EXPERTS - 1).astype(I32)
    tv = jnp.clip(tv, 0, EXP_TM).astype(I32)
    start_rows = jnp.pad((tile_start * EXP_TM).astype(F32), (0, LANES - N_EXPERTS)).reshape(1, LANES)
    active = ntile > 0
    eid = jnp.arange(N_EXPERTS, dtype=I32)
    slot_e = (jnp.cumsum(active.astype(I32)) - 1) % 2
    later = (eid[None, :] > eid[:, None]) & active[None, :]
    next_e = jnp.min(jnp.where(later, eid[None, :], N_EXPERTS), axis=1)
    next_e = jnp.where(next_e < N_EXPERTS, next_e, -1)
    tslot = slot_e[te].astype(I32)
    tnext = next_e[te].astype(I32)

    pos_t = _positions(topi, start_rows)[:TOP_K]

    def pos_blocks(tokens):
        return pos_t.reshape(TOP_K, n // tokens, tokens).transpose(1, 0, 2).reshape(-1)

    xs = _dispatch(tv, pos_blocks(DISPATCH_TOKENS), xm, rows_total)

    ys = _experts(te, tv, tslot, tnext, xs, w1,
                  b1[:, 0::2].reshape(N_EXPERTS, 1, -1), b1[:, 1::2].reshape(N_EXPERTS, 1, -1),
                  w2, b2.reshape(N_EXPERTS, 1, d))

    return _ple(pos_blocks(PLE_TB), ys, topw, x1, p2, ln_ple.reshape(1, d), w_ple_gate.astype(BF16),
                w_ple_proj.astype(BF16), ln_out.reshape(1, d))


def kernel(x, p, ln_mix, w_in, w_gk, b_gk, gla_norm, rel_bias, w_out, ln_moe, w_router, b_router,
           w1, b1, w2, b2, ln_ple, w_ple_gate, w_ple_proj, ln_final):
    bsz, seq, d = x.shape
    depth = p.shape[0]
    assert depth == 1, "the final RMSNorm is fused into the last layer's kernel"
    assert d == SUBLANES * LANES, "token-tile layout needs one vreg tile per token"
    assert seq % max(ATT_TQ, GLA_TT) == 0
    assert (bsz * seq) % max(IN_TM, OUT_TM, RANK_TB, DISPATCH_TOKENS, 2 * PLE_TB) == 0
    x2 = x.reshape(bsz * seq, d)
    out = _layer(x2, p[0].reshape(bsz * seq, -1), bsz, seq, ln_mix[0], w_in[0], w_gk[0], b_gk[0],
                 gla_norm[0], rel_bias[0], w_out[0], ln_moe[0], w_router[0], b_router[0],
                 w1[0], b1[0], w2[0], b2[0], ln_ple[0], w_ple_gate[0], w_ple_proj[0], ln_final)
    return out.reshape(bsz, seq, d)
```

```python
import functools

import numpy as np
import jax
import jax.numpy as jnp
from jax import lax
from jax.experimental import pallas as pl
from jax.experimental.pallas import tpu as pltpu

F32 = jnp.float32
BF16 = jnp.bfloat16
I32 = jnp.int32

LANES = 128
SUBLANES = 8
CHUNK = 64
GLA_HEADS = 4
GLA_DK = 128
GLA_DV = 256
GLA_RANK = 16
GLA_TAU = 16.0
ATT_HEADS = 16
ATT_DH = 64
ATT_PAST = 8
REL_CLIP = 256
N_EXPERTS = 32
TOP_K = 4
SWIGLU_ALPHA = 1.702
SWIGLU_LIMIT = 7.0
EPS = 1e-6
LOG2E = 1.4426950408889634

VMEM_LIMIT = 48 * 1024 * 1024

IN_TM, IN_TN = 512, 1024
GLA_TT = 256
ATT_TQ = 256
ATT_STEP_HEADS = 16
OUT_TM = 1024
RANK_TB = 1024
EXP_TM = 512
DISPATCH_TOKENS = 4096
PLE_TB = 256
DMA_UNROLL = 8


def _cparams(sem):
    return pltpu.CompilerParams(dimension_semantics=sem, vmem_limit_bytes=VMEM_LIMIT)


def _split_bf16(a):
    hi = a.astype(BF16)
    lo = (a - hi.astype(F32)).astype(BF16)
    return hi, lo


def _dot_split(a, b):
    a_hi, a_lo = _split_bf16(a)
    b_hi, b_lo = _split_bf16(b)
    d = functools.partial(jnp.dot, preferred_element_type=F32)
    return d(a_hi, b_hi) + (d(a_hi, b_lo) + d(a_lo, b_hi))


def _rms(x, gain):
    ms = jnp.mean(x * x, axis=-1, keepdims=True)
    return x * lax.rsqrt(ms + EPS) * gain


def _store_token_tiles(ref, val):
    groups = val.shape[1] // LANES
    for c in range(groups):
        ref[pl.ds(c, val.shape[0], stride=groups), :] = val[:, c * LANES:(c + 1) * LANES]


def _load_token_tiles(ref, rows, groups):
    return jnp.concatenate([ref[pl.ds(c, rows, stride=groups), :] for c in range(groups)], axis=1)


_NT = (((1,), (1,)), ((), ()))
_TN = (((0,), (0,)), ((), ()))


def _in_proj_kernel(x_ref, g_ref, wa_ref, wb_ref, wlow_ref, z_ref, gk_ref):
    xn = _rms(x_ref[...], g_ref[...]).astype(BF16)
    gk_ref[...] = jnp.dot(xn, wlow_ref[...], preferred_element_type=F32)
    off = 0
    for w_ref in (wa_ref, wb_ref):
        for j in range(w_ref.shape[1] // IN_TN):
            z_ref[:, off:off + IN_TN] = jnp.dot(xn, w_ref[:, j * IN_TN:(j + 1) * IN_TN],
                                                preferred_element_type=F32).astype(BF16)
            off += IN_TN


def _in_proj(x2, ln, w_a, w_b, w_low):
    n, d = x2.shape
    ncol = w_a.shape[1] + w_b.shape[1]
    once = pl.Buffered(1)
    return pl.pallas_call(
        _in_proj_kernel,
        grid=(n // IN_TM,),
        in_specs=[
            pl.BlockSpec((IN_TM, d), lambda i: (i, 0)),
            pl.BlockSpec((1, d), lambda i: (0, 0)),
            pl.BlockSpec(w_a.shape, lambda i: (0, 0), pipeline_mode=once),
            pl.BlockSpec(w_b.shape, lambda i: (0, 0), pipeline_mode=once),
            pl.BlockSpec((d, LANES), lambda i: (0, 0), pipeline_mode=once),
        ],
        out_specs=[
            pl.BlockSpec((IN_TM, ncol), lambda i: (i, 0)),
            pl.BlockSpec((IN_TM, LANES), lambda i: (i, 0)),
        ],
        out_shape=[
            jax.ShapeDtypeStruct((n, ncol), BF16),
            jax.ShapeDtypeStruct((n, LANES), F32),
        ],
        compiler_params=_cparams(("parallel",)),
        name="in_proj",
    )(x2, ln, w_a, w_b, w_low)


_GLA_LEVELS = (8, 16, 32)
_GLA_BOT = 8


def _gla_consts(tt):
    t = np.arange(tt)
    same_chunk = (t[:, None] // CHUNK) == (t[None, :] // CHUNK)
    tri = (same_chunk & (t[None, :] <= t[:, None])).astype(np.float32)
    mlev = []
    for h in _GLA_LEVELS:
        blk = (t[:, None] // (2 * h)) == (t[None, :] // (2 * h))
        m = blk & ((t[:, None] % (2 * h)) >= h) & ((t[None, :] % (2 * h)) < h)
        mlev.append(m.astype(np.float32))
    mbot = (((t[None, :] // _GLA_BOT) == (t[:, None] // _GLA_BOT))
            & ((t[None, :] % _GLA_BOT) <= (t[:, None] % _GLA_BOT)))
    spread = (np.arange(_GLA_BOT * GLA_DK)[:, None] // GLA_DK) == (t[None, :] % _GLA_BOT)
    return (jnp.asarray(tri, BF16), jnp.asarray(np.stack(mlev), F32),
            jnp.asarray(mbot.astype(np.float32), F32), jnp.asarray(spread.astype(np.float32), BF16))


def _gla_kernel(q_ref, k_ref, v_ref, r_ref, ga_ref, gk_ref, wgk_ref, bgk_ref, gn_ref,
                tri_ref, mlev_ref, mbot_ref, ones_ref, o_ref, st_ref):
    @pl.when(pl.program_id(1) == 0)
    def _():
        st_ref[...] = jnp.zeros_like(st_ref)

    xg = _dot_split(gk_ref[...], wgk_ref[...]) + bgk_ref[...]
    g = -(jnp.maximum(-xg, 0.0) + jnp.log(1.0 + jnp.exp(-jnp.abs(xg)))) * (1.0 / GLA_TAU)
    g_hi, g_lo = _split_bf16(g)
    tri = tri_ref[...]
    b_all = (jnp.dot(tri, g_hi, preferred_element_type=F32)
             + jnp.dot(tri, g_lo, preferred_element_type=F32))

    for hh in range(GLA_HEADS):
        kc = slice(hh * GLA_DK, (hh + 1) * GLA_DK)
        vc = slice(hh * GLA_DV, (hh + 1) * GLA_DV)
        y, st = _gla_head(q_ref[:, kc], k_ref[:, kc], v_ref[:, vc], b_all[:, kc], st_ref[hh],
                          mlev_ref, mbot_ref, ones_ref[...])
        st_ref[hh] = st
        r = r_ref[:, vc].astype(F32)
        ya = _rms(y, gn_ref[...]) * (r * jax.nn.sigmoid(r))
        o_ref[:, vc] = (jax.nn.sigmoid(ga_ref[:, vc].astype(F32)) * ya).astype(BF16)


def _gla_head(q, k, v, b, st, mlev_ref, mbot_ref, ones):
    tt = q.shape[0]
    q = q.astype(F32) * (GLA_DK ** -0.5)
    k = k.astype(F32)
    b = b * LOG2E

    s_intra = jnp.zeros((tt, tt), F32)
    for li, h in enumerate(_GLA_LEVELS):
        b3 = b.reshape(tt // (2 * h), 2 * h, GLA_DK)
        e = jnp.exp2(-jnp.abs(b3 - b3[:, h - 1:h, :])).reshape(tt, GLA_DK)
        sc = lax.dot_general((q * e).astype(BF16), (k * e).astype(BF16), _NT,
                             preferred_element_type=F32)
        s_intra = s_intra + sc * mlev_ref[li]
    nb = tt // _GLA_BOT
    b3 = b.reshape(nb, _GLA_BOT, GLA_DK)
    q3 = q.reshape(nb, _GLA_BOT, GLA_DK)
    k3 = k.reshape(nb, _GLA_BOT, GLA_DK)
    prods = []
    for s in range(_GLA_BOT):
        e = jnp.exp2(jnp.minimum(b3 - b3[:, s:s + 1, :], 0.0))
        prods.append((q3 * k3[:, s:s + 1, :] * e).reshape(tt, GLA_DK).astype(BF16))
    s_intra = s_intra + jnp.dot(jnp.concatenate(prods, axis=1), ones,
                                preferred_element_type=F32) * mbot_ref[...]
    o_intra = jnp.dot(s_intra.astype(BF16), v, preferred_element_type=F32)

    outs = []
    for c in range(tt // CHUNK):
        lo = c * CHUNK
        bc = b[lo:lo + CHUNK]
        bl = b[lo + CHUNK - 1:lo + CHUNK]
        qe = (q[lo:lo + CHUNK] * jnp.exp2(bc)).astype(BF16)
        outs.append(lax.dot_general(qe, st.astype(BF16), _NT, preferred_element_type=F32))
        kd = (k[lo:lo + CHUNK] * jnp.exp2(bl - bc)).astype(BF16)
        upd = lax.dot_general(v[lo:lo + CHUNK], kd, _TN, preferred_element_type=F32)
        st = st * jnp.exp2(bl) + upd
    return o_intra + jnp.concatenate(outs, axis=0), st


def _gla(z3, gk3, wgk_pad, bgk, gnorm, col):
    b, t, _ = z3.shape
    tt = GLA_TT
    tri, mlev, mbot, ones = _gla_consts(tt)
    qk, gv = GLA_HEADS * GLA_DK, GLA_HEADS * GLA_DV

    def zspec(width, off):
        return pl.BlockSpec((None, tt, width), lambda bi, ti, o=off // width: (bi, ti, o))

    const2 = lambda bi, ti: (0, 0)
    const3 = lambda bi, ti: (0, 0, 0)
    return pl.pallas_call(
        _gla_kernel,
        grid=(b, t // tt),
        in_specs=[
            zspec(qk, col["q_g"]), zspec(qk, col["k_g"]), zspec(gv, col["v_g"]),
            zspec(gv, col["r_g"]), zspec(gv, col["gt_a"]),
            pl.BlockSpec((None, tt, LANES), lambda bi, ti: (bi, ti, 0)),
            pl.BlockSpec((LANES, qk), const2),
            pl.BlockSpec((1, qk), const2),
            pl.BlockSpec((1, GLA_DV), const2),
            pl.BlockSpec((tt, tt), const2),
            pl.BlockSpec((len(_GLA_LEVELS), tt, tt), const3),
            pl.BlockSpec((tt, tt), const2),
            pl.BlockSpec((_GLA_BOT * GLA_DK, tt), const2),
        ],
        out_specs=pl.BlockSpec((None, tt, gv), lambda bi, ti: (bi, ti, 0)),
        out_shape=jax.ShapeDtypeStruct((b, t, gv), BF16),
        scratch_shapes=[pltpu.VMEM((GLA_HEADS, GLA_DV, GLA_DK), F32)],
        compiler_params=_cparams(("parallel", "arbitrary")),
        name="gla",
    )(z3, z3, z3, z3, z3, gk3, wgk_pad, bgk, gnorm, tri, mlev, mbot, ones)


_ATT_NKB = 3


def _attn_bias(rel_bias):
    tq = ATT_TQ
    nk = _ATT_NKB * tq
    nheads = rel_bias.shape[0]
    band = (ATT_PAST + 1) * CHUNK
    assert nk - tq == ATT_PAST * CHUNK and tq % CHUNK == 0
    span = band + CHUNK - 1
    dist = np.clip(np.arange(span) - (CHUNK - 1), -REL_CLIP, REL_CLIP) + REL_CLIP
    g = rel_bias.astype(F32)[:, dist]
    x = jnp.pad(g[:, ::-1], ((0, 0), (0, 1)))
    base = jnp.tile(x, (1, CHUNK))[:, :CHUNK * span].reshape(nheads, CHUNK, span)[:, :, CHUNK - 1:CHUNK - 1 + band]
    tab = jnp.concatenate(
        [jnp.pad(base, ((0, 0), (0, 0), (ci * CHUNK, nk - band - ci * CHUNK)), constant_values=-jnp.inf)
         for ci in range(tq // CHUNK)], axis=1)
    present = np.stack([np.arange(nk) // tq >= _ATT_NKB - 1 - e for e in range(_ATT_NKB)])
    return jnp.where(jnp.asarray(present)[:, None, None, :], tab[None], -jnp.inf)


def _attn_kernel(q_ref, k0_ref, k1_ref, k2_ref, v0_ref, v1_ref, v2_ref, gb_ref, bias_ref, o_ref):
    tq = q_ref.shape[0]
    first = lax.broadcasted_iota(I32, (1, LANES), 1) < ATT_DH
    scale = jnp.asarray(ATT_DH ** -0.5, BF16)
    krefs = (k0_ref, k1_ref, k2_ref)
    vrefs = (v0_ref, v1_ref, v2_ref)
    for lb in range(q_ref.shape[1] // LANES):
        cols = slice(lb * LANES, (lb + 1) * LANES)
        q = q_ref[:, cols] * scale
        zero = jnp.zeros_like(q)
        qs = jnp.concatenate([jnp.where(first, q, zero), jnp.where(first, zero, q)], axis=0)
        s = jnp.concatenate(
            [lax.dot_general(qs, r[:, cols], _NT, preferred_element_type=F32) for r in krefs], axis=1)
        s = s + jnp.concatenate([bias_ref[2 * lb], bias_ref[2 * lb + 1]], axis=0)
        p = jnp.exp(s - jnp.max(s, axis=-1, keepdims=True))
        l = jnp.sum(p, axis=-1, keepdims=True)
        pb = p.astype(BF16)
        pv = functools.reduce(jnp.add, [
            jnp.dot(pb[:, i * tq:(i + 1) * tq], vrefs[i][:, cols], preferred_element_type=F32)
            for i in range(_ATT_NKB)]) / l
        o = jnp.where(first, pv[:tq], pv[tq:])
        o_ref[:, cols] = (jax.nn.sigmoid(gb_ref[:, cols].astype(F32)) * o).astype(BF16)


def _attn(z3, bias, col):
    b, t, _ = z3.shape
    tq = ATT_TQ
    width = ATT_STEP_HEADS * ATT_DH
    steps = ATT_HEADS // ATT_STEP_HEADS

    def cur(off):
        return pl.BlockSpec((None, tq, width), lambda h, j, bi, o=off // width: (bi, j, o + h))

    def past(off, back):
        return pl.BlockSpec((None, tq, width),
                            lambda h, j, bi, o=off // width: (bi, jnp.maximum(j - back, 0), o + h))

    return pl.pallas_call(
        _attn_kernel,
        grid=(steps, t // tq, b),
        in_specs=[
            cur(col["q_a"]),
            past(col["k_a"], 2), past(col["k_a"], 1), cur(col["k_a"]),
            past(col["v_a"], 2), past(col["v_a"], 1), cur(col["v_a"]),
            cur(col["gt_b"]),
            pl.BlockSpec((None, ATT_STEP_HEADS, tq, _ATT_NKB * tq),
                         lambda h, j, bi: (jnp.minimum(j, _ATT_NKB - 1), h, 0, 0)),
        ],
        out_specs=pl.BlockSpec((None, tq, width), lambda h, j, bi: (bi, j, h)),
        out_shape=jax.ShapeDtypeStruct((b, t, ATT_HEADS * ATT_DH), BF16),
        compiler_params=_cparams(("parallel", "parallel", "parallel")),
        name="band_attn",
    )(z3, z3, z3, z3, z3, z3, z3, z3, bias)


def _outproj_kernel(x_ref, ya_ref, yb_ref, wo_ref, lnm_ref, wr_ref, br_ref,
                    x1_ref, xm_ref, topi_ref, topw_ref, cnt_ref):
    h = (ya_ref[...].astype(F32) + yb_ref[...].astype(F32)).astype(BF16)
    x1 = x_ref[...] + jnp.dot(h, wo_ref[...], preferred_element_type=F32)
    x1_ref[...] = x1
    xm = _rms(x1, lnm_ref[...])
    _store_token_tiles(xm_ref, xm)

    tm = xm.shape[0]
    lane = lax.broadcasted_iota(I32, (tm, LANES), 1)
    lanef = lane.astype(F32)
    x_hi, x_lo = _split_bf16(xm)
    w_both = jnp.concatenate(_split_bf16(wr_ref[...]), axis=1)
    part = (jnp.dot(x_hi, w_both, preferred_element_type=F32)
            + jnp.dot(x_lo, w_both, preferred_element_type=F32))
    logits = part[:, :LANES] + part[:, LANES:] + br_ref[...]
    l = jnp.where(lane < N_EXPERTS, logits, -jnp.inf)
    vals, idxs = [], []
    picked = jnp.zeros((tm, LANES), F32)
    for _ in range(TOP_K):
        m = jnp.max(l, axis=-1, keepdims=True)
        idx = jnp.min(jnp.where(l == m, lanef, float(LANES)), axis=-1, keepdims=True)
        vals.append(m)
        idxs.append(idx)
        hit = lanef == idx
        picked = jnp.where(hit, 1.0, picked)
        l = jnp.where(hit, -jnp.inf, l)
    es = [jnp.exp(vv - vals[0]) for vv in vals]
    tot = functools.reduce(jnp.add, es)
    topw = jnp.zeros((tm, LANES), F32)
    topi = jnp.zeros((tm, LANES), F32)
    for kk in range(TOP_K):
        topw = jnp.where(lane == kk, es[kk] / tot, topw)
        topi = jnp.where(lane == kk, idxs[kk], topi)
    topw_ref[...] = topw
    topi_ref[...] = topi.astype(I32)

    @pl.when(pl.program_id(0) == 0)
    def _():
        cnt_ref[...] = jnp.zeros_like(cnt_ref)

    cnt_ref[...] = cnt_ref[...] + jnp.sum(picked, axis=0, keepdims=True)


def _outproj(x2, ya, yb, w_out, ln_moe, wr_pad, br_pad):
    n, d = x2.shape
    tm = OUT_TM
    row = lambda i: (i, 0)
    const = lambda i: (0, 0)
    return pl.pallas_call(
        _outproj_kernel,
        grid=(n // tm,),
        in_specs=[
            pl.BlockSpec((tm, d), row), pl.BlockSpec((tm, d), row), pl.BlockSpec((tm, d), row),
            pl.BlockSpec((d, d), const), pl.BlockSpec((1, d), const),
            pl.BlockSpec((d, LANES), const), pl.BlockSpec((1, LANES), const),
        ],
        out_specs=[
            pl.BlockSpec((tm, d), row), pl.BlockSpec((tm * d // LANES, LANES), row),
            pl.BlockSpec((tm, LANES), row), pl.BlockSpec((tm, LANES), row),
            pl.BlockSpec((1, LANES), const),
        ],
        out_shape=[
            jax.ShapeDtypeStruct((n, d), F32), jax.ShapeDtypeStruct((n * d // LANES, LANES), F32),
            jax.ShapeDtypeStruct((n, LANES), I32), jax.ShapeDtypeStruct((n, LANES), F32),
            jax.ShapeDtypeStruct((1, LANES), F32),
        ],
        compiler_params=_cparams(("arbitrary",)),
        name="out_proj_router",
    )(x2, ya, yb, w_out, ln_moe, wr_pad, br_pad)


def _pos_kernel(topi_ref, start_ref, tri_ref, pos_ref, carry_ref):
    @pl.when(pl.program_id(0) == 0)
    def _():
        carry_ref[...] = start_ref[...]

    ti = topi_ref[...]
    tb = ti.shape[0]
    lane = lax.broadcasted_iota(I32, (tb, LANES), 1)
    sel = [lane == ti[:, kk:kk + 1] for kk in range(TOP_K)]
    oh = functools.reduce(jnp.add, [jnp.where(s, 1.0, 0.0) for s in sel])
    row = carry_ref[...] + jnp.dot(tri_ref[...], oh.astype(BF16), preferred_element_type=F32)
    out = jnp.zeros((tb, LANES), F32)
    for kk in range(TOP_K):
        rk = jnp.sum(jnp.where(sel[kk], row, 0.0), axis=-1, keepdims=True)
        out = jnp.where(lane == kk, rk, out)
    pos_ref[...] = jnp.transpose(out)[:pos_ref.shape[0]].astype(I32)
    carry_ref[...] = carry_ref[...] + jnp.sum(oh, axis=0, keepdims=True)


def _positions(topi, start_rows):
    n = topi.shape[0]
    tb = RANK_TB
    t = np.arange(tb)
    tri = jnp.asarray((t[None, :] < t[:, None]).astype(np.float32), BF16)
    return pl.pallas_call(
        _pos_kernel,
        grid=(n // tb,),
        in_specs=[pl.BlockSpec((tb, LANES), lambda i: (i, 0)),
                  pl.BlockSpec((1, LANES), lambda i: (0, 0)),
                  pl.BlockSpec((tb, tb), lambda i: (0, 0))],
        out_specs=pl.BlockSpec((SUBLANES, tb), lambda i: (0, i)),
        out_shape=jax.ShapeDtypeStruct((SUBLANES, n), I32),
        scratch_shapes=[pltpu.VMEM((1, LANES), F32)],
        compiler_params=_cparams(("arbitrary",)),
        name="route_positions",
    )(topi, start_rows, tri)


def _token_copy(src, src_tok, dst, dst_tok, sem):
    s0 = pl.multiple_of(src_tok * SUBLANES, SUBLANES)
    d0 = pl.multiple_of(dst_tok * SUBLANES, SUBLANES)
    return pltpu.make_async_copy(src.at[pl.ds(s0, SUBLANES)], dst.at[pl.ds(d0, SUBLANES)], sem)


def _tokens_wait(ref, ntok, sem):
    pltpu.make_async_copy(ref.at[pl.ds(0, ntok * SUBLANES)], ref.at[pl.ds(0, ntok * SUBLANES)], sem).wait()


def _dispatch_kernel(tv_ref, pos_hbm, xm_ref, xs_hbm, pos_smem, zeros_ref, sem_pos, sem_rows, sem_fill):
    i = pl.program_id(0)
    nsteps = pl.num_programs(0)
    tokens = xm_ref.shape[0] // SUBLANES
    ch = tokens * TOP_K
    slot = i % 2

    def pos_copy(step, sl):
        return pltpu.make_async_copy(pos_hbm.at[pl.ds(step * ch, ch)], pos_smem.at[pl.ds(sl * ch, ch)],
                                     sem_pos.at[sl])

    @pl.when(i == 0)
    def _():
        pos_copy(0, 0).start()
        tm = zeros_ref.shape[0]
        zeros_ref[...] = jnp.zeros_like(zeros_ref)

        def fill(t):
            return pltpu.make_async_copy(zeros_ref, xs_hbm.at[pl.ds(t * tm, tm)], sem_fill)

        def start(t, carry):
            @pl.when(tv_ref[t] < EXP_TM)
            def _():
                fill(t).start()
            return carry

        def wait(t, carry):
            @pl.when(tv_ref[t] < EXP_TM)
            def _():
                fill(t).wait()
            return carry

        ntiles = xs_hbm.shape[0] // tm
        lax.fori_loop(0, ntiles, start, 0)
        lax.fori_loop(0, ntiles, wait, 0)

    @pl.when(i + 1 < nsteps)
    def _():
        pos_copy(i + 1, 1 - slot).start()

    pos_copy(i, slot).wait()

    def body(t, carry):
        for kk in range(TOP_K):
            _token_copy(xm_ref, t, xs_hbm, pos_smem[slot * ch + kk * tokens + t],
                        sem_rows).start(priority=kk % 2)
        return carry

    lax.fori_loop(0, tokens, body, 0, unroll=DMA_UNROLL)
    _tokens_wait(xs_hbm, ch, sem_rows)


def _dispatch(tile_valid, pos_flat, xm, rows_total):
    tokens = DISPATCH_TOKENS
    n = xm.shape[0] // SUBLANES
    return pl.pallas_call(
        _dispatch_kernel,
        grid_spec=pltpu.PrefetchScalarGridSpec(
            num_scalar_prefetch=1,
            grid=(n // tokens,),
            in_specs=[pl.BlockSpec(memory_space=pl.ANY),
                      pl.BlockSpec((tokens * SUBLANES, LANES), lambda i, tv: (i, 0))],
            out_specs=pl.BlockSpec(memory_space=pl.ANY),
            scratch_shapes=[pltpu.SMEM((2 * tokens * TOP_K,), I32),
                            pltpu.VMEM((EXP_TM * SUBLANES, LANES), F32),
                            pltpu.SemaphoreType.DMA((2,)), pltpu.SemaphoreType.DMA,
                            pltpu.SemaphoreType.DMA],
        ),
        out_shape=jax.ShapeDtypeStruct((rows_total * SUBLANES, LANES), F32),
        compiler_params=_cparams(("arbitrary",)),
        name="dispatch",
    )(tile_valid, pos_flat, xm)


def _deinterleave_matrix():
    j = np.arange(2 * LANES)[:, None]
    c = np.arange(2 * LANES)[None, :]
    sel = np.where(c < LANES, j == 2 * c, j == 2 * (c - LANES) + 1)
    return jnp.asarray(sel.astype(np.float32), BF16)


def _expert_kernel(te_ref, tv_ref, slot_ref, next_ref, xs_ref, w1_hbm, b1g_ref, b1l_ref, w2_hbm, b2_ref,
                   sel_ref, ys_ref, w1g_ref, w1l_ref, w2p_ref, w1buf, w2buf, sem_w1, sem_w2):
    t = pl.program_id(0)
    nvalid = tv_ref[t]
    new_expert = (t == 0) | (te_ref[t] != te_ref[jnp.maximum(t - 1, 0)])

    def weight_copies(expert, sl):
        return (pltpu.make_async_copy(w1_hbm.at[expert], w1buf.at[sl], sem_w1.at[sl]),
                pltpu.make_async_copy(w2_hbm.at[expert], w2buf.at[sl], sem_w2.at[sl]))

    @pl.when((nvalid > 0) & new_expert)
    def _():
        sl = slot_ref[t]

        @pl.when(t == 0)
        def _():
            for cp in weight_copies(te_ref[t], sl):
                cp.start()

        for cp in weight_copies(te_ref[t], sl):
            cp.wait()

        @pl.when(next_ref[t] >= 0)
        def _():
            for cp in weight_copies(next_ref[t], 1 - sl):
                cp.start()

        sel = sel_ref[...]
        for m in range(w1g_ref.shape[1] // LANES):
            pair = w1buf[sl, :, 2 * m * LANES:(2 * m + 2) * LANES].astype(BF16)
            split = jnp.dot(pair, sel, preferred_element_type=F32).astype(BF16)
            w1g_ref[:, m * LANES:(m + 1) * LANES] = split[:, :LANES]
            w1l_ref[:, m * LANES:(m + 1) * LANES] = split[:, LANES:]
        w2p_ref[...] = w2buf[sl].astype(BF16)

    @pl.when(nvalid > 0)
    def _():
        tm = xs_ref.shape[0] // SUBLANES
        x = _load_token_tiles(xs_ref, tm, SUBLANES).astype(BF16)
        hg = jnp.dot(x, w1g_ref[...], preferred_element_type=F32) + b1g_ref[...]
        hl = jnp.dot(x, w1l_ref[...], preferred_element_type=F32) + b1l_ref[...]
        glu = jnp.minimum(hg, SWIGLU_LIMIT)
        lin = jnp.clip(hl, -SWIGLU_LIMIT, SWIGLU_LIMIT)
        act = glu * jax.nn.sigmoid(SWIGLU_ALPHA * glu) * (lin + 1.0)
        y = jnp.dot(act.astype(BF16), w2p_ref[...], preferred_element_type=F32) + b2_ref[...]
        _store_token_tiles(ys_ref, y)

    @pl.when(nvalid <= 0)
    def _():
        ys_ref[...] = jnp.zeros_like(ys_ref)


def _experts(tile_expert, tile_valid, tile_slot, tile_next, xs, w1, b1g, b1l, w2, b2):
    f, d = w2.shape[1:]
    p = xs.shape[0] // SUBLANES
    tm = EXP_TM
    wmap = lambda t, te, tv, ts, tn: (te[t], 0, 0)
    tile = lambda t, te, tv, ts, tn: (t, 0)
    return pl.pallas_call(
        _expert_kernel,
        grid_spec=pltpu.PrefetchScalarGridSpec(
            num_scalar_prefetch=4,
            grid=(p // tm,),
            in_specs=[
                pl.BlockSpec((tm * SUBLANES, LANES), tile),
                pl.BlockSpec(memory_space=pl.ANY),
                pl.BlockSpec((None, 1, f), wmap), pl.BlockSpec((None, 1, f), wmap),
                pl.BlockSpec(memory_space=pl.ANY), pl.BlockSpec((None, 1, d), wmap),
                pl.BlockSpec((2 * LANES, 2 * LANES), lambda t, te, tv, ts, tn: (0, 0)),
            ],
            out_specs=pl.BlockSpec((tm * SUBLANES, LANES), tile),
            scratch_shapes=[pltpu.VMEM((d, f), BF16), pltpu.VMEM((d, f), BF16),
                            pltpu.VMEM((f, d), BF16),
                            pltpu.VMEM((2, d, 2 * f), F32), pltpu.VMEM((2, f, d), F32),
                            pltpu.SemaphoreType.DMA((2,)), pltpu.SemaphoreType.DMA((2,))],
        ),
        out_shape=jax.ShapeDtypeStruct((p * SUBLANES, LANES), F32),
        compiler_params=_cparams(("arbitrary",)),
        name="experts",
    )(tile_expert, tile_valid, tile_slot, tile_next, xs, w1, b1g, b1l, w2, b2, _deinterleave_matrix())


def _ple_kernel(pos_hbm, ys_hbm, topw_ref, x1_ref, p_ref, lnp_ref, wpg_ref, wpp_ref, lnf_ref, o_ref,
                pos0, pos1, ybuf0, ybuf1, sem_pos, sem_rows):
    i = pl.program_id(0)
    nsteps = pl.num_programs(0)
    tb = x1_ref.shape[0] // 2
    ch = tb * TOP_K
    nblocks = 2 * nsteps
    pos_bufs = (pos0, pos1)
    ybufs = (ybuf0, ybuf1)

    def pos_copy(block, par):
        blk = jnp.minimum(block, nblocks - 1)
        return pltpu.make_async_copy(pos_hbm.at[pl.ds(blk * ch, ch)], pos_bufs[par], sem_pos.at[par])

    def issue_gathers(par):
        def body(t, carry):
            for kk in range(TOP_K):
                _token_copy(ys_hbm, pos_bufs[par][kk * tb + t], ybufs[par].at[kk], t,
                            sem_rows.at[par]).start(priority=kk % 2)
            return carry
        lax.fori_loop(0, tb, body, 0, unroll=True)

    def wait_gathers(par):
        for kk in range(TOP_K):
            _tokens_wait(ybufs[par].at[kk], tb, sem_rows.at[par])

    def compute(par):
        rows = slice(par * tb, (par + 1) * tb)
        topw = topw_ref[rows, :]
        groups = x1_ref.shape[1] // LANES
        moe = functools.reduce(jnp.add, [
            topw[:, kk:kk + 1] * _load_token_tiles(ybufs[par].at[kk], tb, groups)
            for kk in range(TOP_K)])
        x2 = x1_ref[rows, :] + moe
        gate = jax.nn.sigmoid(jnp.dot(_rms(x2, lnp_ref[...]).astype(BF16), wpg_ref[...],
                                      preferred_element_type=F32))
        proj = jnp.dot(p_ref[rows, :].astype(BF16), wpp_ref[...], preferred_element_type=F32)
        o_ref[rows, :] = _rms(x2 + gate * proj, lnf_ref[...])

    @pl.when(i == 0)
    def _():
        pos_copy(0, 0).start()
        pos_copy(0, 0).wait()
        issue_gathers(0)
        pos_copy(1, 1).start()

    pos_copy(2 * i + 1, 1).wait()
    wait_gathers(0)
    pos_copy(2 * i + 2, 0).start()
    issue_gathers(1)
    compute(0)
    pos_copy(2 * i + 2, 0).wait()
    wait_gathers(1)
    pos_copy(2 * i + 3, 1).start()
    issue_gathers(0)
    compute(1)

    @pl.when(i == nsteps - 1)
    def _():
        wait_gathers(0)
        pos_copy(0, 1).wait()


def _ple(pos_flat, ys, topw, x1, p2, ln_ple, wpg, wpp, ln_final):
    n, d = x1.shape
    pd = p2.shape[1]
    tb = PLE_TB
    row = lambda i: (i, 0)
    const = lambda i: (0, 0)
    return pl.pallas_call(
        _ple_kernel,
        grid=(n // (2 * tb),),
        in_specs=[
            pl.BlockSpec(memory_space=pl.ANY), pl.BlockSpec(memory_space=pl.ANY),
            pl.BlockSpec((2 * tb, LANES), row), pl.BlockSpec((2 * tb, d), row),
            pl.BlockSpec((2 * tb, pd), row),
            pl.BlockSpec((1, d), const), pl.BlockSpec((d, d), const),
            pl.BlockSpec((pd, d), const), pl.BlockSpec((1, d), const),
        ],
        out_specs=pl.BlockSpec((2 * tb, d), row),
        out_shape=jax.ShapeDtypeStruct((n, d), F32),
        scratch_shapes=[
            pltpu.SMEM((tb * TOP_K,), I32), pltpu.SMEM((tb * TOP_K,), I32),
            pltpu.VMEM((TOP_K, tb * d // LANES, LANES), F32),
            pltpu.VMEM((TOP_K, tb * d // LANES, LANES), F32),
            pltpu.SemaphoreType.DMA((2,)),
            pltpu.SemaphoreType.DMA((2,)),
        ],
        compiler_params=_cparams(("arbitrary",)),
        name="combine_ple_final",
    )(pos_flat, ys, topw, x1, p2, ln_ple, wpg, wpp, ln_final)


def _layer(x2, p2, bsz, seq, ln_mix, w_in, w_gk, b_gk, gla_norm, rel_bias, w_out, ln_moe,
           w_router, b_router, w1, b1, w2, b2, ln_ple, w_ple_gate, w_ple_proj, ln_out):
    n, d = x2.shape
    qk, gv, aw = GLA_HEADS * GLA_DK, GLA_HEADS * GLA_DV, ATT_HEADS * ATT_DH
    names = ("q_g", "k_g", "v_g", "gk_low", "r_g", "q_a", "k_a", "v_a", "gt_a", "gt_b")
    widths = (qk, qk, gv, GLA_RANK, gv, aw, aw, aw, d, d)
    src = dict(zip(names, np.cumsum((0,) + widths[:-1]).tolist()))
    wid = dict(zip(names, widths))
    order = [nm for nm in names if nm != "gk_low"]
    col, off = {}, 0
    for nm in order:
        col[nm] = off
        off += wid[nm]
    lo, hi = src["gk_low"], src["gk_low"] + GLA_RANK
    assert lo % IN_TN == 0 and (w_in.shape[1] - hi) % IN_TN == 0
    w_low = jnp.pad(w_in[:, lo:hi], ((0, 0), (0, LANES - GLA_RANK))).astype(BF16)

    z, gk = _in_proj(x2, ln_mix.reshape(1, d), w_in[:, :lo].astype(BF16), w_in[:, hi:].astype(BF16), w_low)
    z3 = z.reshape(bsz, seq, -1)
    gk3 = gk.reshape(bsz, seq, LANES)

    wgk_pad = jnp.pad(w_gk, ((0, LANES - GLA_RANK), (0, 0)))
    ya = _gla(z3, gk3, wgk_pad, b_gk.reshape(1, qk), gla_norm.reshape(1, GLA_DV), col)
    yb = _attn(z3, _attn_bias(rel_bias), col)

    wr_pad = jnp.pad(w_router, ((0, 0), (0, LANES - N_EXPERTS)))
    br_pad = jnp.pad(b_router, (0, LANES - N_EXPERTS)).reshape(1, LANES)
    x1, xm, topi, topw, cnt = _outproj(x2, ya.reshape(n, d), yb.reshape(n, d), w_out.astype(BF16),
                                       ln_moe.reshape(1, d), wr_pad, br_pad)

    counts = cnt[0, :N_EXPERTS].astype(I32)
    ntile = (counts + EXP_TM - 1) // EXP_TM
    tile_end = jnp.cumsum(ntile)
    tile_start = tile_end - ntile
    rows_total = n * TOP_K + N_EXPERTS * EXP_TM
    tiles = jnp.arange(rows_total // EXP_TM, dtype=I32)
    onehot = (tiles[:, None] >= tile_start[None, :]) & (tiles[:, None] < tile_end[None, :])
    te = jnp.sum(jnp.where(onehot, jnp.arange(N_EXPERTS, dtype=I32)[None, :], 0), axis=1)
    tv = jnp.sum(jnp.where(onehot, counts[None, :] - (tiles[:, None] - tile_start[None, :]) * EXP_TM, 0),
                 axis=1)
    te = jnp.where(tiles < tile_end[-1], te, N_EXPERTS - 1).astype(I32)
    tv = jnp.clip(tv, 0, EXP_TM).astype(I32)
    start_rows = jnp.pad((tile_start * EXP_TM).astype(F32), (0, LANES - N_EXPERTS)).reshape(1, LANES)
    active = ntile > 0
    eid = jnp.arange(N_EXPERTS, dtype=I32)
    slot_e = (jnp.cumsum(active.astype(I32)) - 1) % 2
    later = (eid[None, :] > eid[:, None]) & active[None, :]
    next_e = jnp.min(jnp.where(later, eid[None, :], N_EXPERTS), axis=1)
    next_e = jnp.where(next_e < N_EXPERTS, next_e, -1)
    tslot = slot_e[te].astype(I32)
    tnext = next_e[te].astype(I32)

    pos_t = _positions(topi, start_rows)[:TOP_K]

    def pos_blocks(tokens):
        return pos_t.reshape(TOP_K, n // tokens, tokens).transpose(1, 0, 2).reshape(-1)

    xs = _dispatch(tv, pos_blocks(DISPATCH_TOKENS), xm, rows_total)

    ys = _experts(te, tv, tslot, tnext, xs, w1,
                  b1[:, 0::2].reshape(N_EXPERTS, 1, -1), b1[:, 1::2].reshape(N_EXPERTS, 1, -1),
                  w2, b2.reshape(N_EXPERTS, 1, d))

    return _ple(pos_blocks(PLE_TB), ys, topw, x1, p2, ln_ple.reshape(1, d), w_ple_gate.astype(BF16),
                w_ple_proj.astype(BF16), ln_out.reshape(1, d))


def kernel(x, p, ln_mix, w_in, w_gk, b_gk, gla_norm, rel_bias, w_out, ln_moe, w_router, b_router,
           w1, b1, w2, b2, ln_ple, w_ple_gate, w_ple_proj, ln_final):
    bsz, seq, d = x.shape
    depth = p.shape[0]
    assert depth == 1, "the final RMSNorm is fused into the last layer's kernel"
    assert d == SUBLANES * LANES, "token-tile layout needs one vreg tile per token"
    assert seq % max(ATT_TQ, GLA_TT) == 0
    assert (bsz * seq) % max(IN_TM, OUT_TM, RANK_TB, DISPATCH_TOKENS, 2 * PLE_TB) == 0
    x2 = x.reshape(bsz * seq, d)
    out = _layer(x2, p[0].reshape(bsz * seq, -1), bsz, seq, ln_mix[0], w_in[0], w_gk[0], b_gk[0],
                 gla_norm[0], rel_bias[0], w_out[0], ln_moe[0], w_router[0], b_router[0],
                 w1[0], b1[0], w2[0], b2[0], ln_ple[0], w_ple_gate[0], w_ple_proj[0], ln_final)
    return out.reshape(bsz, seq, d)
```

```python
import functools

import numpy as np
import jax
import jax.numpy as jnp
from jax import lax
from jax.experimental import pallas as pl
from jax.experimental.pallas import tpu as pltpu

F32 = jnp.float32
BF16 = jnp.bfloat16
I32 = jnp.int32

LANES = 128
SUBLANES = 8
CHUNK = 64
GLA_HEADS = 4
GLA_DK = 128
GLA_DV = 256
GLA_RANK = 16
GLA_TAU = 16.0
ATT_HEADS = 16
ATT_DH = 64
ATT_PAST = 8
REL_CLIP = 256
N_EXPERTS = 32
TOP_K = 4
SWIGLU_ALPHA = 1.702
SWIGLU_LIMIT = 7.0
EPS = 1e-6
LOG2E = 1.4426950408889634

VMEM_LIMIT = 48 * 1024 * 1024

IN_TM, IN_TN = 512, 1024
GLA_TT = 256
ATT_TQ = 256
ATT_STEP_HEADS = 16
OUT_TM = 1024
RANK_TB = 1024
EXP_TM = 512
DISPATCH_TOKENS = 2048
PLE_TB = 256
DMA_UNROLL = 8


def _cparams(sem):
    return pltpu.CompilerParams(dimension_semantics=sem, vmem_limit_bytes=VMEM_LIMIT)


def _split_bf16(a):
    hi = a.astype(BF16)
    lo = (a - hi.astype(F32)).astype(BF16)
    return hi, lo


def _dot_split(a, b):
    a_hi, a_lo = _split_bf16(a)
    b_hi, b_lo = _split_bf16(b)
    d = functools.partial(jnp.dot, preferred_element_type=F32)
    return d(a_hi, b_hi) + (d(a_hi, b_lo) + d(a_lo, b_hi))


def _rms(x, gain):
    ms = jnp.mean(x * x, axis=-1, keepdims=True)
    return x * lax.rsqrt(ms + EPS) * gain


def _store_token_tiles(ref, val):
    groups = val.shape[1] // LANES
    for c in range(groups):
        ref[pl.ds(c, val.shape[0], stride=groups), :] = val[:, c * LANES:(c + 1) * LANES]


def _load_token_tiles(ref, rows, groups):
    return jnp.concatenate([ref[pl.ds(c, rows, stride=groups), :] for c in range(groups)], axis=1)


_NT = (((1,), (1,)), ((), ()))
_TN = (((0,), (0,)), ((), ()))


def _in_proj_kernel(x_ref, g_ref, wa_ref, wb_ref, wlow_ref, z_ref, gk_ref):
    xn = _rms(x_ref[...], g_ref[...]).astype(BF16)
    gk_ref[...] = jnp.dot(xn, wlow_ref[...], preferred_element_type=F32)
    off = 0
    for w_ref in (wa_ref, wb_ref):
        for j in range(w_ref.shape[1] // IN_TN):
            z_ref[:, off:off + IN_TN] = jnp.dot(xn, w_ref[:, j * IN_TN:(j + 1) * IN_TN],
                                                preferred_element_type=F32).astype(BF16)
            off += IN_TN


def _in_proj(x2, ln, w_a, w_b, w_low):
    n, d = x2.shape
    ncol = w_a.shape[1] + w_b.shape[1]
    once = pl.Buffered(1)
    return pl.pallas_call(
        _in_proj_kernel,
        grid=(n // IN_TM,),
        in_specs=[
            pl.BlockSpec((IN_TM, d), lambda i: (i, 0)),
            pl.BlockSpec((1, d), lambda i: (0, 0)),
            pl.BlockSpec(w_a.shape, lambda i: (0, 0), pipeline_mode=once),
            pl.BlockSpec(w_b.shape, lambda i: (0, 0), pipeline_mode=once),
            pl.BlockSpec((d, LANES), lambda i: (0, 0), pipeline_mode=once),
        ],
        out_specs=[
            pl.BlockSpec((IN_TM, ncol), lambda i: (i, 0)),
            pl.BlockSpec((IN_TM, LANES), lambda i: (i, 0)),
        ],
        out_shape=[
            jax.ShapeDtypeStruct((n, ncol), BF16),
            jax.ShapeDtypeStruct((n, LANES), F32),
        ],
        compiler_params=_cparams(("parallel",)),
        name="in_proj",
    )(x2, ln, w_a, w_b, w_low)


_GLA_LEVELS = (8, 16, 32)
_GLA_BOT = 8


def _gla_consts(tt):
    t = np.arange(tt)
    same_chunk = (t[:, None] // CHUNK) == (t[None, :] // CHUNK)
    tri = (same_chunk & (t[None, :] <= t[:, None])).astype(np.float32)
    mlev = []
    for h in _GLA_LEVELS:
        blk = (t[:, None] // (2 * h)) == (t[None, :] // (2 * h))
        m = blk & ((t[:, None] % (2 * h)) >= h) & ((t[None, :] % (2 * h)) < h)
        mlev.append(m.astype(np.float32))
    mbot = (((t[None, :] // _GLA_BOT) == (t[:, None] // _GLA_BOT))
            & ((t[None, :] % _GLA_BOT) <= (t[:, None] % _GLA_BOT)))
    spread = (np.arange(_GLA_BOT * GLA_DK)[:, None] // GLA_DK) == (t[None, :] % _GLA_BOT)
    return (jnp.asarray(tri, BF16), jnp.asarray(np.stack(mlev), F32),
            jnp.asarray(mbot.astype(np.float32), F32), jnp.asarray(spread.astype(np.float32), BF16))


def _gla_kernel(q_ref, k_ref, v_ref, r_ref, ga_ref, gk_ref, wgk_ref, bgk_ref, gn_ref,
                tri_ref, mlev_ref, mbot_ref, ones_ref, o_ref, st_ref):
    @pl.when(pl.program_id(1) == 0)
    def _():
        st_ref[...] = jnp.zeros_like(st_ref)

    xg = _dot_split(gk_ref[...], wgk_ref[...]) + bgk_ref[...]
    g = -(jnp.maximum(-xg, 0.0) + jnp.log(1.0 + jnp.exp(-jnp.abs(xg)))) * (1.0 / GLA_TAU)
    g_hi, g_lo = _split_bf16(g)
    tri = tri_ref[...]
    b_all = (jnp.dot(tri, g_hi, preferred_element_type=F32)
             + jnp.dot(tri, g_lo, preferred_element_type=F32))

    for hh in range(GLA_HEADS):
        kc = slice(hh * GLA_DK, (hh + 1) * GLA_DK)
        vc = slice(hh * GLA_DV, (hh + 1) * GLA_DV)
        y, st = _gla_head(q_ref[:, kc], k_ref[:, kc], v_ref[:, vc], b_all[:, kc], st_ref[hh],
                          mlev_ref, mbot_ref, ones_ref[...])
        st_ref[hh] = st
        r = r_ref[:, vc].astype(F32)
        ya = _rms(y, gn_ref[...]) * (r * jax.nn.sigmoid(r))
        o_ref[:, vc] = (jax.nn.sigmoid(ga_ref[:, vc].astype(F32)) * ya).astype(BF16)


def _gla_head(q, k, v, b, st, mlev_ref, mbot_ref, ones):
    tt = q.shape[0]
    q = q.astype(F32) * (GLA_DK ** -0.5)
    k = k.astype(F32)
    b = b * LOG2E

    s_intra = jnp.zeros((tt, tt), F32)
    for li, h in enumerate(_GLA_LEVELS):
        b3 = b.reshape(tt // (2 * h), 2 * h, GLA_DK)
        e = jnp.exp2(-jnp.abs(b3 - b3[:, h - 1:h, :])).reshape(tt, GLA_DK)
        sc = lax.dot_general((q * e).astype(BF16), (k * e).astype(BF16), _NT,
                             preferred_element_type=F32)
        s_intra = s_intra + sc * mlev_ref[li]
    nb = tt // _GLA_BOT
    b3 = b.reshape(nb, _GLA_BOT, GLA_DK)
    q3 = q.reshape(nb, _GLA_BOT, GLA_DK)
    k3 = k.reshape(nb, _GLA_BOT, GLA_DK)
    prods = []
    for s in range(_GLA_BOT):
        e = jnp.exp2(jnp.minimum(b3 - b3[:, s:s + 1, :], 0.0))
        prods.append((q3 * k3[:, s:s + 1, :] * e).reshape(tt, GLA_DK).astype(BF16))
    s_intra = s_intra + jnp.dot(jnp.concatenate(prods, axis=1), ones,
                                preferred_element_type=F32) * mbot_ref[...]
    o_intra = jnp.dot(s_intra.astype(BF16), v, preferred_element_type=F32)

    outs = []
    for c in range(tt // CHUNK):
        lo = c * CHUNK
        bc = b[lo:lo + CHUNK]
        bl = b[lo + CHUNK - 1:lo + CHUNK]
        qe = (q[lo:lo + CHUNK] * jnp.exp2(bc)).astype(BF16)
        outs.append(lax.dot_general(qe, st.astype(BF16), _NT, preferred_element_type=F32))
        kd = (k[lo:lo + CHUNK] * jnp.exp2(bl - bc)).astype(BF16)
        upd = lax.dot_general(v[lo:lo + CHUNK], kd, _TN, preferred_element_type=F32)
        st = st * jnp.exp2(bl) + upd
    return o_intra + jnp.concatenate(outs, axis=0), st


def _gla(z3, gk3, wgk_pad, bgk, gnorm, col):
    b, t, _ = z3.shape
    tt = GLA_TT
    tri, mlev, mbot, ones = _gla_consts(tt)
    qk, gv = GLA_HEADS * GLA_DK, GLA_HEADS * GLA_DV

    def zspec(width, off):
        return pl.BlockSpec((None, tt, width), lambda bi, ti, o=off // width: (bi, ti, o))

    const2 = lambda bi, ti: (0, 0)
    const3 = lambda bi, ti: (0, 0, 0)
    return pl.pallas_call(
        _gla_kernel,
        grid=(b, t // tt),
        in_specs=[
            zspec(qk, col["q_g"]), zspec(qk, col["k_g"]), zspec(gv, col["v_g"]),
            zspec(gv, col["r_g"]), zspec(gv, col["gt_a"]),
            pl.BlockSpec((None, tt, LANES), lambda bi, ti: (bi, ti, 0)),
            pl.BlockSpec((LANES, qk), const2),
            pl.BlockSpec((1, qk), const2),
            pl.BlockSpec((1, GLA_DV), const2),
            pl.BlockSpec((tt, tt), const2),
            pl.BlockSpec((len(_GLA_LEVELS), tt, tt), const3),
            pl.BlockSpec((tt, tt), const2),
            pl.BlockSpec((_GLA_BOT * GLA_DK, tt), const2),
        ],
        out_specs=pl.BlockSpec((None, tt, gv), lambda bi, ti: (bi, ti, 0)),
        out_shape=jax.ShapeDtypeStruct((b, t, gv), BF16),
        scratch_shapes=[pltpu.VMEM((GLA_HEADS, GLA_DV, GLA_DK), F32)],
        compiler_params=_cparams(("parallel", "arbitrary")),
        name="gla",
    )(z3, z3, z3, z3, z3, gk3, wgk_pad, bgk, gnorm, tri, mlev, mbot, ones)


_ATT_NKB = 3


def _attn_bias(rel_bias):
    tq = ATT_TQ
    nk = _ATT_NKB * tq
    nheads = rel_bias.shape[0]
    band = (ATT_PAST + 1) * CHUNK
    assert nk - tq == ATT_PAST * CHUNK and tq % CHUNK == 0
    span = band + CHUNK - 1
    dist = np.clip(np.arange(span) - (CHUNK - 1), -REL_CLIP, REL_CLIP) + REL_CLIP
    g = rel_bias.astype(F32)[:, dist]
    x = jnp.pad(g[:, ::-1], ((0, 0), (0, 1)))
    base = jnp.tile(x, (1, CHUNK))[:, :CHUNK * span].reshape(nheads, CHUNK, span)[:, :, CHUNK - 1:CHUNK - 1 + band]
    tab = jnp.concatenate(
        [jnp.pad(base, ((0, 0), (0, 0), (ci * CHUNK, nk - band - ci * CHUNK)), constant_values=-jnp.inf)
         for ci in range(tq // CHUNK)], axis=1)
    present = np.stack([np.arange(nk) // tq >= _ATT_NKB - 1 - e for e in range(_ATT_NKB)])
    return jnp.where(jnp.asarray(present)[:, None, None, :], tab[None], -jnp.inf)


def _attn_kernel(q_ref, k0_ref, k1_ref, k2_ref, v0_ref, v1_ref, v2_ref, gb_ref, bias_ref, o_ref):
    tq = q_ref.shape[0]
    first = lax.broadcasted_iota(I32, (1, LANES), 1) < ATT_DH
    scale = jnp.asarray(ATT_DH ** -0.5, BF16)
    krefs = (k0_ref, k1_ref, k2_ref)
    vrefs = (v0_ref, v1_ref, v2_ref)
    for lb in range(q_ref.shape[1] // LANES):
        cols = slice(lb * LANES, (lb + 1) * LANES)
        q = q_ref[:, cols] * scale
        zero = jnp.zeros_like(q)
        qs = jnp.concatenate([jnp.where(first, q, zero), jnp.where(first, zero, q)], axis=0)
        s = jnp.concatenate(
            [lax.dot_general(qs, r[:, cols], _NT, preferred_element_type=F32) for r in krefs], axis=1)
        s = s + jnp.concatenate([bias_ref[2 * lb], bias_ref[2 * lb + 1]], axis=0)
        p = jnp.exp(s - jnp.max(s, axis=-1, keepdims=True))
        l = jnp.sum(p, axis=-1, keepdims=True)
        pb = p.astype(BF16)
        pv = functools.reduce(jnp.add, [
            jnp.dot(pb[:, i * tq:(i + 1) * tq], vrefs[i][:, cols], preferred_element_type=F32)
            for i in range(_ATT_NKB)]) / l
        o = jnp.where(first, pv[:tq], pv[tq:])
        o_ref[:, cols] = (jax.nn.sigmoid(gb_ref[:, cols].astype(F32)) * o).astype(BF16)


def _attn(z3, bias, col):
    b, t, _ = z3.shape
    tq = ATT_TQ
    width = ATT_STEP_HEADS * ATT_DH
    steps = ATT_HEADS // ATT_STEP_HEADS

    def cur(off):
        return pl.BlockSpec((None, tq, width), lambda h, j, bi, o=off // width: (bi, j, o + h))

    def past(off, back):
        return pl.BlockSpec((None, tq, width),
                            lambda h, j, bi, o=off // width: (bi, jnp.maximum(j - back, 0), o + h))

    return pl.pallas_call(
        _attn_kernel,
        grid=(steps, t // tq, b),
        in_specs=[
            cur(col["q_a"]),
            past(col["k_a"], 2), past(col["k_a"], 1), cur(col["k_a"]),
            past(col["v_a"], 2), past(col["v_a"], 1), cur(col["v_a"]),
            cur(col["gt_b"]),
            pl.BlockSpec((None, ATT_STEP_HEADS, tq, _ATT_NKB * tq),
                         lambda h, j, bi: (jnp.minimum(j, _ATT_NKB - 1), h, 0, 0)),
        ],
        out_specs=pl.BlockSpec((None, tq, width), lambda h, j, bi: (bi, j, h)),
        out_shape=jax.ShapeDtypeStruct((b, t, ATT_HEADS * ATT_DH), BF16),
        compiler_params=_cparams(("parallel", "parallel", "parallel")),
        name="band_attn",
    )(z3, z3, z3, z3, z3, z3, z3, z3, bias)


def _outproj_kernel(x_ref, ya_ref, yb_ref, wo_ref, lnm_ref, wr_ref, br_ref,
                    x1_ref, xm_ref, topi_ref, topw_ref, cnt_ref):
    h = (ya_ref[...].astype(F32) + yb_ref[...].astype(F32)).astype(BF16)
    x1 = x_ref[...] + jnp.dot(h, wo_ref[...], preferred_element_type=F32)
    x1_ref[...] = x1
    xm = _rms(x1, lnm_ref[...])
    _store_token_tiles(xm_ref, xm)

    tm = xm.shape[0]
    lane = lax.broadcasted_iota(I32, (tm, LANES), 1)
    lanef = lane.astype(F32)
    logits = _dot_split(xm, wr_ref[...]) + br_ref[...]
    l = jnp.where(lane < N_EXPERTS, logits, -jnp.inf)
    vals, idxs = [], []
    picked = jnp.zeros((tm, LANES), F32)
    for _ in range(TOP_K):
        m = jnp.max(l, axis=-1, keepdims=True)
        idx = jnp.min(jnp.where(l == m, lanef, float(LANES)), axis=-1, keepdims=True)
        vals.append(m)
        idxs.append(idx)
        hit = lanef == idx
        picked = jnp.where(hit, 1.0, picked)
        l = jnp.where(hit, -jnp.inf, l)
    es = [jnp.exp(vv - vals[0]) for vv in vals]
    tot = functools.reduce(jnp.add, es)
    topw = jnp.zeros((tm, LANES), F32)
    topi = jnp.zeros((tm, LANES), F32)
    for kk in range(TOP_K):
        topw = jnp.where(lane == kk, es[kk] / tot, topw)
        topi = jnp.where(lane == kk, idxs[kk], topi)
    topw_ref[...] = topw
    topi_ref[...] = topi.astype(I32)

    @pl.when(pl.program_id(0) == 0)
    def _():
        cnt_ref[...] = jnp.zeros_like(cnt_ref)

    cnt_ref[...] = cnt_ref[...] + jnp.sum(picked, axis=0, keepdims=True)


def _outproj(x2, ya, yb, w_out, ln_moe, wr_pad, br_pad):
    n, d = x2.shape
    tm = OUT_TM
    row = lambda i: (i, 0)
    const = lambda i: (0, 0)
    return pl.pallas_call(
        _outproj_kernel,
        grid=(n // tm,),
        in_specs=[
            pl.BlockSpec((tm, d), row), pl.BlockSpec((tm, d), row), pl.BlockSpec((tm, d), row),
            pl.BlockSpec((d, d), const), pl.BlockSpec((1, d), const),
            pl.BlockSpec((d, LANES), const), pl.BlockSpec((1, LANES), const),
        ],
        out_specs=[
            pl.BlockSpec((tm, d), row), pl.BlockSpec((tm * d // LANES, LANES), row),
            pl.BlockSpec((tm, LANES), row), pl.BlockSpec((tm, LANES), row),
            pl.BlockSpec((1, LANES), const),
        ],
        out_shape=[
            jax.ShapeDtypeStruct((n, d), F32), jax.ShapeDtypeStruct((n * d // LANES, LANES), F32),
            jax.ShapeDtypeStruct((n, LANES), I32), jax.ShapeDtypeStruct((n, LANES), F32),
            jax.ShapeDtypeStruct((1, LANES), F32),
        ],
        compiler_params=_cparams(("arbitrary",)),
        name="out_proj_router",
    )(x2, ya, yb, w_out, ln_moe, wr_pad, br_pad)


def _pos_kernel(topi_ref, start_ref, tri_ref, pos_ref, carry_ref):
    @pl.when(pl.program_id(0) == 0)
    def _():
        carry_ref[...] = start_ref[...]

    ti = topi_ref[...]
    tb = ti.shape[0]
    lane = lax.broadcasted_iota(I32, (tb, LANES), 1)
    sel = [lane == ti[:, kk:kk + 1] for kk in range(TOP_K)]
    oh = functools.reduce(jnp.add, [jnp.where(s, 1.0, 0.0) for s in sel])
    row = carry_ref[...] + jnp.dot(tri_ref[...], oh.astype(BF16), preferred_element_type=F32)
    out = jnp.zeros((tb, LANES), F32)
    for kk in range(TOP_K):
        rk = jnp.sum(jnp.where(sel[kk], row, 0.0), axis=-1, keepdims=True)
        out = jnp.where(lane == kk, rk, out)
    pos_ref[...] = jnp.transpose(out)[:pos_ref.shape[0]].astype(I32)
    carry_ref[...] = carry_ref[...] + jnp.sum(oh, axis=0, keepdims=True)


def _positions(topi, start_rows):
    n = topi.shape[0]
    tb = RANK_TB
    t = np.arange(tb)
    tri = jnp.asarray((t[None, :] < t[:, None]).astype(np.float32), BF16)
    return pl.pallas_call(
        _pos_kernel,
        grid=(n // tb,),
        in_specs=[pl.BlockSpec((tb, LANES), lambda i: (i, 0)),
                  pl.BlockSpec((1, LANES), lambda i: (0, 0)),
                  pl.BlockSpec((tb, tb), lambda i: (0, 0))],
        out_specs=pl.BlockSpec((SUBLANES, tb), lambda i: (0, i)),
        out_shape=jax.ShapeDtypeStruct((SUBLANES, n), I32),
        scratch_shapes=[pltpu.VMEM((1, LANES), F32)],
        compiler_params=_cparams(("arbitrary",)),
        name="route_positions",
    )(topi, start_rows, tri)


def _token_copy(src, src_tok, dst, dst_tok, sem):
    s0 = pl.multiple_of(src_tok * SUBLANES, SUBLANES)
    d0 = pl.multiple_of(dst_tok * SUBLANES, SUBLANES)
    return pltpu.make_async_copy(src.at[pl.ds(s0, SUBLANES)], dst.at[pl.ds(d0, SUBLANES)], sem)


def _tokens_wait(ref, ntok, sem):
    pltpu.make_async_copy(ref.at[pl.ds(0, ntok * SUBLANES)], ref.at[pl.ds(0, ntok * SUBLANES)], sem).wait()


def _dispatch_kernel(tv_ref, pos_hbm, xm_ref, xs_hbm, pos_smem, zeros_ref, sem_pos, sem_rows, sem_fill):
    i = pl.program_id(0)
    nsteps = pl.num_programs(0)
    tokens = xm_ref.shape[0] // SUBLANES
    ch = tokens * TOP_K
    slot = i % 2

    def pos_copy(step, sl):
        return pltpu.make_async_copy(pos_hbm.at[pl.ds(step * ch, ch)], pos_smem.at[pl.ds(sl * ch, ch)],
                                     sem_pos.at[sl])

    @pl.when(i == 0)
    def _():
        pos_copy(0, 0).start()
        tm = zeros_ref.shape[0]
        zeros_ref[...] = jnp.zeros_like(zeros_ref)

        def fill(t):
            return pltpu.make_async_copy(zeros_ref, xs_hbm.at[pl.ds(t * tm, tm)], sem_fill)

        def start(t, carry):
            @pl.when(tv_ref[t] < EXP_TM)
            def _():
                fill(t).start()
            return carry

        def wait(t, carry):
            @pl.when(tv_ref[t] < EXP_TM)
            def _():
                fill(t).wait()
            return carry

        ntiles = xs_hbm.shape[0] // tm
        lax.fori_loop(0, ntiles, start, 0)
        lax.fori_loop(0, ntiles, wait, 0)

    @pl.when(i + 1 < nsteps)
    def _():
        pos_copy(i + 1, 1 - slot).start()

    pos_copy(i, slot).wait()

    def body(t, carry):
        for kk in range(TOP_K):
            _token_copy(xm_ref, t, xs_hbm, pos_smem[slot * ch + kk * tokens + t],
                        sem_rows).start(priority=kk % 2)
        return carry

    lax.fori_loop(0, tokens, body, 0, unroll=DMA_UNROLL)
    _tokens_wait(xs_hbm, ch, sem_rows)


def _dispatch(tile_valid, pos_flat, xm, rows_total):
    tokens = DISPATCH_TOKENS
    n = xm.shape[0] // SUBLANES
    return pl.pallas_call(
        _dispatch_kernel,
        grid_spec=pltpu.PrefetchScalarGridSpec(
            num_scalar_prefetch=1,
            grid=(n // tokens,),
            in_specs=[pl.BlockSpec(memory_space=pl.ANY),
                      pl.BlockSpec((tokens * SUBLANES, LANES), lambda i, tv: (i, 0))],
            out_specs=pl.BlockSpec(memory_space=pl.ANY),
            scratch_shapes=[pltpu.SMEM((2 * tokens * TOP_K,), I32),
                            pltpu.VMEM((EXP_TM * SUBLANES, LANES), F32),
                            pltpu.SemaphoreType.DMA((2,)), pltpu.SemaphoreType.DMA,
                            pltpu.SemaphoreType.DMA],
        ),
        out_shape=jax.ShapeDtypeStruct((rows_total * SUBLANES, LANES), F32),
        compiler_params=_cparams(("arbitrary",)),
        name="dispatch",
    )(tile_valid, pos_flat, xm)


def _deinterleave_matrix():
    j = np.arange(2 * LANES)[:, None]
    c = np.arange(2 * LANES)[None, :]
    sel = np.where(c < LANES, j == 2 * c, j == 2 * (c - LANES) + 1)
    return jnp.asarray(sel.astype(np.float32), BF16)


def _expert_kernel(te_ref, tv_ref, slot_ref, next_ref, xs_ref, w1_hbm, b1g_ref, b1l_ref, w2_hbm, b2_ref,
                   sel_ref, ys_ref, w1g_ref, w1l_ref, w2p_ref, w1buf, w2buf, sem_w1, sem_w2):
    t = pl.program_id(0)
    nvalid = tv_ref[t]
    new_expert = (t == 0) | (te_ref[t] != te_ref[jnp.maximum(t - 1, 0)])

    def weight_copies(expert, sl):
        return (pltpu.make_async_copy(w1_hbm.at[expert], w1buf.at[sl], sem_w1.at[sl]),
                pltpu.make_async_copy(w2_hbm.at[expert], w2buf.at[sl], sem_w2.at[sl]))

    @pl.when((nvalid > 0) & new_expert)
    def _():
        sl = slot_ref[t]

        @pl.when(t == 0)
        def _():
            for cp in weight_copies(te_ref[t], sl):
                cp.start()

        for cp in weight_copies(te_ref[t], sl):
            cp.wait()

        @pl.when(next_ref[t] >= 0)
        def _():
            for cp in weight_copies(next_ref[t], 1 - sl):
                cp.start()

        sel = sel_ref[...]
        for m in range(w1g_ref.shape[1] // LANES):
            pair = w1buf[sl, :, 2 * m * LANES:(2 * m + 2) * LANES].astype(BF16)
            split = jnp.dot(pair, sel, preferred_element_type=F32).astype(BF16)
            w1g_ref[:, m * LANES:(m + 1) * LANES] = split[:, :LANES]
            w1l_ref[:, m * LANES:(m + 1) * LANES] = split[:, LANES:]
        w2p_ref[...] = w2buf[sl].astype(BF16)

    @pl.when(nvalid > 0)
    def _():
        tm = xs_ref.shape[0] // SUBLANES
        x = _load_token_tiles(xs_ref, tm, SUBLANES).astype(BF16)
        hg = jnp.dot(x, w1g_ref[...], preferred_element_type=F32) + b1g_ref[...]
        hl = jnp.dot(x, w1l_ref[...], preferred_element_type=F32) + b1l_ref[...]
        glu = jnp.minimum(hg, SWIGLU_LIMIT)
        lin = jnp.clip(hl, -SWIGLU_LIMIT, SWIGLU_LIMIT)
        act = glu * jax.nn.sigmoid(SWIGLU_ALPHA * glu) * (lin + 1.0)
        y = jnp.dot(act.astype(BF16), w2p_ref[...], preferred_element_type=F32) + b2_ref[...]
        _store_token_tiles(ys_ref, y)

    @pl.when(nvalid <= 0)
    def _():
        ys_ref[...] = jnp.zeros_like(ys_ref)


def _experts(tile_expert, tile_valid, tile_slot, tile_next, xs, w1, b1g, b1l, w2, b2):
    f, d = w2.shape[1:]
    p = xs.shape[0] // SUBLANES
    tm = EXP_TM
    wmap = lambda t, te, tv, ts, tn: (te[t], 0, 0)
    tile = lambda t, te, tv, ts, tn: (t, 0)
    return pl.pallas_call(
        _expert_kernel,
        grid_spec=pltpu.PrefetchScalarGridSpec(
            num_scalar_prefetch=4,
            grid=(p // tm,),
            in_specs=[
                pl.BlockSpec((tm * SUBLANES, LANES), tile),
                pl.BlockSpec(memory_space=pl.ANY),
                pl.BlockSpec((None, 1, f), wmap), pl.BlockSpec((None, 1, f), wmap),
                pl.BlockSpec(memory_space=pl.ANY), pl.BlockSpec((None, 1, d), wmap),
                pl.BlockSpec((2 * LANES, 2 * LANES), lambda t, te, tv, ts, tn: (0, 0)),
            ],
            out_specs=pl.BlockSpec((tm * SUBLANES, LANES), tile),
            scratch_shapes=[pltpu.VMEM((d, f), BF16), pltpu.VMEM((d, f), BF16),
                            pltpu.VMEM((f, d), BF16),
                            pltpu.VMEM((2, d, 2 * f), F32), pltpu.VMEM((2, f, d), F32),
                            pltpu.SemaphoreType.DMA((2,)), pltpu.SemaphoreType.DMA((2,))],
        ),
        out_shape=jax.ShapeDtypeStruct((p * SUBLANES, LANES), F32),
        compiler_params=_cparams(("arbitrary",)),
        name="experts",
    )(tile_expert, tile_valid, tile_slot, tile_next, xs, w1, b1g, b1l, w2, b2, _deinterleave_matrix())


def _ple_kernel(pos_hbm, ys_hbm, topw_ref, x1_ref, p_ref, lnp_ref, wpg_ref, wpp_ref, lnf_ref, o_ref,
                pos0, pos1, ybuf0, ybuf1, sem_pos, sem_rows):
    i = pl.program_id(0)
    nsteps = pl.num_programs(0)
    tb = x1_ref.shape[0] // 2
    ch = tb * TOP_K
    nblocks = 2 * nsteps
    pos_bufs = (pos0, pos1)
    ybufs = (ybuf0, ybuf1)

    def pos_copy(block, par):
        blk = jnp.minimum(block, nblocks - 1)
        return pltpu.make_async_copy(pos_hbm.at[pl.ds(blk * ch, ch)], pos_bufs[par], sem_pos.at[par])

    def issue_gathers(par):
        def body(t, carry):
            for kk in range(TOP_K):
                _token_copy(ys_hbm, pos_bufs[par][kk * tb + t], ybufs[par].at[kk], t,
                            sem_rows.at[par]).start(priority=1)
            return carry
        lax.fori_loop(0, tb, body, 0, unroll=True)

    def wait_gathers(par):
        for kk in range(TOP_K):
            _tokens_wait(ybufs[par].at[kk], tb, sem_rows.at[par])

    def compute(par):
        rows = slice(par * tb, (par + 1) * tb)
        topw = topw_ref[rows, :]
        groups = x1_ref.shape[1] // LANES
        moe = functools.reduce(jnp.add, [
            topw[:, kk:kk + 1] * _load_token_tiles(ybufs[par].at[kk], tb, groups)
            for kk in range(TOP_K)])
        x2 = x1_ref[rows, :] + moe
        gate = jax.nn.sigmoid(jnp.dot(_rms(x2, lnp_ref[...]).astype(BF16), wpg_ref[...],
                                      preferred_element_type=F32))
        proj = jnp.dot(p_ref[rows, :].astype(BF16), wpp_ref[...], preferred_element_type=F32)
        o_ref[rows, :] = _rms(x2 + gate * proj, lnf_ref[...])

    @pl.when(i == 0)
    def _():
        pos_copy(0, 0).start()
        pos_copy(0, 0).wait()
        issue_gathers(0)
        pos_copy(1, 1).start()

    pos_copy(2 * i + 1, 1).wait()
    wait_gathers(0)
    pos_copy(2 * i + 2, 0).start()
    issue_gathers(1)
    compute(0)
    pos_copy(2 * i + 2, 0).wait()
    wait_gathers(1)
    pos_copy(2 * i + 3, 1).start()
    issue_gathers(0)
    compute(1)

    @pl.when(i == nsteps - 1)
    def _():
        wait_gathers(0)
        pos_copy(0, 1).wait()


def _ple(pos_flat, ys, topw, x1, p2, ln_ple, wpg, wpp, ln_final):
    n, d = x1.shape
    pd = p2.shape[1]
    tb = PLE_TB
    row = lambda i: (i, 0)
    const = lambda i: (0, 0)
    return pl.pallas_call(
        _ple_kernel,
        grid=(n // (2 * tb),),
        in_specs=[
            pl.BlockSpec(memory_space=pl.ANY), pl.BlockSpec(memory_space=pl.ANY),
            pl.BlockSpec((2 * tb, LANES), row), pl.BlockSpec((2 * tb, d), row),
            pl.BlockSpec((2 * tb, pd), row),
            pl.BlockSpec((1, d), const), pl.BlockSpec((d, d), const),
            pl.BlockSpec((pd, d), const), pl.BlockSpec((1, d), const),
        ],
        out_specs=pl.BlockSpec((2 * tb, d), row),
        out_shape=jax.ShapeDtypeStruct((n, d), F32),
        scratch_shapes=[
            pltpu.SMEM((tb * TOP_K,), I32), pltpu.SMEM((tb * TOP_K,), I32),
            pltpu.VMEM((TOP_K, tb * d // LANES, LANES), F32),
            pltpu.VMEM((TOP_K, tb * d // LANES, LANES), F32),
            pltpu.SemaphoreType.DMA((2,)),
            pltpu.SemaphoreType.DMA((2,)),
        ],
        compiler_params=_cparams(("arbitrary",)),
        name="combine_ple_final",
    )(pos_flat, ys, topw, x1, p2, ln_ple, wpg, wpp, ln_final)


def _layer(x2, p2, bsz, seq, ln_mix, w_in, w_gk, b_gk, gla_norm, rel_bias, w_out, ln_moe,
           w_router, b_router, w1, b1, w2, b2, ln_ple, w_ple_gate, w_ple_proj, ln_out):
    n, d = x2.shape
    qk, gv, aw = GLA_HEADS * GLA_DK, GLA_HEADS * GLA_DV, ATT_HEADS * ATT_DH
    names = ("q_g", "k_g", "v_g", "gk_low", "r_g", "q_a", "k_a", "v_a", "gt_a", "gt_b")
    widths = (qk, qk, gv, GLA_RANK, gv, aw, aw, aw, d, d)
    src = dict(zip(names, np.cumsum((0,) + widths[:-1]).tolist()))
    wid = dict(zip(names, widths))
    order = [nm for nm in names if nm != "gk_low"]
    col, off = {}, 0
    for nm in order:
        col[nm] = off
        off += wid[nm]
    lo, hi = src["gk_low"], src["gk_low"] + GLA_RANK
    assert lo % IN_TN == 0 and (w_in.shape[1] - hi) % IN_TN == 0
    w_low = jnp.pad(w_in[:, lo:hi], ((0, 0), (0, LANES - GLA_RANK))).astype(BF16)

    z, gk = _in_proj(x2, ln_mix.reshape(1, d), w_in[:, :lo].astype(BF16), w_in[:, hi:].astype(BF16), w_low)
    z3 = z.reshape(bsz, seq, -1)
    gk3 = gk.reshape(bsz, seq, LANES)

    wgk_pad = jnp.pad(w_gk, ((0, LANES - GLA_RANK), (0, 0)))
    ya = _gla(z3, gk3, wgk_pad, b_gk.reshape(1, qk), gla_norm.reshape(1, GLA_DV), col)
    yb = _attn(z3, _attn_bias(rel_bias), col)

    wr_pad = jnp.pad(w_router, ((0, 0), (0, LANES - N_EXPERTS)))
    br_pad = jnp.pad(b_router, (0, LANES - N_EXPERTS)).reshape(1, LANES)
    x1, xm, topi, topw, cnt = _outproj(x2, ya.reshape(n, d), yb.reshape(n, d), w_out.astype(BF16),
                                       ln_moe.reshape(1, d), wr_pad, br_pad)

    counts = cnt[0, :N_EXPERTS].astype(I32)
    ntile = (counts + EXP_TM - 1) // EXP_TM
    tile_end = jnp.cumsum(ntile)
    tile_start = tile_end - ntile
    rows_total = n * TOP_K + N_EXPERTS * EXP_TM
    tiles = jnp.arange(rows_total // EXP_TM, dtype=I32)
    onehot = (tiles[:, None] >= tile_start[None, :]) & (tiles[:, None] < tile_end[None, :])
    te = jnp.sum(jnp.where(onehot, jnp.arange(N_EXPERTS, dtype=I32)[None, :], 0), axis=1)
    tv = jnp.sum(jnp.where(onehot, counts[None, :] - (tiles[:, None] - tile_start[None, :]) * EXP_TM, 0),
                 axis=1)
    te = jnp.where(tiles < tile_end[-1], te, N_EXPERTS - 1).astype(I32)
    tv = jnp.clip(tv, 0, EXP_TM).astype(I32)
    start_rows = jnp.pad((tile_start * EXP_TM).astype(F32), (0, LANES - N_EXPERTS)).reshape(1, LANES)
    active = ntile > 0
    eid = jnp.arange(N_EXPERTS, dtype=I32)
    slot_e = (jnp.cumsum(active.astype(I32)) - 1) % 2
    later = (eid[None, :] > eid[:, None]) & active[None, :]
    next_e = jnp.min(jnp.where(later, eid[None, :], N_EXPERTS), axis=1)
    next_e = jnp.where(next_e < N_EXPERTS, next_e, -1)
    tslot = slot_e[te].astype(I32)
    tnext = next_e[te].astype(I32)

    pos_t = _positions(topi, start_rows)[:TOP_K]

    def pos_blocks(tokens):
        return pos_t.reshape(TOP_K, n // tokens, tokens).transpose(1, 0, 2).reshape(-1)

    xs = _dispatch(tv, pos_blocks(DISPATCH_TOKENS), xm, rows_total)

    ys = _experts(te, tv, tslot, tnext, xs, w1,
                  b1[:, 0::2].reshape(N_EXPERTS, 1, -1), b1[:, 1::2].reshape(N_EXPERTS, 1, -1),
                  w2, b2.reshape(N_EXPERTS, 1, d))

    return _ple(pos_blocks(PLE_TB), ys, topw, x1, p2, ln_ple.reshape(1, d), w_ple_gate.astype(BF16),
                w_ple_proj.astype(BF16), ln_out.reshape(1, d))


def kernel(x, p, ln_mix, w_in, w_gk, b_gk, gla_norm, rel_bias, w_out, ln_moe, w_router, b_router,
           w1, b1, w2, b2, ln_ple, w_ple_gate, w_ple_proj, ln_final):
    bsz, seq, d = x.shape
    depth = p.shape[0]
    assert depth == 1, "the final RMSNorm is fused into the last layer's kernel"
    assert d == SUBLANES * LANES, "token-tile layout needs one vreg tile per token"
    assert seq % max(ATT_TQ, GLA_TT) == 0
    assert (bsz * seq) % max(IN_TM, OUT_TM, RANK_TB, DISPATCH_TOKENS, 2 * PLE_TB) == 0
    x2 = x.reshape(bsz * seq, d)
    out = _layer(x2, p[0].reshape(bsz * seq, -1), bsz, seq, ln_mix[0], w_in[0], w_gk[0], b_gk[0],
                 gla_norm[0], rel_bias[0], w_out[0], ln_moe[0], w_router[0], b_router[0],
                 w1[0], b1[0], w2[0], b2[0], ln_ple[0], w_ple_gate[0], w_ple_proj[0], ln_final)
    return out.reshape(bsz, seq, d)
```
